```python
import math
import jax
import jax.numpy as jnp
from jax import lax
import numpy as np


D_MODEL = 4096
BATCH = 4
SEQ = 2048
DEPTH = 1

CHUNK = 64
EPS = 1e-6
N_MOD = 6
A_HEADS = 16
A_HEAD_DIM = 128
A_Q_RANK = 1024
A_KV_RANK = 512
A_WIDTH = A_HEADS * A_HEAD_DIM
IDX_HEADS = 64
IDX_DIM = 128
TOPK_MAX = 256
Q_BLOCK = 128
T5_BUCKETS = 32
T5_MAX_DIST = 128
R_HEADS = 8
R_QK_DIM = 128
R_V_DIM = 256
R_QK_WIDTH = R_HEADS * R_QK_DIM
R_WIDTH = R_HEADS * R_V_DIM
ROT_BASE = 10000.0
P_HEADS = 8
P_QUERY_DIM = 256
P_NKEYS = 128
P_TOPK = 16
P_EXPERTS = P_NKEYS * P_NKEYS
P_TOKEN_BLOCK = 128
IN_SPLITS = (A_Q_RANK, A_KV_RANK, IDX_DIM, IDX_HEADS, R_QK_WIDTH, R_QK_WIDTH, R_WIDTH, R_WIDTH)
IN_WIDTH = A_Q_RANK + A_KV_RANK + IDX_DIM + IDX_HEADS + 2 * R_QK_WIDTH + 2 * R_WIDTH
N_BRANCH = 2

kernel_name = 'hybrid_dsa_retention_peer_block'


def rms_norm(x, g):
    xf = x.astype(jnp.float32)
    y = xf * lax.rsqrt(jnp.mean(xf * xf, axis=-1, keepdims=True) + EPS)
    return (y * g).astype(x.dtype)


def layer_norm(x, g, b):
    xf = x.astype(jnp.float32)
    mu = jnp.mean(xf, axis=-1, keepdims=True)
    var = jnp.mean(jnp.square(xf - mu), axis=-1, keepdims=True)
    y = (xf - mu) * lax.rsqrt(var + EPS)
    return (y * g + b).astype(x.dtype)


def modulate(h, shift, scale):
    return h * (1.0 + scale[:, None, :]) + shift[:, None, :]


def t5_bucket(rel):
    half = T5_BUCKETS // 2
    exact = half // 2
    n = jnp.abs(rel)
    log_ratio = jnp.log(jnp.maximum(n, 1).astype(jnp.float32) / exact) / math.log(T5_MAX_DIST / exact)
    large = jnp.minimum(exact + (log_ratio * (half - exact)).astype(jnp.int32), half - 1)
    return jnp.where(rel > 0, half, 0) + jnp.where(n < exact, n, large)


def dsa_branch(c_q, c_kv, k_i, w_i, g_cq, g_ckv, w_uq, w_uk, w_uv, w_qi, g_ki, b_ki, t5_bias, top_k):
    B, S, _ = c_q.shape
    c_q = rms_norm(c_q, g_cq)
    c_kv = rms_norm(c_kv, g_ckv)
    q = jnp.einsum('bsr,rhd->bshd', c_q, w_uq)
    q_lat = jnp.einsum('bshd,chd->bshc', q, w_uk)
    q_idx = jnp.einsum('bsr,rhd->bshd', c_q, w_qi)
    k_idx = layer_norm(k_i, g_ki, b_ki)
    w_idx = w_i * (IDX_HEADS ** -0.5 * IDX_DIM ** -0.5)
    kchunk = jnp.arange(S, dtype=jnp.int32) // CHUNK

    def block(i):
        q0 = i * Q_BLOCK
        qi = lax.dynamic_slice_in_dim(q_idx, q0, Q_BLOCK, axis=1)
        wi = lax.dynamic_slice_in_dim(w_idx, q0, Q_BLOCK, axis=1)
        ql = lax.dynamic_slice_in_dim(q_lat, q0, Q_BLOCK, axis=1)
        qpos = q0 + jnp.arange(Q_BLOCK, dtype=jnp.int32)
        qchunk = qpos // CHUNK
        rel = jax.nn.relu(jnp.einsum('bthd,bsd->bths', qi, k_idx))
        score = jnp.einsum('bths,bth->bts', rel, wi).astype(jnp.float32)
        adm = kchunk[None, :] <= qchunk[:, None]
        score = jnp.where(adm[None], score, -jnp.inf)
        _, sel = lax.top_k(score, top_k)
        valid = (sel // CHUNK) <= qchunk[None, :, None]
        kv_sel = jax.vmap(lambda kv, ix: kv[ix])(c_kv, sel)
        logits = jnp.einsum('bthc,btkc->bthk', ql, kv_sel).astype(jnp.float32) * (A_HEAD_DIM ** -0.5)
        bias = t5_bias[t5_bucket(sel - qpos[None, :, None])]
        logits = logits + jnp.moveaxis(bias, -1, 2).astype(jnp.float32)
        logits = jnp.where(valid[:, :, None, :], logits, -jnp.inf)
        p = jax.nn.softmax(logits, axis=-1).astype(kv_sel.dtype)
        o_lat = jnp.einsum('bthk,btkc->bthc', p, kv_sel)
        o = jnp.einsum('bthc,chd->bthd', o_lat, w_uv)
        return o.reshape(B, Q_BLOCK, A_WIDTH)

    out = lax.map(block, jnp.arange(S // Q_BLOCK))
    return jnp.moveaxis(out, 0, 1).reshape(B, S, A_WIDTH)


def retention_branch(r_q, r_k, r_v, r_g, g_ret):
    B, S, _ = r_q.shape
    f32 = jnp.float32
    NC = S // CHUNK
    half = R_QK_DIM // 2
    q = r_q.astype(f32).reshape(B, S, R_HEADS, R_QK_DIM)
    k = r_k.astype(f32).reshape(B, S, R_HEADS, R_QK_DIM) * (R_QK_DIM ** -0.5)
    v = r_v.astype(f32).reshape(B, S, R_HEADS, R_V_DIM)
    inv = 1.0 / (ROT_BASE ** jnp.linspace(0.0, 1.0, half, dtype=f32))
    ang = jnp.arange(S, dtype=f32)[:, None] * inv[None, :]
    cos = jnp.cos(ang)[None, :, None, :]
    sin = jnp.sin(ang)[None, :, None, :]

    def rot(t):
        t1, t2 = t[..., :half], t[..., half:]
        return jnp.concatenate([t1 * cos - t2 * sin, t1 * sin + t2 * cos], axis=-1)

    q = rot(q).reshape(B, NC, CHUNK, R_HEADS, R_QK_DIM)
    k = rot(k).reshape(B, NC, CHUNK, R_HEADS, R_QK_DIM)
    v = v.reshape(B, NC, CHUNK, R_HEADS, R_V_DIM)
    log_g = jnp.log(1.0 - jnp.power(2.0, -5.0 - jnp.arange(R_HEADS, dtype=f32)))
    j = jnp.arange(CHUNK, dtype=f32)
    diff = j[:, None] - j[None, :]
    dmask = jnp.where(diff[None] >= 0, jnp.exp(jnp.maximum(diff, 0.0)[None] * log_g[:, None, None]), 0.0)
    att = jnp.einsum('bnjhk,bnlhk->bnhjl', q, k) * dmask[None, None]
    y_intra = jnp.einsum('bnhjl,bnlhv->bnjhv', att, v)
    k_dec = k * jnp.exp((CHUNK - 1.0 - j)[:, None] * log_g[None, :])[None, None, :, :, None]
    kv = jnp.einsum('bnlhk,bnlhv->nbhkv', k_dec, v)
    chunk_dec = jnp.exp(CHUNK * log_g)[None, :, None, None]

    def step(state, kv_n):
        return state * chunk_dec + kv_n, state

    _, s_prev = lax.scan(step, jnp.zeros((B, R_HEADS, R_QK_DIM, R_V_DIM), f32), kv)
    q_dec = q * jnp.exp((j + 1.0)[:, None] * log_g[None, :])[None, None, :, :, None]
    y_cross = jnp.einsum('bnjhk,nbhkv->bnjhv', q_dec, s_prev)
    y = (y_intra + y_cross).reshape(B, S, R_HEADS, R_V_DIM)
    y = y * lax.rsqrt(jnp.mean(y * y, axis=-1, keepdims=True) + EPS) * g_ret
    y = jax.nn.silu(r_g.astype(f32)) * y.reshape(B, S, R_WIDTH)
    return y.astype(r_q.dtype)


def mixer_block(h, w_in, g_cq, g_ckv, w_uq, w_uk, w_uv, w_qi, g_ki, b_ki, t5_bias, g_ret,
                w_up, w_gate, b_gate, w_out, top_k):
    proj = jnp.einsum('bsd,de->bse', h, w_in)
    offs = np.cumsum(IN_SPLITS)[:-1].tolist()
    c_q, c_kv, k_i, w_i, r_q, r_k, r_v, r_g = jnp.split(proj, offs, axis=-1)
    y_a = dsa_branch(c_q, c_kv, k_i, w_i, g_cq, g_ckv, w_uq, w_uk, w_uv, w_qi, g_ki, b_ki, t5_bias, top_k)
    y_r = retention_branch(r_q, r_k, r_v, r_g, g_ret)
    p_a = jnp.einsum('bse,ed->bsd', y_a, w_up[:A_WIDTH])
    p_r = jnp.einsum('bse,ed->bsd', y_r, w_up[A_WIDTH:])
    gates = jax.nn.sigmoid(jnp.einsum('bsd,de->bse', h, w_gate) + b_gate)
    g_a, g_r = jnp.split(gates, N_BRANCH, axis=-1)
    merged = g_a * p_a + g_r * p_r
    return jnp.einsum('bsd,de->bse', merged, w_out)


def peer_ffn(h, w_pq, sub_keys, u_exp, v_exp):
    B, S, D = h.shape
    hb = h.reshape(B * S // P_TOKEN_BLOCK, P_TOKEN_BLOCK, D)
    hq = P_QUERY_DIM // 2

    def block(t):
        q = jnp.einsum('td,de->te', t, w_pq).reshape(P_TOKEN_BLOCK, P_HEADS, P_QUERY_DIM)
        s1 = jnp.einsum('thd,kd->thk', q[..., :hq], sub_keys[0]).astype(jnp.float32)
        s2 = jnp.einsum('thd,kd->thk', q[..., hq:], sub_keys[1]).astype(jnp.float32)
        v1, i1 = lax.top_k(s1, P_TOPK)
        v2, i2 = lax.top_k(s2, P_TOPK)
        cand = (v1[..., :, None] + v2[..., None, :]).reshape(P_TOKEN_BLOCK, P_HEADS, P_TOPK * P_TOPK)
        cid = (i1[..., :, None] * P_NKEYS + i2[..., None, :]).reshape(P_TOKEN_BLOCK, P_HEADS, P_TOPK * P_TOPK)
        top, pidx = lax.top_k(cand, P_TOPK)
        eid = jnp.take_along_axis(cid, pidx, axis=-1)
        gate = jax.nn.softmax(top, axis=-1)
        u = u_exp[eid]
        act = jax.nn.gelu(jnp.einsum('thed,td->the', u, t).astype(jnp.float32), approximate=False)
        coef = (gate * act).astype(t.dtype)
        return jnp.einsum('the,thed->td', coef, v_exp[eid])

    return lax.map(block, hb).reshape(B, S, D)


def setup_inputs(seed: int = 0) -> dict:
    key = jax.random.key(seed)
    ks = jax.random.split(key, 32)
    f32 = jnp.float32
    L = DEPTH
    D = D_MODEL

    def nrm(k, shape, scale):
        return jax.random.normal(k, shape, f32) * scale

    return {
        'x': nrm(ks[0], (BATCH, SEQ, D), 1.0),
        'c': nrm(ks[1], (BATCH, D), 1.0),
        'w_ada': nrm(ks[2], (L, D, N_MOD * D), 0.5 * D ** -0.5),
        'b_ada': nrm(ks[3], (L, N_MOD * D), 0.02),
        'g_mix': 1.0 + nrm(ks[4], (L, D), 0.02),
        'w_in': nrm(ks[5], (L, D, IN_WIDTH), D ** -0.5),
        'g_cq': 1.0 + nrm(ks[6], (L, A_Q_RANK), 0.02),
        'g_ckv': 1.0 + nrm(ks[7], (L, A_KV_RANK), 0.02),
        'w_uq': nrm(ks[8], (L, A_Q_RANK, A_HEADS, A_HEAD_DIM), A_Q_RANK ** -0.5),
        'w_uk': nrm(ks[9], (L, A_KV_RANK, A_HEADS, A_HEAD_DIM), A_HEAD_DIM ** -0.5),
        'w_uv': nrm(ks[10], (L, A_KV_RANK, A_HEADS, A_HEAD_DIM), A_KV_RANK ** -0.5),
        'w_qi': nrm(ks[11], (L, A_Q_RANK, IDX_HEADS, IDX_DIM), A_Q_RANK ** -0.5),
        'g_ki': 1.0 + nrm(ks[12], (L, IDX_DIM), 0.02),
        'b_ki': nrm(ks[13], (L, IDX_DIM), 0.02),
        't5_bias': nrm(ks[14], (T5_BUCKETS, A_HEADS), 0.3),
        'g_ret': 1.0 + nrm(ks[15], (L, R_HEADS, R_V_DIM), 0.02),
        'w_up': nrm(ks[16], (L, A_WIDTH + R_WIDTH, D), A_WIDTH ** -0.5),
        'w_gate': nrm(ks[17], (L, D, N_BRANCH * D), D ** -0.5),
        'b_gate': nrm(ks[18], (L, N_BRANCH * D), 0.02),
        'w_out': nrm(ks[19], (L, D, D), D ** -0.5),
        'g_ffn': 1.0 + nrm(ks[20], (L, D), 0.02),
        'w_pq': nrm(ks[21], (L, D, P_HEADS * P_QUERY_DIM), D ** -0.5),
        'sub_keys': nrm(ks[22], (L, 2, P_NKEYS, P_QUERY_DIM // 2), (P_QUERY_DIM // 2) ** -0.5),
        'u_exp': nrm(ks[23], (L, P_EXPERTS, D), D ** -0.5),
        'v_exp': nrm(ks[24], (L, P_EXPERTS, D), P_HEADS ** -0.5),
        'g_final': 1.0 + nrm(ks[25], (D,), 0.02),
    }


def reference(x, c, w_ada, b_ada, g_mix, w_in, g_cq, g_ckv, w_uq, w_uk, w_uv, w_qi, g_ki, b_ki,
              t5_bias, g_ret, w_up, w_gate, b_gate, w_out, g_ffn, w_pq, sub_keys, u_exp, v_exp, g_final):
    S = x.shape[1]
    top_k = min(TOPK_MAX, S // 4)
    c_act = jax.nn.silu(c)
    for l in range(DEPTH):
        mod = jnp.einsum('bd,de->be', c_act, w_ada[l]) + b_ada[l]
        sh_m, sc_m, gt_m, sh_f, sc_f, gt_f = jnp.split(mod, N_MOD, axis=-1)
        h = modulate(rms_norm(x, g_mix[l]), sh_m, sc_m)
        y = mixer_block(h, w_in[l], g_cq[l], g_ckv[l], w_uq[l], w_uk[l], w_uv[l], w_qi[l], g_ki[l],
                        b_ki[l], t5_bias, g_ret[l], w_up[l], w_gate[l], b_gate[l], w_out[l], top_k)
        x = x + gt_m[:, None, :] * y
        h = modulate(rms_norm(x, g_ffn[l]), sh_f, sc_f)
        x = x + gt_f[:, None, :] * peer_ffn(h, w_pq[l], sub_keys[l], u_exp[l], v_exp[l])
    return rms_norm(x, g_final)
```

```python
import functools
import math

import numpy as np
import jax
import jax.numpy as jnp
from jax import lax
from jax.experimental import pallas as pl
from jax.experimental.pallas import tpu as pltpu

f32 = jnp.float32
bf16 = jnp.bfloat16
i32 = jnp.int32

D_MODEL = 4096
BATCH = 4
SEQ = 2048
TOKENS = BATCH * SEQ
CHUNK = 64
EPS = 1e-6
N_MOD = 6
A_HEADS = 16
A_HEAD_DIM = 128
A_Q_RANK = 1024
A_KV_RANK = 512
A_WIDTH = A_HEADS * A_HEAD_DIM
IDX_HEADS = 64
IDX_DIM = 128
TOPK = 256
T5_BUCKETS = 32
T5_MAX_DIST = 128
R_HEADS = 8
R_QK_DIM = 128
R_V_DIM = 256
R_QK_WIDTH = R_HEADS * R_QK_DIM
R_WIDTH = R_HEADS * R_V_DIM
ROT_BASE = 10000.0
P_HEADS = 8
P_QUERY_DIM = 256
P_NKEYS = 128
P_TOPK = 16
P_EXPERTS = P_NKEYS * P_NKEYS

PROJ_WIDTH = 8192
OFF_CQ, OFF_CKV, OFF_KI, OFF_WI = 0, 1024, 1536, 1664
OFF_RQ, OFF_RK, OFF_RV, OFF_RG = 2048, 3072, 4096, 6144

Q_BLOCK = 128
NEAR = 2 * Q_BLOCK
VMEM_LIMIT = 56 * 1024 * 1024
INT_MIN = -2147483648


def _params(*sem):
    return pltpu.CompilerParams(dimension_semantics=sem, vmem_limit_bytes=VMEM_LIMIT)


def _ada_kernel(c_ref, w_ref, b_ref, o_ref):
    c = c_ref[...]
    ca = (c * jax.nn.sigmoid(c)).astype(bf16)
    o_ref[...] = jnp.dot(ca, w_ref[...].astype(bf16), preferred_element_type=f32) + b_ref[...]


def _ada(c8, w, b):
    n = w.shape[1]
    tn = 512
    return pl.pallas_call(
        _ada_kernel,
        grid=(n // tn,),
        in_specs=[pl.BlockSpec((8, D_MODEL), lambda j: (0, 0)),
                  pl.BlockSpec((D_MODEL, tn), lambda j: (0, j)),
                  pl.BlockSpec((1, tn), lambda j: (0, j))],
        out_specs=pl.BlockSpec((8, tn), lambda j: (0, j)),
        out_shape=jax.ShapeDtypeStruct((8, n), f32),
        compiler_params=_params("arbitrary"),
        name="ada_mod",
    )(c8, w, b)


def _normmod_kernel(x_ref, g_ref, sh_ref, sc_ref, o_ref):
    x = x_ref[...]
    y = x * lax.rsqrt(jnp.mean(x * x, axis=-1, keepdims=True) + EPS) * g_ref[...]
    o_ref[...] = (y * (1.0 + sc_ref[0]) + sh_ref[0]).astype(o_ref.dtype)


def _normmod(x2d, g, modr, shift_slot, scale_slot):
    ts = 256
    per_b = SEQ // ts
    return pl.pallas_call(
        _normmod_kernel,
        grid=(TOKENS // ts,),
        in_specs=[pl.BlockSpec((ts, D_MODEL), lambda i: (i, 0)),
                  pl.BlockSpec((1, D_MODEL), lambda i: (0, 0)),
                  pl.BlockSpec((1, 1, D_MODEL), lambda i: ((i // per_b) * N_MOD + shift_slot, 0, 0)),
                  pl.BlockSpec((1, 1, D_MODEL), lambda i: ((i // per_b) * N_MOD + scale_slot, 0, 0))],
        out_specs=pl.BlockSpec((ts, D_MODEL), lambda i: (i, 0)),
        out_shape=jax.ShapeDtypeStruct((TOKENS, D_MODEL), bf16),
        compiler_params=_params("arbitrary"),
        name="norm_modulate",
    )(x2d, g, modr, modr)


def _mm_kernel(a_ref, w_ref, *rest, epilogue):
    o_ref = rest[-1]
    acc = jnp.dot(a_ref[...], w_ref[...], preferred_element_type=f32)
    o_ref[...] = epilogue(acc, *rest[:-1]).astype(o_ref.dtype)


def _matmul(a, w, out_dtype, *, tm, tn, epilogue=None, extra=(), extra_specs=(), name):
    m, k = a.shape
    n = w.shape[1]
    if epilogue is None:
        epilogue = lambda acc: acc
    return pl.pallas_call(
        functools.partial(_mm_kernel, epilogue=epilogue),
        grid=(m // tm, n // tn),
        in_specs=[pl.BlockSpec((tm, k), lambda i, j: (i, 0)),
                  pl.BlockSpec((k, tn), lambda i, j: (0, j)),
                  *extra_specs],
        out_specs=pl.BlockSpec((tm, tn), lambda i, j: (i, j)),
        out_shape=jax.ShapeDtypeStruct((m, n), out_dtype),
        compiler_params=_params("arbitrary", "arbitrary"),
        name=name,
    )(a, w, *extra)


def _gate_epilogue(acc, b_ref):
    return jax.nn.sigmoid(acc + b_ref[...])


def _resid_epilogue(acc, x_ref, g_ref):
    return x_ref[...] + g_ref[0] * acc


def _rms(x, g):
    return x * lax.rsqrt(jnp.mean(x * x, axis=-1, keepdims=True) + EPS) * g


def _dsa_prep_kernel(p_ref, gcq_ref, gckv_ref, gki_ref, bki_ref, cq_o, ckv_o, ki_o, wi_o):
    cq_o[...] = _rms(p_ref[:, OFF_CQ:OFF_CQ + A_Q_RANK], gcq_ref[...]).astype(bf16)
    ckv_o[...] = _rms(p_ref[:, OFF_CKV:OFF_CKV + A_KV_RANK], gckv_ref[...]).astype(bf16)
    ki = p_ref[:, OFF_KI:OFF_KI + IDX_DIM]
    mu = jnp.mean(ki, axis=-1, keepdims=True)
    var = jnp.mean(jnp.square(ki - mu), axis=-1, keepdims=True)
    ki_o[...] = ((ki - mu) * lax.rsqrt(var + EPS) * gki_ref[...] + bki_ref[...]).astype(bf16)
    wi_o[...] = p_ref[:, OFF_WI:OFF_WI + 128] * (IDX_HEADS ** -0.5 * IDX_DIM ** -0.5)


def _dsa_prep(proj, g_cq, g_ckv, g_ki, b_ki):
    ts = 512
    row = lambda i: (i, 0)
    fixed = lambda i: (0, 0)
    return pl.pallas_call(
        _dsa_prep_kernel,
        grid=(TOKENS // ts,),
        in_specs=[pl.BlockSpec((ts, 2048), row),
                  pl.BlockSpec((1, A_Q_RANK), fixed),
                  pl.BlockSpec((1, A_KV_RANK), fixed),
                  pl.BlockSpec((1, IDX_DIM), fixed),
                  pl.BlockSpec((1, IDX_DIM), fixed)],
        out_specs=[pl.BlockSpec((ts, A_Q_RANK), row),
                   pl.BlockSpec((ts, A_KV_RANK), row),
                   pl.BlockSpec((ts, IDX_DIM), row),
                   pl.BlockSpec((ts, 128), row)],
        out_shape=[jax.ShapeDtypeStruct((TOKENS, A_Q_RANK), bf16),
                   jax.ShapeDtypeStruct((TOKENS, A_KV_RANK), bf16),
                   jax.ShapeDtypeStruct((TOKENS, IDX_DIM), bf16),
                   jax.ShapeDtypeStruct((TOKENS, 128), f32)],
        compiler_params=_params("arbitrary"),
        name="dsa_prep",
    )(proj, g_cq, g_ckv, g_ki, b_ki)


def _qlat_kernel(q_ref, w_ref, o_ref):
    o_ref[...] = jnp.dot(q_ref[...], w_ref[0], preferred_element_type=f32).astype(o_ref.dtype)


def _qlat(qq, wuk_t):
    tm = 1024
    q_col0 = IDX_HEADS
    return pl.pallas_call(
        _qlat_kernel,
        grid=(TOKENS // tm, A_HEADS),
        in_specs=[pl.BlockSpec((tm, A_HEAD_DIM), lambda i, h: (i, q_col0 + h)),
                  pl.BlockSpec((1, A_HEAD_DIM, A_KV_RANK), lambda i, h: (h, 0, 0))],
        out_specs=pl.BlockSpec((tm, A_KV_RANK), lambda i, h: (i, h)),
        out_shape=jax.ShapeDtypeStruct((TOKENS, A_HEADS * A_KV_RANK), bf16),
        compiler_params=_params("arbitrary", "arbitrary"),
        name="q_lat",
    )(qq, wuk_t)


def _t5_kernel(t5_ref, o_ref):
    h = pl.program_id(0)
    half = T5_BUCKETS // 2
    exact = half // 2
    qi = lax.broadcasted_iota(i32, (Q_BLOCK, NEAR), 0)
    kj = lax.broadcasted_iota(i32, (Q_BLOCK, NEAR), 1)
    rel = kj - Q_BLOCK - qi
    n = jnp.abs(rel)
    log_ratio = jnp.log(jnp.maximum(n, 1).astype(f32) / exact) / math.log(T5_MAX_DIST / exact)
    large = jnp.minimum(exact + (log_ratio * (half - exact)).astype(i32), half - 1)
    bucket = jnp.where(rel > 0, half, 0) + jnp.where(n < exact, n, large)
    acc = jnp.zeros((Q_BLOCK, NEAR), f32)
    for k in range(T5_BUCKETS):
        acc = jnp.where(bucket == k, t5_ref[k, h], acc)
    o_ref[0] = acc - t5_ref[half - 1, h]


def _t5_table(t5_bias):
    return pl.pallas_call(
        _t5_kernel,
        grid=(A_HEADS,),
        in_specs=[pl.BlockSpec(memory_space=pltpu.SMEM)],
        out_specs=pl.BlockSpec((1, Q_BLOCK, NEAR), lambda h: (h, 0, 0)),
        out_shape=jax.ShapeDtypeStruct((A_HEADS, Q_BLOCK, NEAR), f32),
        compiler_params=_params("arbitrary"),
        name="t5_table",
    )(t5_bias)


def _dsa_kernel(t5_ref, qi_ref, kit_ref, wi_ref, ql_ref, kv_ref, wuv_ref, pt_ref, o_ref,
                key_ref, mask_ref, lg_ref):
    qb = pl.program_id(1)
    q0 = qb * Q_BLOCK
    ct_w = 256
    w = wi_ref[...]

    def col_tile(ct, carry):
        c0 = pl.multiple_of(ct * ct_w, ct_w)
        kt = kit_ref[:, pl.ds(c0, ct_w)]
        acc = jnp.zeros((Q_BLOCK, ct_w), f32)
        for h in range(IDX_HEADS):
            x = jnp.dot(qi_ref[:, h * IDX_DIM:(h + 1) * IDX_DIM], kt, preferred_element_type=f32)
            acc = acc + jnp.maximum(x, 0.0) * w[:, h:h + 1]
        lg_ref[:, pl.ds(c0, ct_w)] = acc
        return carry

    lax.fori_loop(0, SEQ // ct_w, col_tile, 0)

    row = lax.broadcasted_iota(i32, (Q_BLOCK, SEQ), 0)
    col = lax.broadcasted_iota(i32, (Q_BLOCK, SEQ), 1)
    adm = (col // CHUNK) <= ((q0 + row) // CHUNK)
    score = jnp.where(adm, lg_ref[...], -jnp.inf)
    bits = pltpu.bitcast(score, i32)
    key_ref[...] = jnp.where(bits < 0, bits ^ 0x7FFFFFFF, bits)

    def bisect(b, thr_u):
        cand_u = thr_u | jnp.left_shift(jnp.int32(1), 31 - b)
        cnt = jnp.sum(jnp.where(key_ref[...] >= (cand_u ^ INT_MIN), 1.0, 0.0), axis=1, keepdims=True)
        return jnp.where(cnt >= TOPK, cand_u, thr_u)

    thr_u = lax.fori_loop(0, 32, bisect, jnp.zeros((Q_BLOCK, 1), i32))
    sel = (key_ref[...] >= (thr_u ^ INT_MIN)) & adm
    mask_ref[...] = jnp.where(sel, 0.0, -jnp.inf)

    scale = A_HEAD_DIM ** -0.5

    def head(h, carry):
        qh = ql_ref[:, pl.ds(pl.multiple_of(h * A_KV_RANK, A_KV_RANK), A_KV_RANK)]
        lg = lax.dot_general(qh, kv_ref[...], (((1,), (1,)), ((), ())), preferred_element_type=f32)
        lg_ref[...] = lg * scale + (mask_ref[...] + t5_ref[T5_BUCKETS // 2 - 1, h])

        @pl.when(qb == 0)
        def _():
            lg_ref[:, 0:Q_BLOCK] += pt_ref[h, :, Q_BLOCK:NEAR]

        @pl.when(qb > 0)
        def _():
            w0 = pl.multiple_of(q0 - Q_BLOCK, Q_BLOCK)
            lg_ref[:, pl.ds(w0, NEAR)] += pt_ref[h]

        lg = lg_ref[...]
        m = jnp.max(lg, axis=1, keepdims=True)
        e = jnp.exp(lg - m)
        l = jnp.sum(e, axis=1, keepdims=True)
        ol = jnp.dot(e.astype(bf16), kv_ref[...], preferred_element_type=f32) / l
        o = jnp.dot(ol.astype(bf16), wuv_ref[h], preferred_element_type=f32)
        o_ref[:, pl.ds(pl.multiple_of(h * A_HEAD_DIM, A_HEAD_DIM), A_HEAD_DIM)] = o.astype(o_ref.dtype)
        return carry

    lax.fori_loop(0, A_HEADS, head, 0)


def _dsa(t5_bias, qq, kidx_t, widx, qlat, ckv, wuv, ptab):
    nqb = SEQ // Q_BLOCK
    return pl.pallas_call(
        _dsa_kernel,
        grid=(BATCH, nqb),
        in_specs=[pl.BlockSpec(memory_space=pltpu.SMEM),
                  pl.BlockSpec((Q_BLOCK, IDX_HEADS * IDX_DIM), lambda b, i: (b * nqb + i, 0)),
                  pl.BlockSpec((IDX_DIM, SEQ), lambda b, i: (0, b)),
                  pl.BlockSpec((Q_BLOCK, 128), lambda b, i: (b * nqb + i, 0)),
                  pl.BlockSpec((Q_BLOCK, A_HEADS * A_KV_RANK), lambda b, i: (b * nqb + i, 0)),
                  pl.BlockSpec((SEQ, A_KV_RANK), lambda b, i: (b, 0)),
                  pl.BlockSpec((A_HEADS, A_KV_RANK, A_HEAD_DIM), lambda b, i: (0, 0, 0)),
                  pl.BlockSpec((A_HEADS, Q_BLOCK, NEAR), lambda b, i: (0, 0, 0))],
        out_specs=pl.BlockSpec((Q_BLOCK, A_WIDTH), lambda b, i: (b * nqb + i, 0)),
        out_shape=jax.ShapeDtypeStruct((TOKENS, A_WIDTH), bf16),
        scratch_shapes=[pltpu.VMEM((Q_BLOCK, SEQ), i32),
                        pltpu.VMEM((Q_BLOCK, SEQ), f32),
                        pltpu.VMEM((Q_BLOCK, SEQ), f32)],
        compiler_params=_params("arbitrary", "arbitrary"),
        name="dsa_main",
    )(t5_bias, qq, kidx_t, widx, qlat, ckv, wuv, ptab)


def _ret_kernel(q_ref, k_ref, v_ref, g_ref, cc_ref, ss_ref, dm_ref, qd_ref, kd_ref, cd_ref, gr_ref,
                o_ref, qs_ref, ks_ref):
    cc = cc_ref[...]
    ss = ss_ref[...]
    q = q_ref[...]
    k = k_ref[...] * (R_QK_DIM ** -0.5)
    half = R_QK_DIM // 2
    qs_ref[...] = q * cc + pltpu.roll(q, half, 1) * ss
    ks_ref[...] = k * cc + pltpu.roll(k, half, 1) * ss
    dm = dm_ref[0]
    qd = qd_ref[0]
    kd = kd_ref[0]
    cd = cd_ref[0]
    gr = gr_ref[0]

    def chunk(n, state):
        r0 = pl.multiple_of(n * CHUNK, CHUNK)
        qn = qs_ref[pl.ds(r0, CHUNK), :]
        kn = ks_ref[pl.ds(r0, CHUNK), :]
        vn = v_ref[pl.ds(r0, CHUNK), :].astype(bf16)
        att = lax.dot_general(qn.astype(bf16), kn.astype(bf16), (((1,), (1,)), ((), ())),
                              preferred_element_type=f32) * dm
        y = jnp.dot(att.astype(bf16), vn, preferred_element_type=f32)
        y = y + jnp.dot((qn * qd).astype(bf16), state.astype(bf16), preferred_element_type=f32)
        kv = lax.dot_general((kn * kd).astype(bf16), vn, (((0,), (0,)), ((), ())),
                             preferred_element_type=f32)
        y = y * lax.rsqrt(jnp.mean(y * y, axis=-1, keepdims=True) + EPS) * gr
        gate = g_ref[pl.ds(r0, CHUNK), :]
        o_ref[pl.ds(r0, CHUNK), :] = (gate * jax.nn.sigmoid(gate) * y).astype(o_ref.dtype)
        return state * cd + kv

    lax.fori_loop(0, SEQ // CHUNK, chunk, jnp.zeros((R_QK_DIM, R_V_DIM), f32))


def _retention(proj, cc, ss, dmask, qdec, kdec, cdec, g_ret):
    qk_blk = OFF_RQ // R_QK_DIM
    k_blk = OFF_RK // R_QK_DIM
    v_blk = OFF_RV // R_V_DIM
    g_blk = OFF_RG // R_V_DIM
    per_head = lambda b, h: (h, 0, 0)
    return pl.pallas_call(
        _ret_kernel,
        grid=(BATCH, R_HEADS),
        in_specs=[pl.BlockSpec((SEQ, R_QK_DIM), lambda b, h: (b, qk_blk + h)),
                  pl.BlockSpec((SEQ, R_QK_DIM), lambda b, h: (b, k_blk + h)),
                  pl.BlockSpec((SEQ, R_V_DIM), lambda b, h: (b, v_blk + h)),
                  pl.BlockSpec((SEQ, R_V_DIM), lambda b, h: (b, g_blk + h)),
                  pl.BlockSpec((SEQ, R_QK_DIM), lambda b, h: (0, 0)),
                  pl.BlockSpec((SEQ, R_QK_DIM), lambda b, h: (0, 0)),
                  pl.BlockSpec((1, CHUNK, CHUNK), per_head),
                  pl.BlockSpec((1, CHUNK, R_QK_DIM), per_head),
                  pl.BlockSpec((1, CHUNK, R_QK_DIM), per_head),
                  pl.BlockSpec((1, 1, R_V_DIM), per_head),
                  pl.BlockSpec((1, 1, R_V_DIM), per_head)],
        out_specs=pl.BlockSpec((SEQ, R_V_DIM), lambda b, h: (b, h)),
        out_shape=jax.ShapeDtypeStruct((TOKENS, R_WIDTH), bf16),
        scratch_shapes=[pltpu.VMEM((SEQ, R_QK_DIM), f32), pltpu.VMEM((SEQ, R_QK_DIM), f32)],
        compiler_params=_params("arbitrary", "arbitrary"),
        name="retention",
    )(proj, proj, proj, proj, cc, ss, dmask, qdec, kdec, cdec, g_ret)


def _merge_kernel(ya_ref, yr_ref, wa_ref, wr_ref, ga_ref, gr_ref, o_ref):
    pa = jnp.dot(ya_ref[...], wa_ref[...], preferred_element_type=f32)
    pr = jnp.dot(yr_ref[...], wr_ref[...], preferred_element_type=f32)
    o_ref[...] = (ga_ref[...].astype(f32) * pa + gr_ref[...].astype(f32) * pr).astype(o_ref.dtype)


def _merge(ya, yr, wa, wr, gates):
    tm, tn = 1024, 512
    nj = D_MODEL // tn
    return pl.pallas_call(
        _merge_kernel,
        grid=(TOKENS // tm, nj),
        in_specs=[pl.BlockSpec((tm, A_WIDTH), lambda i, j: (i, 0)),
                  pl.BlockSpec((tm, R_WIDTH), lambda i, j: (i, 0)),
                  pl.BlockSpec((A_WIDTH, tn), lambda i, j: (0, j)),
                  pl.BlockSpec((R_WIDTH, tn), lambda i, j: (0, j)),
                  pl.BlockSpec((tm, tn), lambda i, j: (i, j)),
                  pl.BlockSpec((tm, tn), lambda i, j: (i, nj + j))],
        out_specs=pl.BlockSpec((tm, tn), lambda i, j: (i, j)),
        out_shape=jax.ShapeDtypeStruct((TOKENS, D_MODEL), bf16),
        compiler_params=_params("arbitrary", "arbitrary"),
        name="merge_up",
    )(ya, yr, wa, wr, gates, gates)


def _peer_route_kernel(q_ref, k_ref, s1_o, s2_o, a1_o, m2_o, thr_o, v1_ref, v2_ref, cand_ref, ec_ref):
    hq = P_QUERY_DIM // 2
    s1 = jnp.dot(k_ref[0], q_ref[0:hq, :].astype(bf16), preferred_element_type=f32)
    s2 = jnp.dot(k_ref[1], q_ref[hq:P_QUERY_DIM, :].astype(bf16), preferred_element_type=f32)

    def top_values(s, v_ref):
        cur = s
        for r in range(P_TOPK):
            m = jnp.max(cur, axis=0, keepdims=True)
            v_ref[r:r + 1, :] = m
            cur = jnp.where(cur == m, -jnp.inf, cur)

    top_values(s1, v1_ref)
    top_values(s2, v2_ref)
    v1 = v1_ref[...]
    v2 = v2_ref[...]
    m1 = v1[0:1]
    m2 = v2[0:1]
    e1 = jnp.exp(v1 - m1)
    e2 = jnp.exp(v2 - m2)
    for r in range(P_TOPK):
        cand_ref[r * P_TOPK:(r + 1) * P_TOPK, :] = v1[r:r + 1] + v2
        ec_ref[r * P_TOPK:(r + 1) * P_TOPK, :] = e1[r:r + 1] * e2
    cand = cand_ref[...]
    cur = cand
    thr = None
    for r in range(P_TOPK):
        thr = jnp.max(cur, axis=0, keepdims=True)
        cur = jnp.where(cur == thr, -jnp.inf, cur)
    z = jnp.sum(jnp.where(cand >= thr, ec_ref[...], 0.0), axis=0, keepdims=True)
    s1_o[0] = s1
    s2_o[0] = s2
    a1_o[0] = jnp.exp(s1 - m1) / z
    m2_o[0] = m2
    thr_o[0] = thr


def _peer_route(q_t, keys):
    tl = 512
    big = lambda h, j: (h, 0, j)
    big_shape = jax.ShapeDtypeStruct((P_HEADS, P_NKEYS, TOKENS), f32)
    row_shape = jax.ShapeDtypeStruct((P_HEADS, 1, TOKENS), f32)
    return pl.pallas_call(
        _peer_route_kernel,
        grid=(P_HEADS, TOKENS // tl),
        in_specs=[pl.BlockSpec((P_QUERY_DIM, tl), lambda h, j: (h, j)),
                  pl.BlockSpec((2, P_NKEYS, P_QUERY_DIM // 2), lambda h, j: (0, 0, 0))],
        out_specs=[pl.BlockSpec((1, P_NKEYS, tl), big),
                   pl.BlockSpec((1, P_NKEYS, tl), big),
                   pl.BlockSpec((1, P_NKEYS, tl), big),
                   pl.BlockSpec((1, 1, tl), big),
                   pl.BlockSpec((1, 1, tl), big)],
        out_shape=[big_shape, big_shape, big_shape, row_shape, row_shape],
        scratch_shapes=[pltpu.VMEM((P_TOPK, tl), f32), pltpu.VMEM((P_TOPK, tl), f32),
                        pltpu.VMEM((P_TOPK * P_TOPK, tl), f32), pltpu.VMEM((P_TOPK * P_TOPK, tl), f32)],
        compiler_params=_params("arbitrary", "arbitrary"),
        name="peer_route",
    )(q_t, keys)


PEER_TM = 512
PEER_TE = 256


def _peer_kernel(ht_ref, u_ref, vt_ref, s1_ref, a1_ref, s2_ref, m2_ref, thr_ref, o_ref, b2_ref, coef_ref):
    c = pl.program_id(1)
    rows = PEER_TE // P_NKEYS

    @pl.when(c == 0)
    def _():
        o_ref[...] = jnp.zeros_like(o_ref)
        b2_ref[...] = jnp.exp(s2_ref[...] - m2_ref[...])

    pre = jnp.dot(u_ref[...], ht_ref[...], preferred_element_type=f32)
    act = 0.5 * pre * (1.0 + lax.erf(pre * np.float32(np.sqrt(0.5))))
    for j in range(rows):
        i1 = c * rows + j
        wj = jnp.zeros((P_NKEYS, PEER_TM), f32)
        for h in range(P_HEADS):
            s1row = s1_ref[h, pl.ds(i1, 1), :]
            a1row = a1_ref[h, pl.ds(i1, 1), :]
            sel = (s1row + s2_ref[h]) >= thr_ref[h]
            wj = wj + jnp.where(sel, b2_ref[h], 0.0) * a1row
        coef_ref[j * P_NKEYS:(j + 1) * P_NKEYS, :] = (wj * act[j * P_NKEYS:(j + 1) * P_NKEYS]).astype(bf16)
    o_ref[...] += jnp.dot(vt_ref[...], coef_ref[...], preferred_element_type=f32)


def _peer(h_t, u, v_t, s1, a1, s2, m2, thr):
    tm, te = PEER_TM, PEER_TE
    tok3 = lambda i, c: (0, 0, i)
    return pl.pallas_call(
        _peer_kernel,
        grid=(TOKENS // tm, P_EXPERTS // te),
        in_specs=[pl.BlockSpec((D_MODEL, tm), lambda i, c: (0, i)),
                  pl.BlockSpec((te, D_MODEL), lambda i, c: (c, 0)),
                  pl.BlockSpec((D_MODEL, te), lambda i, c: (0, c)),
                  pl.BlockSpec((P_HEADS, P_NKEYS, tm), tok3),
                  pl.BlockSpec((P_HEADS, P_NKEYS, tm), tok3),
                  pl.BlockSpec((P_HEADS, P_NKEYS, tm), tok3),
                  pl.BlockSpec((P_HEADS, 1, tm), tok3),
                  pl.BlockSpec((P_HEADS, 1, tm), tok3)],
        out_specs=pl.BlockSpec((D_MODEL, tm), lambda i, c: (0, i)),
        out_shape=jax.ShapeDtypeStruct((D_MODEL, TOKENS), f32),
        scratch_shapes=[pltpu.VMEM((P_HEADS, P_NKEYS, tm), f32), pltpu.VMEM((te, tm), bf16)],
        compiler_params=_params("arbitrary", "arbitrary"),
        name="peer_experts",
    )(h_t, u, v_t, s1, a1, s2, m2, thr)


def _final_kernel(x_ref, y_ref, gt_ref, g_ref, o_ref):
    x = x_ref[...] + gt_ref[0] * y_ref[...]
    o_ref[...] = x * lax.rsqrt(jnp.mean(x * x, axis=-1, keepdims=True) + EPS) * g_ref[...]


def _final(x1, y, modr, g_final):
    ts = 256
    per_b = SEQ // ts
    return pl.pallas_call(
        _final_kernel,
        grid=(TOKENS // ts,),
        in_specs=[pl.BlockSpec((ts, D_MODEL), lambda i: (i, 0)),
                  pl.BlockSpec((ts, D_MODEL), lambda i: (i, 0)),
                  pl.BlockSpec((1, 1, D_MODEL), lambda i: ((i // per_b) * N_MOD + 5, 0, 0)),
                  pl.BlockSpec((1, D_MODEL), lambda i: (0, 0))],
        out_specs=pl.BlockSpec((ts, D_MODEL), lambda i: (i, 0)),
        out_shape=jax.ShapeDtypeStruct((TOKENS, D_MODEL), f32),
        compiler_params=_params("arbitrary"),
        name="final_norm",
    )(x1, y, modr, g_final)


def _retention_tables():
    half = R_QK_DIM // 2
    inv = 1.0 / (ROT_BASE ** jnp.linspace(0.0, 1.0, half, dtype=f32))
    ang = jnp.arange(SEQ, dtype=f32)[:, None] * inv[None, :]
    cos, sin = jnp.cos(ang), jnp.sin(ang)
    cc = jnp.concatenate([cos, cos], axis=-1)
    ss = jnp.concatenate([-sin, sin], axis=-1)
    log_g = jnp.log(1.0 - jnp.power(2.0, -5.0 - jnp.arange(R_HEADS, dtype=f32)))
    j = jnp.arange(CHUNK, dtype=f32)
    diff = j[:, None] - j[None, :]
    dmask = jnp.where(diff[None] >= 0, jnp.exp(jnp.maximum(diff, 0.0)[None] * log_g[:, None, None]), 0.0)
    kdec = jnp.exp((CHUNK - 1.0 - j)[None, :] * log_g[:, None])
    qdec = jnp.exp((j + 1.0)[None, :] * log_g[:, None])
    cdec = jnp.exp(CHUNK * log_g)
    kdec = jnp.broadcast_to(kdec[:, :, None], (R_HEADS, CHUNK, R_QK_DIM))
    qdec = jnp.broadcast_to(qdec[:, :, None], (R_HEADS, CHUNK, R_QK_DIM))
    cdec = jnp.broadcast_to(cdec[:, None, None], (R_HEADS, 1, R_V_DIM))
    return cc, ss, dmask, qdec, kdec, cdec


def kernel(x, c, w_ada, b_ada, g_mix, w_in, g_cq, g_ckv, w_uq, w_uk, w_uv, w_qi, g_ki, b_ki, t5_bias, g_ret,
           w_up, w_gate, b_gate, w_out, g_ffn, w_pq, sub_keys, u_exp, v_exp, g_final):
    x2d = x.reshape(TOKENS, D_MODEL)

    w_in_l = w_in[0]
    zeros = lambda n: jnp.zeros((D_MODEL, n), w_in_l.dtype)
    w_in_p = jnp.concatenate(
        [w_in_l[:, :1728], zeros(OFF_RQ - 1728), w_in_l[:, 1728:]], axis=1).astype(bf16)
    w_gate_b = w_gate[0].astype(bf16)
    w_q_all = jnp.concatenate(
        [w_qi[0].reshape(A_Q_RANK, IDX_HEADS * IDX_DIM), w_uq[0].reshape(A_Q_RANK, A_WIDTH)], axis=1).astype(bf16)
    wuk_t = jnp.transpose(w_uk[0], (1, 2, 0)).astype(bf16)
    wuv_h = jnp.transpose(w_uv[0], (1, 0, 2)).astype(bf16)
    w_up_a = w_up[0, :A_WIDTH].astype(bf16)
    w_up_r = w_up[0, A_WIDTH:].astype(bf16)
    w_out_b = w_out[0].astype(bf16)
    w_pq_t = w_pq[0].T.astype(bf16)
    keys_b = sub_keys[0].astype(bf16)
    u_b = u_exp[0].astype(bf16)
    v_t = v_exp[0].T.astype(bf16)

    c8 = jnp.pad(c, ((0, 8 - BATCH), (0, 0)))
    mod = _ada(c8, w_ada[0], b_ada[0].reshape(1, N_MOD * D_MODEL))[:BATCH]
    modr = mod.reshape(BATCH * N_MOD, 1, D_MODEL)

    h = _normmod(x2d, g_mix[0].reshape(1, D_MODEL), modr, 0, 1)
    proj = _matmul(h, w_in_p, f32, tm=1024, tn=512, name="in_proj")
    gates = _matmul(h, w_gate_b, bf16, tm=1024, tn=512, epilogue=_gate_epilogue,
                    extra=(b_gate[0].reshape(1, 2 * D_MODEL),),
                    extra_specs=(pl.BlockSpec((1, 512), lambda i, j: (0, j)),), name="gates")

    cqn, ckvn, kidx, widx = _dsa_prep(proj, g_cq[0].reshape(1, -1), g_ckv[0].reshape(1, -1),
                                      g_ki[0].reshape(1, -1), b_ki[0].reshape(1, -1))
    qq = _matmul(cqn, w_q_all, bf16, tm=1024, tn=512, name="q_up")
    qlat = _qlat(qq, wuk_t)
    ptab = _t5_table(t5_bias)
    y_a = _dsa(t5_bias, qq, kidx.T, widx, qlat, ckvn, wuv_h, ptab)

    cc, ss, dmask, qdec, kdec, cdec = _retention_tables()
    y_r = _retention(proj, cc, ss, dmask, qdec, kdec, cdec, g_ret[0].reshape(R_HEADS, 1, R_V_DIM))

    merged = _merge(y_a, y_r, w_up_a, w_up_r, gates)
    per_b = SEQ // 1024
    x1 = _matmul(merged, w_out_b, f32, tm=1024, tn=512, epilogue=_resid_epilogue,
                 extra=(x2d, modr),
                 extra_specs=(pl.BlockSpec((1024, 512), lambda i, j: (i, j)),
                              pl.BlockSpec((1, 1, 512), lambda i, j: ((i // per_b) * N_MOD + 2, 0, j))),
                 name="out_proj")

    h2 = _normmod(x1, g_ffn[0].reshape(1, D_MODEL), modr, 3, 4)
    h2_t = h2.T
    q_t = _matmul(w_pq_t, h2_t, f32, tm=1024, tn=512, name="peer_q")
    s1, s2, a1, m2, thr = _peer_route(q_t, keys_b)
    y_t = _peer(h2_t, u_b, v_t, s1, a1, s2, m2, thr)
    out = _final(x1, y_t.T, modr, g_final.reshape(1, D_MODEL))
    return out.reshape(BATCH, SEQ, D_MODEL)
```

```python
import functools
import math

import numpy as np
import jax
import jax.numpy as jnp
from jax import lax
from jax.experimental import pallas as pl
from jax.experimental.pallas import tpu as pltpu

f32 = jnp.float32
bf16 = jnp.bfloat16
i32 = jnp.int32

D_MODEL = 4096
BATCH = 4
SEQ = 2048
TOKENS = BATCH * SEQ
CHUNK = 64
EPS = 1e-6
N_MOD = 6
A_HEADS = 16
A_HEAD_DIM = 128
A_Q_RANK = 1024
A_KV_RANK = 512
A_WIDTH = A_HEADS * A_HEAD_DIM
IDX_HEADS = 64
IDX_DIM = 128
TOPK = 256
T5_BUCKETS = 32
T5_MAX_DIST = 128
R_HEADS = 8
R_QK_DIM = 128
R_V_DIM = 256
R_QK_WIDTH = R_HEADS * R_QK_DIM
R_WIDTH = R_HEADS * R_V_DIM
ROT_BASE = 10000.0
P_HEADS = 8
P_QUERY_DIM = 256
P_NKEYS = 128
P_TOPK = 16
P_EXPERTS = P_NKEYS * P_NKEYS

PROJ_WIDTH = 8192
OFF_CQ, OFF_CKV, OFF_KI, OFF_WI = 0, 1024, 1536, 1664
OFF_RQ, OFF_RK, OFF_RV, OFF_RG = 2048, 3072, 4096, 6144

Q_BLOCK = 128
NEAR = 2 * Q_BLOCK
VMEM_LIMIT = 56 * 1024 * 1024
INT_MIN = -2147483648


def _params(*sem, flags=None):
    return pltpu.CompilerParams(dimension_semantics=sem, vmem_limit_bytes=VMEM_LIMIT, flags=flags)


def _ada_kernel(c_ref, w_ref, b_ref, o_ref):
    c = c_ref[...]
    ca = (c * jax.nn.sigmoid(c)).astype(bf16)
    o_ref[...] = jnp.dot(ca, w_ref[...].astype(bf16), preferred_element_type=f32) + b_ref[...]


def _ada(c8, w, b):
    n = w.shape[1]
    tn = 512
    return pl.pallas_call(
        _ada_kernel,
        grid=(n // tn,),
        in_specs=[pl.BlockSpec((8, D_MODEL), lambda j: (0, 0)),
                  pl.BlockSpec((D_MODEL, tn), lambda j: (0, j)),
                  pl.BlockSpec((1, tn), lambda j: (0, j))],
        out_specs=pl.BlockSpec((8, tn), lambda j: (0, j)),
        out_shape=jax.ShapeDtypeStruct((8, n), f32),
        compiler_params=_params("arbitrary"),
        name="ada_mod",
    )(c8, w, b)


def _normmod_kernel(x_ref, g_ref, sh_ref, sc_ref, o_ref):
    x = x_ref[...]
    y = x * lax.rsqrt(jnp.mean(x * x, axis=-1, keepdims=True) + EPS) * g_ref[...]
    o_ref[...] = (y * (1.0 + sc_ref[0]) + sh_ref[0]).astype(o_ref.dtype)


def _normmod(x2d, g, modr, shift_slot, scale_slot):
    ts = 256
    per_b = SEQ // ts
    return pl.pallas_call(
        _normmod_kernel,
        grid=(TOKENS // ts,),
        in_specs=[pl.BlockSpec((ts, D_MODEL), lambda i: (i, 0)),
                  pl.BlockSpec((1, D_MODEL), lambda i: (0, 0)),
                  pl.BlockSpec((1, 1, D_MODEL), lambda i: ((i // per_b) * N_MOD + shift_slot, 0, 0)),
                  pl.BlockSpec((1, 1, D_MODEL), lambda i: ((i // per_b) * N_MOD + scale_slot, 0, 0))],
        out_specs=pl.BlockSpec((ts, D_MODEL), lambda i: (i, 0)),
        out_shape=jax.ShapeDtypeStruct((TOKENS, D_MODEL), bf16),
        compiler_params=_params("arbitrary"),
        name="norm_modulate",
    )(x2d, g, modr, modr)


def _mm_kernel(a_ref, w_ref, *rest, epilogue):
    o_ref = rest[-1]
    acc = jnp.dot(a_ref[...], w_ref[...], preferred_element_type=f32)
    o_ref[...] = epilogue(acc, *rest[:-1]).astype(o_ref.dtype)


def _matmul(a, w, out_dtype, *, tm, tn, epilogue=None, extra=(), extra_specs=(), name):
    m, k = a.shape
    n = w.shape[1]
    if epilogue is None:
        epilogue = lambda acc: acc
    return pl.pallas_call(
        functools.partial(_mm_kernel, epilogue=epilogue),
        grid=(m // tm, n // tn),
        in_specs=[pl.BlockSpec((tm, k), lambda i, j: (i, 0)),
                  pl.BlockSpec((k, tn), lambda i, j: (0, j)),
                  *extra_specs],
        out_specs=pl.BlockSpec((tm, tn), lambda i, j: (i, j)),
        out_shape=jax.ShapeDtypeStruct((m, n), out_dtype),
        compiler_params=_params("arbitrary", "arbitrary"),
        name=name,
    )(a, w, *extra)


def _gate_epilogue(acc, b_ref):
    return jax.nn.sigmoid(acc + b_ref[...])


def _resid_epilogue(acc, x_ref, g_ref):
    return x_ref[...] + g_ref[0] * acc


def _rms(x, g):
    return x * lax.rsqrt(jnp.mean(x * x, axis=-1, keepdims=True) + EPS) * g


def _dsa_prep_kernel(p_ref, gcq_ref, gckv_ref, gki_ref, bki_ref, cq_o, ckv_o, ki_o, wi_o):
    cq_o[...] = _rms(p_ref[:, OFF_CQ:OFF_CQ + A_Q_RANK], gcq_ref[...]).astype(bf16)
    ckv_o[...] = _rms(p_ref[:, OFF_CKV:OFF_CKV + A_KV_RANK], gckv_ref[...]).astype(bf16)
    ki = p_ref[:, OFF_KI:OFF_KI + IDX_DIM]
    mu = jnp.mean(ki, axis=-1, keepdims=True)
    var = jnp.mean(jnp.square(ki - mu), axis=-1, keepdims=True)
    ki_o[...] = ((ki - mu) * lax.rsqrt(var + EPS) * gki_ref[...] + bki_ref[...]).astype(bf16)
    wi_o[...] = p_ref[:, OFF_WI:OFF_WI + 128] * (IDX_HEADS ** -0.5 * IDX_DIM ** -0.5)


def _dsa_prep(proj, g_cq, g_ckv, g_ki, b_ki):
    ts = 512
    row = lambda i: (i, 0)
    fixed = lambda i: (0, 0)
    return pl.pallas_call(
        _dsa_prep_kernel,
        grid=(TOKENS // ts,),
        in_specs=[pl.BlockSpec((ts, 2048), row),
                  pl.BlockSpec((1, A_Q_RANK), fixed),
                  pl.BlockSpec((1, A_KV_RANK), fixed),
                  pl.BlockSpec((1, IDX_DIM), fixed),
                  pl.BlockSpec((1, IDX_DIM), fixed)],
        out_specs=[pl.BlockSpec((ts, A_Q_RANK), row),
                   pl.BlockSpec((ts, A_KV_RANK), row),
                   pl.BlockSpec((ts, IDX_DIM), row),
                   pl.BlockSpec((ts, 128), row)],
        out_shape=[jax.ShapeDtypeStruct((TOKENS, A_Q_RANK), bf16),
                   jax.ShapeDtypeStruct((TOKENS, A_KV_RANK), bf16),
                   jax.ShapeDtypeStruct((TOKENS, IDX_DIM), bf16),
                   jax.ShapeDtypeStruct((TOKENS, 128), f32)],
        compiler_params=_params("arbitrary"),
        name="dsa_prep",
    )(proj, g_cq, g_ckv, g_ki, b_ki)


def _qlat_kernel(q_ref, w_ref, o_ref):
    o_ref[...] = jnp.dot(q_ref[...], w_ref[0], preferred_element_type=f32).astype(o_ref.dtype)


def _qlat(qq, wuk_t):
    tm = 1024
    q_col0 = IDX_HEADS
    return pl.pallas_call(
        _qlat_kernel,
        grid=(TOKENS // tm, A_HEADS),
        in_specs=[pl.BlockSpec((tm, A_HEAD_DIM), lambda i, h: (i, q_col0 + h)),
                  pl.BlockSpec((1, A_HEAD_DIM, A_KV_RANK), lambda i, h: (h, 0, 0))],
        out_specs=pl.BlockSpec((tm, A_KV_RANK), lambda i, h: (i, h)),
        out_shape=jax.ShapeDtypeStruct((TOKENS, A_HEADS * A_KV_RANK), bf16),
        compiler_params=_params("arbitrary", "arbitrary"),
        name="q_lat",
    )(qq, wuk_t)


def _t5_kernel(t5_ref, o_ref):
    h = pl.program_id(0)
    half = T5_BUCKETS // 2
    exact = half // 2
    qi = lax.broadcasted_iota(i32, (Q_BLOCK, NEAR), 0)
    kj = lax.broadcasted_iota(i32, (Q_BLOCK, NEAR), 1)
    rel = kj - Q_BLOCK - qi
    n = jnp.abs(rel)
    log_ratio = jnp.log(jnp.maximum(n, 1).astype(f32) / exact) / math.log(T5_MAX_DIST / exact)
    large = jnp.minimum(exact + (log_ratio * (half - exact)).astype(i32), half - 1)
    bucket = jnp.where(rel > 0, half, 0) + jnp.where(n < exact, n, large)
    acc = jnp.zeros((Q_BLOCK, NEAR), f32)
    for k in range(T5_BUCKETS):
        acc = jnp.where(bucket == k, t5_ref[k, h], acc)
    o_ref[0] = acc - t5_ref[half - 1, h]


def _t5_table(t5_bias):
    return pl.pallas_call(
        _t5_kernel,
        grid=(A_HEADS,),
        in_specs=[pl.BlockSpec(memory_space=pltpu.SMEM)],
        out_specs=pl.BlockSpec((1, Q_BLOCK, NEAR), lambda h: (h, 0, 0)),
        out_shape=jax.ShapeDtypeStruct((A_HEADS, Q_BLOCK, NEAR), f32),
        compiler_params=_params("arbitrary"),
        name="t5_table",
    )(t5_bias)


def _dsa_kernel(t5_ref, qi_ref, kit_ref, wi_ref, ql_ref, kv_ref, wuv_ref, pt_ref, o_ref,
                key_ref, mask_ref, lg_ref):
    qb = pl.program_id(1)
    q0 = qb * Q_BLOCK
    ct_w = 256
    w = wi_ref[...]

    def col_tile(ct, carry):
        c0 = pl.multiple_of(ct * ct_w, ct_w)
        kt = kit_ref[:, pl.ds(c0, ct_w)]
        acc = jnp.zeros((Q_BLOCK, ct_w), f32)
        for h in range(IDX_HEADS):
            x = jnp.dot(qi_ref[:, h * IDX_DIM:(h + 1) * IDX_DIM], kt, preferred_element_type=f32)
            acc = acc + jnp.maximum(x, 0.0) * w[:, h:h + 1]
        lg_ref[:, pl.ds(c0, ct_w)] = acc
        return carry

    lax.fori_loop(0, SEQ // ct_w, col_tile, 0)

    row = lax.broadcasted_iota(i32, (Q_BLOCK, SEQ), 0)
    col = lax.broadcasted_iota(i32, (Q_BLOCK, SEQ), 1)
    adm = (col // CHUNK) <= ((q0 + row) // CHUNK)
    score = jnp.where(adm, lg_ref[...], -jnp.inf)
    bits = pltpu.bitcast(score, i32)
    key_ref[...] = jnp.where(bits < 0, bits ^ 0x7FFFFFFF, bits)

    def bisect(b, thr_u):
        cand_u = thr_u | jnp.left_shift(jnp.int32(1), 31 - b)
        cnt = jnp.sum(jnp.where(key_ref[...] >= (cand_u ^ INT_MIN), 1.0, 0.0), axis=1, keepdims=True)
        return jnp.where(cnt >= TOPK, cand_u, thr_u)

    thr_u = lax.fori_loop(0, 32, bisect, jnp.zeros((Q_BLOCK, 1), i32))
    sel = (key_ref[...] >= (thr_u ^ INT_MIN)) & adm
    mask_ref[...] = jnp.where(sel, 0.0, -jnp.inf)

    scale = A_HEAD_DIM ** -0.5

    def head(h, carry):
        qh = ql_ref[:, pl.ds(pl.multiple_of(h * A_KV_RANK, A_KV_RANK), A_KV_RANK)]
        lg = lax.dot_general(qh, kv_ref[...], (((1,), (1,)), ((), ())), preferred_element_type=f32)
        lg_ref[...] = lg * scale + (mask_ref[...] + t5_ref[T5_BUCKETS // 2 - 1, h])

        @pl.when(qb == 0)
        def _():
            lg_ref[:, 0:Q_BLOCK] += pt_ref[h, :, Q_BLOCK:NEAR]

        @pl.when(qb > 0)
        def _():
            w0 = pl.multiple_of(q0 - Q_BLOCK, Q_BLOCK)
            lg_ref[:, pl.ds(w0, NEAR)] += pt_ref[h]

        lg = lg_ref[...]
        m = jnp.max(lg, axis=1, keepdims=True)
        e = jnp.exp(lg - m)
        l = jnp.sum(e, axis=1, keepdims=True)
        ol = jnp.dot(e.astype(bf16), kv_ref[...], preferred_element_type=f32) / l
        o = jnp.dot(ol.astype(bf16), wuv_ref[h], preferred_element_type=f32)
        o_ref[:, pl.ds(pl.multiple_of(h * A_HEAD_DIM, A_HEAD_DIM), A_HEAD_DIM)] = o.astype(o_ref.dtype)
        return carry

    lax.fori_loop(0, A_HEADS, head, 0)


def _dsa(t5_bias, qq, kidx_t, widx, qlat, ckv, wuv, ptab):
    nqb = SEQ // Q_BLOCK
    return pl.pallas_call(
        _dsa_kernel,
        grid=(BATCH, nqb),
        in_specs=[pl.BlockSpec(memory_space=pltpu.SMEM),
                  pl.BlockSpec((Q_BLOCK, IDX_HEADS * IDX_DIM), lambda b, i: (b * nqb + i, 0)),
                  pl.BlockSpec((IDX_DIM, SEQ), lambda b, i: (0, b)),
                  pl.BlockSpec((Q_BLOCK, 128), lambda b, i: (b * nqb + i, 0)),
                  pl.BlockSpec((Q_BLOCK, A_HEADS * A_KV_RANK), lambda b, i: (b * nqb + i, 0)),
                  pl.BlockSpec((SEQ, A_KV_RANK), lambda b, i: (b, 0)),
                  pl.BlockSpec((A_HEADS, A_KV_RANK, A_HEAD_DIM), lambda b, i: (0, 0, 0)),
                  pl.BlockSpec((A_HEADS, Q_BLOCK, NEAR), lambda b, i: (0, 0, 0))],
        out_specs=pl.BlockSpec((Q_BLOCK, A_WIDTH), lambda b, i: (b * nqb + i, 0)),
        out_shape=jax.ShapeDtypeStruct((TOKENS, A_WIDTH), bf16),
        scratch_shapes=[pltpu.VMEM((Q_BLOCK, SEQ), i32),
                        pltpu.VMEM((Q_BLOCK, SEQ), f32),
                        pltpu.VMEM((Q_BLOCK, SEQ), f32)],
        compiler_params=_params("arbitrary", "arbitrary"),
        name="dsa_main",
    )(t5_bias, qq, kidx_t, widx, qlat, ckv, wuv, ptab)


def _ret_kernel(q_ref, k_ref, v_ref, g_ref, cc_ref, ss_ref, dm_ref, qd_ref, kd_ref, cd_ref, gr_ref,
                o_ref, qs_ref, ks_ref):
    cc = cc_ref[...]
    ss = ss_ref[...]
    q = q_ref[...]
    k = k_ref[...] * (R_QK_DIM ** -0.5)
    half = R_QK_DIM // 2
    qs_ref[...] = q * cc + pltpu.roll(q, half, 1) * ss
    ks_ref[...] = k * cc + pltpu.roll(k, half, 1) * ss
    dm = dm_ref[0]
    qd = qd_ref[0]
    kd = kd_ref[0]
    cd = cd_ref[0]
    gr = gr_ref[0]

    def chunk(n, state):
        r0 = pl.multiple_of(n * CHUNK, CHUNK)
        qn = qs_ref[pl.ds(r0, CHUNK), :]
        kn = ks_ref[pl.ds(r0, CHUNK), :]
        vn = v_ref[pl.ds(r0, CHUNK), :].astype(bf16)
        att = lax.dot_general(qn.astype(bf16), kn.astype(bf16), (((1,), (1,)), ((), ())),
                              preferred_element_type=f32) * dm
        y = jnp.dot(att.astype(bf16), vn, preferred_element_type=f32)
        y = y + jnp.dot((qn * qd).astype(bf16), state.astype(bf16), preferred_element_type=f32)
        kv = lax.dot_general((kn * kd).astype(bf16), vn, (((0,), (0,)), ((), ())),
                             preferred_element_type=f32)
        y = y * lax.rsqrt(jnp.mean(y * y, axis=-1, keepdims=True) + EPS) * gr
        gate = g_ref[pl.ds(r0, CHUNK), :]
        o_ref[pl.ds(r0, CHUNK), :] = (gate * jax.nn.sigmoid(gate) * y).astype(o_ref.dtype)
        return state * cd + kv

    lax.fori_loop(0, SEQ // CHUNK, chunk, jnp.zeros((R_QK_DIM, R_V_DIM), f32))


def _retention(proj, cc, ss, dmask, qdec, kdec, cdec, g_ret):
    qk_blk = OFF_RQ // R_QK_DIM
    k_blk = OFF_RK // R_QK_DIM
    v_blk = OFF_RV // R_V_DIM
    g_blk = OFF_RG // R_V_DIM
    per_head = lambda b, h: (h, 0, 0)
    return pl.pallas_call(
        _ret_kernel,
        grid=(BATCH, R_HEADS),
        in_specs=[pl.BlockSpec((SEQ, R_QK_DIM), lambda b, h: (b, qk_blk + h)),
                  pl.BlockSpec((SEQ, R_QK_DIM), lambda b, h: (b, k_blk + h)),
                  pl.BlockSpec((SEQ, R_V_DIM), lambda b, h: (b, v_blk + h)),
                  pl.BlockSpec((SEQ, R_V_DIM), lambda b, h: (b, g_blk + h)),
                  pl.BlockSpec((SEQ, R_QK_DIM), lambda b, h: (0, 0)),
                  pl.BlockSpec((SEQ, R_QK_DIM), lambda b, h: (0, 0)),
                  pl.BlockSpec((1, CHUNK, CHUNK), per_head),
                  pl.BlockSpec((1, CHUNK, R_QK_DIM), per_head),
                  pl.BlockSpec((1, CHUNK, R_QK_DIM), per_head),
                  pl.BlockSpec((1, 1, R_V_DIM), per_head),
                  pl.BlockSpec((1, 1, R_V_DIM), per_head)],
        out_specs=pl.BlockSpec((SEQ, R_V_DIM), lambda b, h: (b, h)),
        out_shape=jax.ShapeDtypeStruct((TOKENS, R_WIDTH), bf16),
        scratch_shapes=[pltpu.VMEM((SEQ, R_QK_DIM), f32), pltpu.VMEM((SEQ, R_QK_DIM), f32)],
        compiler_params=_params("arbitrary", "arbitrary"),
        name="retention",
    )(proj, proj, proj, proj, cc, ss, dmask, qdec, kdec, cdec, g_ret)


def _merge_kernel(ya_ref, yr_ref, wa_ref, wr_ref, ga_ref, gr_ref, o_ref):
    pa = jnp.dot(ya_ref[...], wa_ref[...], preferred_element_type=f32)
    pr = jnp.dot(yr_ref[...], wr_ref[...], preferred_element_type=f32)
    o_ref[...] = (ga_ref[...].astype(f32) * pa + gr_ref[...].astype(f32) * pr).astype(o_ref.dtype)


def _merge(ya, yr, wa, wr, gates):
    tm, tn = 1024, 512
    nj = D_MODEL // tn
    return pl.pallas_call(
        _merge_kernel,
        grid=(TOKENS // tm, nj),
        in_specs=[pl.BlockSpec((tm, A_WIDTH), lambda i, j: (i, 0)),
                  pl.BlockSpec((tm, R_WIDTH), lambda i, j: (i, 0)),
                  pl.BlockSpec((A_WIDTH, tn), lambda i, j: (0, j)),
                  pl.BlockSpec((R_WIDTH, tn), lambda i, j: (0, j)),
                  pl.BlockSpec((tm, tn), lambda i, j: (i, j)),
                  pl.BlockSpec((tm, tn), lambda i, j: (i, nj + j))],
        out_specs=pl.BlockSpec((tm, tn), lambda i, j: (i, j)),
        out_shape=jax.ShapeDtypeStruct((TOKENS, D_MODEL), bf16),
        compiler_params=_params("arbitrary", "arbitrary"),
        name="merge_up",
    )(ya, yr, wa, wr, gates, gates)


def _peer_route_kernel(q_ref, k_ref, s1_o, s2_o, a1_o, m2_o, thr_o, v1_ref, v2_ref, cand_ref, ec_ref):
    hq = P_QUERY_DIM // 2
    s1 = jnp.dot(k_ref[0], q_ref[0:hq, :].astype(bf16), preferred_element_type=f32)
    s2 = jnp.dot(k_ref[1], q_ref[hq:P_QUERY_DIM, :].astype(bf16), preferred_element_type=f32)

    def top_values(s, v_ref):
        cur = s
        for r in range(P_TOPK):
            m = jnp.max(cur, axis=0, keepdims=True)
            v_ref[r:r + 1, :] = m
            cur = jnp.where(cur == m, -jnp.inf, cur)

    top_values(s1, v1_ref)
    top_values(s2, v2_ref)
    v1 = v1_ref[...]
    v2 = v2_ref[...]
    m1 = v1[0:1]
    m2 = v2[0:1]
    e1 = jnp.exp(v1 - m1)
    e2 = jnp.exp(v2 - m2)
    for r in range(P_TOPK):
        cand_ref[r * P_TOPK:(r + 1) * P_TOPK, :] = v1[r:r + 1] + v2
        ec_ref[r * P_TOPK:(r + 1) * P_TOPK, :] = e1[r:r + 1] * e2
    cand = cand_ref[...]
    cur = cand
    thr = None
    for r in range(P_TOPK):
        thr = jnp.max(cur, axis=0, keepdims=True)
        cur = jnp.where(cur == thr, -jnp.inf, cur)
    z = jnp.sum(jnp.where(cand >= thr, ec_ref[...], 0.0), axis=0, keepdims=True)
    s1_o[0] = s1
    for lt in range(s2.shape[1] // 128):
        s2_o[0, lt] = s2[:, lt * 128:(lt + 1) * 128]
    a1_o[0] = jnp.exp(s1 - m1) / z
    m2_o[0] = m2
    thr_o[0] = thr


def _peer_route(q_t, keys):
    tl = 512
    big = lambda h, j: (h, 0, j)
    big_shape = jax.ShapeDtypeStruct((P_HEADS, P_NKEYS, TOKENS), f32)
    row_shape = jax.ShapeDtypeStruct((P_HEADS, 1, TOKENS), f32)
    return pl.pallas_call(
        _peer_route_kernel,
        grid=(P_HEADS, TOKENS // tl),
        in_specs=[pl.BlockSpec((P_QUERY_DIM, tl), lambda h, j: (h, j)),
                  pl.BlockSpec((2, P_NKEYS, P_QUERY_DIM // 2), lambda h, j: (0, 0, 0))],
        out_specs=[pl.BlockSpec((1, P_NKEYS, tl), big),
                   pl.BlockSpec((1, tl // 128, P_NKEYS, 128), lambda h, j: (h, j, 0, 0)),
                   pl.BlockSpec((1, P_NKEYS, tl), big),
                   pl.BlockSpec((1, 1, tl), big),
                   pl.BlockSpec((1, 1, tl), big)],
        out_shape=[big_shape, jax.ShapeDtypeStruct((P_HEADS, TOKENS // 128, P_NKEYS, 128), f32),
                   big_shape, row_shape, row_shape],
        scratch_shapes=[pltpu.VMEM((P_TOPK, tl), f32), pltpu.VMEM((P_TOPK, tl), f32),
                        pltpu.VMEM((P_TOPK * P_TOPK, tl), f32), pltpu.VMEM((P_TOPK * P_TOPK, tl), f32)],
        compiler_params=_params("arbitrary", "arbitrary"),
        name="peer_route",
    )(q_t, keys)


PEER_TM = 512
PEER_TE = 512
PEER_NC = P_EXPERTS // PEER_TE
PEER_ROWS = PEER_TE // P_NKEYS
PEER_KT = 64
PEER_KPIECES = 8


def _peer_kernel(ht_ref, u_ref, vt_ref, s1_ref, a1_ref, s2_ref, m2_ref, thr_ref, o_ref,
                 b2_ref, pre_ref, w_ref):
    c = pl.program_id(1)

    @pl.when(c == 0)
    def _():
        o_ref[...] = jnp.zeros_like(o_ref)
        for lt in range(PEER_TM // 128):
            b2_ref[:, lt] = jnp.exp(s2_ref[:, lt] - m2_ref[:, :, lt * 128:(lt + 1) * 128])

    @pl.when(c > 0)
    def _():
        pre = pre_ref[...]
        act = 0.5 * pre * (1.0 + lax.erf(pre * np.float32(np.sqrt(0.5))))
        coef = (w_ref[0:PEER_TE, :] * act).astype(bf16)
        o_ref[...] += jnp.dot(vt_ref[...], coef, preferred_element_type=f32)

    @pl.when(c < PEER_NC)
    def _():
        z = pl.multiple_of(jnp.minimum(c, 0), PEER_TE)
        units = [(j, lt, kt) for j in range(PEER_ROWS) for lt in range(PEER_TM // 128)
                 for kt in range(P_NKEYS // PEER_KT)]
        per_piece = len(units) // PEER_KPIECES
        kw = D_MODEL // PEER_KPIECES
        for kq in range(PEER_KPIECES):
            part = jnp.dot(u_ref[:, kq * kw:(kq + 1) * kw], ht_ref[kq * kw:(kq + 1) * kw, :],
                           preferred_element_type=f32)
            if kq == 0:
                pre_ref[...] = part
            else:
                pre_ref[...] += part
            w_ref[pl.ds(z + PEER_TE, 8), 0:128] = part[PEER_TE - 8:PEER_TE, PEER_TM - 128:PEER_TM]
            for j, lt, kt in units[kq * per_piece:(kq + 1) * per_piece]:
                ls = slice(lt * 128, (lt + 1) * 128)
                ks = slice(kt * PEER_KT, (kt + 1) * PEER_KT)
                wj = jnp.zeros((PEER_KT, 128), f32)
                for h in range(P_HEADS):
                    sel = (s1_ref[h, 0, j:j + 1, ls] + s2_ref[h, lt, ks, :]) >= thr_ref[h, :, ls]
                    wj = wj + jnp.where(sel, b2_ref[h, lt, ks, :], 0.0) * a1_ref[h, 0, j:j + 1, ls]
                w0 = pl.multiple_of(z + (j * P_NKEYS + kt * PEER_KT), PEER_KT)
                w_ref[pl.ds(w0, PEER_KT), ls] = wj


def _peer(h_t, u, v_t, s1, a1, s2, m2, thr):
    tm, te, nc = PEER_TM, PEER_TE, PEER_NC
    s1r = s1.reshape(P_HEADS, nc, PEER_ROWS, TOKENS)
    a1r = a1.reshape(P_HEADS, nc, PEER_ROWS, TOKENS)
    tok3 = lambda i, c: (0, 0, i)
    cur = lambda i, c: (0, jnp.minimum(c, nc - 1), 0, i)
    return pl.pallas_call(
        _peer_kernel,
        grid=(TOKENS // tm, nc + 1),
        in_specs=[pl.BlockSpec((D_MODEL, tm), lambda i, c: (0, i)),
                  pl.BlockSpec((te, D_MODEL), lambda i, c: (jnp.minimum(c, nc - 1), 0)),
                  pl.BlockSpec((D_MODEL, te), lambda i, c: (0, jnp.maximum(c - 1, 0))),
                  pl.BlockSpec((P_HEADS, 1, PEER_ROWS, tm), cur),
                  pl.BlockSpec((P_HEADS, 1, PEER_ROWS, tm), cur),
                  pl.BlockSpec((P_HEADS, tm // 128, P_NKEYS, 128), lambda i, c: (0, i, 0, 0)),
                  pl.BlockSpec((P_HEADS, 1, tm), tok3),
                  pl.BlockSpec((P_HEADS, 1, tm), tok3)],
        out_specs=pl.BlockSpec((D_MODEL, tm), lambda i, c: (0, i)),
        out_shape=jax.ShapeDtypeStruct((D_MODEL, TOKENS), f32),
        scratch_shapes=[pltpu.VMEM((P_HEADS, tm // 128, P_NKEYS, 128), f32),
                        pltpu.VMEM((te, tm), f32), pltpu.VMEM((te + 8, tm), f32)],
        compiler_params=_params("arbitrary", "arbitrary"),
        name="peer_experts",
    )(h_t, u, v_t, s1r, a1r, s2, m2, thr)


def _final_kernel(x_ref, y_ref, gt_ref, g_ref, o_ref):
    x = x_ref[...] + gt_ref[0] * y_ref[...]
    o_ref[...] = x * lax.rsqrt(jnp.mean(x * x, axis=-1, keepdims=True) + EPS) * g_ref[...]


def _final(x1, y, modr, g_final):
    ts = 256
    per_b = SEQ // ts
    return pl.pallas_call(
        _final_kernel,
        grid=(TOKENS // ts,),
        in_specs=[pl.BlockSpec((ts, D_MODEL), lambda i: (i, 0)),
                  pl.BlockSpec((ts, D_MODEL), lambda i: (i, 0)),
                  pl.BlockSpec((1, 1, D_MODEL), lambda i: ((i // per_b) * N_MOD + 5, 0, 0)),
                  pl.BlockSpec((1, D_MODEL), lambda i: (0, 0))],
        out_specs=pl.BlockSpec((ts, D_MODEL), lambda i: (i, 0)),
        out_shape=jax.ShapeDtypeStruct((TOKENS, D_MODEL), f32),
        compiler_params=_params("arbitrary"),
        name="final_norm",
    )(x1, y, modr, g_final)


def _retention_tables():
    half = R_QK_DIM // 2
    inv = 1.0 / (ROT_BASE ** jnp.linspace(0.0, 1.0, half, dtype=f32))
    ang = jnp.arange(SEQ, dtype=f32)[:, None] * inv[None, :]
    cos, sin = jnp.cos(ang), jnp.sin(ang)
    cc = jnp.concatenate([cos, cos], axis=-1)
    ss = jnp.concatenate([-sin, sin], axis=-1)
    log_g = jnp.log(1.0 - jnp.power(2.0, -5.0 - jnp.arange(R_HEADS, dtype=f32)))
    j = jnp.arange(CHUNK, dtype=f32)
    diff = j[:, None] - j[None, :]
    dmask = jnp.where(diff[None] >= 0, jnp.exp(jnp.maximum(diff, 0.0)[None] * log_g[:, None, None]), 0.0)
    kdec = jnp.exp((CHUNK - 1.0 - j)[None, :] * log_g[:, None])
    qdec = jnp.exp((j + 1.0)[None, :] * log_g[:, None])
    cdec = jnp.exp(CHUNK * log_g)
    kdec = jnp.broadcast_to(kdec[:, :, None], (R_HEADS, CHUNK, R_QK_DIM))
    qdec = jnp.broadcast_to(qdec[:, :, None], (R_HEADS, CHUNK, R_QK_DIM))
    cdec = jnp.broadcast_to(cdec[:, None, None], (R_HEADS, 1, R_V_DIM))
    return cc, ss, dmask, qdec, kdec, cdec


def kernel(x, c, w_ada, b_ada, g_mix, w_in, g_cq, g_ckv, w_uq, w_uk, w_uv, w_qi, g_ki, b_ki, t5_bias, g_ret,
           w_up, w_gate, b_gate, w_out, g_ffn, w_pq, sub_keys, u_exp, v_exp, g_final):
    x2d = x.reshape(TOKENS, D_MODEL)

    w_in_l = w_in[0]
    zeros = lambda n: jnp.zeros((D_MODEL, n), w_in_l.dtype)
    w_in_p = jnp.concatenate(
        [w_in_l[:, :1728], zeros(OFF_RQ - 1728), w_in_l[:, 1728:]], axis=1).astype(bf16)
    w_gate_b = w_gate[0].astype(bf16)
    w_q_all = jnp.concatenate(
        [w_qi[0].reshape(A_Q_RANK, IDX_HEADS * IDX_DIM), w_uq[0].reshape(A_Q_RANK, A_WIDTH)], axis=1).astype(bf16)
    wuk_t = jnp.transpose(w_uk[0], (1, 2, 0)).astype(bf16)
    wuv_h = jnp.transpose(w_uv[0], (1, 0, 2)).astype(bf16)
    w_up_a = w_up[0, :A_WIDTH].astype(bf16)
    w_up_r = w_up[0, A_WIDTH:].astype(bf16)
    w_out_b = w_out[0].astype(bf16)
    w_pq_t = w_pq[0].T.astype(bf16)
    keys_b = sub_keys[0].astype(bf16)
    u_b = u_exp[0].astype(bf16)
    v_t = v_exp[0].T.astype(bf16)

    c8 = jnp.pad(c, ((0, 8 - BATCH), (0, 0)))
    mod = _ada(c8, w_ada[0], b_ada[0].reshape(1, N_MOD * D_MODEL))[:BATCH]
    modr = mod.reshape(BATCH * N_MOD, 1, D_MODEL)

    h = _normmod(x2d, g_mix[0].reshape(1, D_MODEL), modr, 0, 1)
    proj = _matmul(h, w_in_p, f32, tm=1024, tn=512, name="in_proj")
    gates = _matmul(h, w_gate_b, bf16, tm=1024, tn=512, epilogue=_gate_epilogue,
                    extra=(b_gate[0].reshape(1, 2 * D_MODEL),),
                    extra_specs=(pl.BlockSpec((1, 512), lambda i, j: (0, j)),), name="gates")

    cqn, ckvn, kidx, widx = _dsa_prep(proj, g_cq[0].reshape(1, -1), g_ckv[0].reshape(1, -1),
                                      g_ki[0].reshape(1, -1), b_ki[0].reshape(1, -1))
    qq = _matmul(cqn, w_q_all, bf16, tm=1024, tn=512, name="q_up")
    qlat = _qlat(qq, wuk_t)
    ptab = _t5_table(t5_bias)
    y_a = _dsa(t5_bias, qq, kidx.T, widx, qlat, ckvn, wuv_h, ptab)

    cc, ss, dmask, qdec, kdec, cdec = _retention_tables()
    y_r = _retention(proj, cc, ss, dmask, qdec, kdec, cdec, g_ret[0].reshape(R_HEADS, 1, R_V_DIM))

    merged = _merge(y_a, y_r, w_up_a, w_up_r, gates)
    per_b = SEQ // 1024
    x1 = _matmul(merged, w_out_b, f32, tm=1024, tn=512, epilogue=_resid_epilogue,
                 extra=(x2d, modr),
                 extra_specs=(pl.BlockSpec((1024, 512), lambda i, j: (i, j)),
                              pl.BlockSpec((1, 1, 512), lambda i, j: ((i // per_b) * N_MOD + 2, 0, j))),
                 name="out_proj")

    h2 = _normmod(x1, g_ffn[0].reshape(1, D_MODEL), modr, 3, 4)
    h2_t = h2.T
    q_t = _matmul(w_pq_t, h2_t, f32, tm=1024, tn=512, name="peer_q")
    s1, s2, a1, m2, thr = _peer_route(q_t, keys_b)
    y_t = _peer(h2_t, u_b, v_t, s1, a1, s2, m2, thr)
    out = _final(x1, y_t.T, modr, g_final.reshape(1, D_MODEL))
    return out.reshape(BATCH, SEQ, D_MODEL)
```

```python
import functools
import math

import numpy as np
import jax
import jax.numpy as jnp
from jax import lax
from jax.experimental import pallas as pl
from jax.experimental.pallas import tpu as pltpu

f32 = jnp.float32
bf16 = jnp.bfloat16
i32 = jnp.int32

D_MODEL = 4096
BATCH = 4
SEQ = 2048
TOKENS = BATCH * SEQ
CHUNK = 64
EPS = 1e-6
N_MOD = 6
A_HEADS = 16
A_HEAD_DIM = 128
A_Q_RANK = 1024
A_KV_RANK = 512
A_WIDTH = A_HEADS * A_HEAD_DIM
IDX_HEADS = 64
IDX_DIM = 128
TOPK = 256
T5_BUCKETS = 32
T5_MAX_DIST = 128
R_HEADS = 8
R_QK_DIM = 128
R_V_DIM = 256
R_QK_WIDTH = R_HEADS * R_QK_DIM
R_WIDTH = R_HEADS * R_V_DIM
ROT_BASE = 10000.0
P_HEADS = 8
P_QUERY_DIM = 256
P_NKEYS = 128
P_TOPK = 16
P_EXPERTS = P_NKEYS * P_NKEYS

PROJ_WIDTH = 8192
OFF_CQ, OFF_CKV, OFF_KI, OFF_WI = 0, 1024, 1536, 1664
OFF_RQ, OFF_RK, OFF_RV, OFF_RG = 2048, 3072, 4096, 6144

Q_BLOCK = 128
NEAR_BACK = 128
NEAR = NEAR_BACK + Q_BLOCK
ATTN_SCALE = A_HEAD_DIM ** -0.5
VMEM_LIMIT = 56 * 1024 * 1024
INT_MIN = -2147483648


def _params(*sem, flags=None):
    return pltpu.CompilerParams(dimension_semantics=sem, vmem_limit_bytes=VMEM_LIMIT, flags=flags)


def _ada_kernel(c_ref, w_ref, b_ref, o_ref):
    c = c_ref[...]
    ca = (c * jax.nn.sigmoid(c)).astype(bf16)
    o_ref[...] = jnp.dot(ca, w_ref[...].astype(bf16), preferred_element_type=f32) + b_ref[...]


def _ada(c8, w, b):
    n = w.shape[1]
    tn = 512
    return pl.pallas_call(
        _ada_kernel,
        grid=(n // tn,),
        in_specs=[pl.BlockSpec((8, D_MODEL), lambda j: (0, 0)),
                  pl.BlockSpec((D_MODEL, tn), lambda j: (0, j)),
                  pl.BlockSpec((1, tn), lambda j: (0, j))],
        out_specs=pl.BlockSpec((8, tn), lambda j: (0, j)),
        out_shape=jax.ShapeDtypeStruct((8, n), f32),
        compiler_params=_params("arbitrary"),
        name="ada_mod",
    )(c8, w, b)


def _normmod_kernel(x_ref, g_ref, sh_ref, sc_ref, o_ref):
    x = x_ref[...]
    y = x * lax.rsqrt(jnp.mean(x * x, axis=-1, keepdims=True) + EPS) * g_ref[...]
    o_ref[...] = (y * (1.0 + sc_ref[0]) + sh_ref[0]).astype(o_ref.dtype)


def _normmod(x2d, g, modr, shift_slot, scale_slot):
    ts = 256
    per_b = SEQ // ts
    return pl.pallas_call(
        _normmod_kernel,
        grid=(TOKENS // ts,),
        in_specs=[pl.BlockSpec((ts, D_MODEL), lambda i: (i, 0)),
                  pl.BlockSpec((1, D_MODEL), lambda i: (0, 0)),
                  pl.BlockSpec((1, 1, D_MODEL), lambda i: ((i // per_b) * N_MOD + shift_slot, 0, 0)),
                  pl.BlockSpec((1, 1, D_MODEL), lambda i: ((i // per_b) * N_MOD + scale_slot, 0, 0))],
        out_specs=pl.BlockSpec((ts, D_MODEL), lambda i: (i, 0)),
        out_shape=jax.ShapeDtypeStruct((TOKENS, D_MODEL), bf16),
        compiler_params=_params("arbitrary"),
        name="norm_modulate",
    )(x2d, g, modr, modr)


def _mm_kernel(a_ref, w_ref, *rest, epilogue):
    o_ref = rest[-1]
    acc = jnp.dot(a_ref[...], w_ref[...], preferred_element_type=f32)
    o_ref[...] = epilogue(acc, *rest[:-1]).astype(o_ref.dtype)


def _matmul(a, w, out_dtype, *, tm, tn, epilogue=None, extra=(), extra_specs=(), name):
    m, k = a.shape
    n = w.shape[1]
    if epilogue is None:
        epilogue = lambda acc: acc
    return pl.pallas_call(
        functools.partial(_mm_kernel, epilogue=epilogue),
        grid=(m // tm, n // tn),
        in_specs=[pl.BlockSpec((tm, k), lambda i, j: (i, 0)),
                  pl.BlockSpec((k, tn), lambda i, j: (0, j)),
                  *extra_specs],
        out_specs=pl.BlockSpec((tm, tn), lambda i, j: (i, j)),
        out_shape=jax.ShapeDtypeStruct((m, n), out_dtype),
        compiler_params=_params("arbitrary", "arbitrary"),
        name=name,
    )(a, w, *extra)


def _gate_epilogue(acc, b_ref):
    return jax.nn.sigmoid(acc + b_ref[...])


def _resid_epilogue(acc, x_ref, g_ref):
    return x_ref[...] + g_ref[0] * acc


def _rms(x, g):
    return x * lax.rsqrt(jnp.mean(x * x, axis=-1, keepdims=True) + EPS) * g


def _dsa_prep_kernel(p_ref, gcq_ref, gckv_ref, gki_ref, bki_ref, cq_o, ckv_o, ki_o, wi_o):
    cq_o[...] = _rms(p_ref[:, OFF_CQ:OFF_CQ + A_Q_RANK], gcq_ref[...]).astype(bf16)
    ckv_o[...] = _rms(p_ref[:, OFF_CKV:OFF_CKV + A_KV_RANK], gckv_ref[...]).astype(bf16)
    ki = p_ref[:, OFF_KI:OFF_KI + IDX_DIM]
    mu = jnp.mean(ki, axis=-1, keepdims=True)
    var = jnp.mean(jnp.square(ki - mu), axis=-1, keepdims=True)
    ki_o[...] = ((ki - mu) * lax.rsqrt(var + EPS) * gki_ref[...] + bki_ref[...]).astype(bf16)
    wi_o[...] = p_ref[:, OFF_WI:OFF_WI + 128] * (IDX_HEADS ** -0.5 * IDX_DIM ** -0.5)


def _dsa_prep(proj, g_cq, g_ckv, g_ki, b_ki):
    ts = 512
    row = lambda i: (i, 0)
    fixed = lambda i: (0, 0)
    return pl.pallas_call(
        _dsa_prep_kernel,
        grid=(TOKENS // ts,),
        in_specs=[pl.BlockSpec((ts, 2048), row),
                  pl.BlockSpec((1, A_Q_RANK), fixed),
                  pl.BlockSpec((1, A_KV_RANK), fixed),
                  pl.BlockSpec((1, IDX_DIM), fixed),
                  pl.BlockSpec((1, IDX_DIM), fixed)],
        out_specs=[pl.BlockSpec((ts, A_Q_RANK), row),
                   pl.BlockSpec((ts, A_KV_RANK), row),
                   pl.BlockSpec((ts, IDX_DIM), row),
                   pl.BlockSpec((ts, 128), row)],
        out_shape=[jax.ShapeDtypeStruct((TOKENS, A_Q_RANK), bf16),
                   jax.ShapeDtypeStruct((TOKENS, A_KV_RANK), bf16),
                   jax.ShapeDtypeStruct((TOKENS, IDX_DIM), bf16),
                   jax.ShapeDtypeStruct((TOKENS, 128), f32)],
        compiler_params=_params("arbitrary"),
        name="dsa_prep",
    )(proj, g_cq, g_ckv, g_ki, b_ki)


def _qlat_kernel(q_ref, w_ref, o_ref):
    r = jnp.dot(q_ref[...], w_ref[0], preferred_element_type=f32).astype(o_ref.dtype)
    o_ref[:, 0] = r.reshape(o_ref.shape[0], Q_BLOCK, A_KV_RANK)


def _qlat(qq, wuk_t):
    tm = 1024
    q_col0 = IDX_HEADS
    return pl.pallas_call(
        _qlat_kernel,
        grid=(TOKENS // tm, A_HEADS),
        in_specs=[pl.BlockSpec((tm, A_HEAD_DIM), lambda i, h: (i, q_col0 + h)),
                  pl.BlockSpec((1, A_HEAD_DIM, A_KV_RANK), lambda i, h: (h, 0, 0))],
        out_specs=pl.BlockSpec((tm // Q_BLOCK, 1, Q_BLOCK, A_KV_RANK), lambda i, h: (i, h, 0, 0)),
        out_shape=jax.ShapeDtypeStruct((TOKENS // Q_BLOCK, A_HEADS, Q_BLOCK, A_KV_RANK), bf16),
        compiler_params=_params("arbitrary", "arbitrary"),
        name="q_lat",
    )(qq, wuk_t)


def _t5_kernel(t5_ref, o_ref):
    h = pl.program_id(0)
    half = T5_BUCKETS // 2
    exact = half // 2
    qi = lax.broadcasted_iota(i32, (Q_BLOCK, NEAR), 0)
    kj = lax.broadcasted_iota(i32, (Q_BLOCK, NEAR), 1)
    rel = kj - NEAR_BACK - qi
    n = jnp.abs(rel)
    log_ratio = jnp.log(jnp.maximum(n, 1).astype(f32) / exact) / math.log(T5_MAX_DIST / exact)
    large = jnp.minimum(exact + (log_ratio * (half - exact)).astype(i32), half - 1)
    bucket = jnp.where(rel > 0, half, 0) + jnp.where(n < exact, n, large)
    acc = jnp.zeros((Q_BLOCK, NEAR), f32)
    for k in range(T5_BUCKETS):
        acc = jnp.where(bucket == k, t5_ref[k, h], acc)
    o_ref[0] = (acc - t5_ref[half - 1, h]) * (1.0 / ATTN_SCALE)


def _t5_table(t5_bias):
    return pl.pallas_call(
        _t5_kernel,
        grid=(A_HEADS,),
        in_specs=[pl.BlockSpec(memory_space=pltpu.SMEM)],
        out_specs=pl.BlockSpec((1, Q_BLOCK, NEAR), lambda h: (h, 0, 0)),
        out_shape=jax.ShapeDtypeStruct((A_HEADS, Q_BLOCK, NEAR), f32),
        compiler_params=_params("arbitrary"),
        name="t5_table",
    )(t5_bias)


DSA_COL_STEP = 512
DSA_HEAD_GROUP = 8


def _dsa_block(width, qb, t5_ref, qi_ref, kit_ref, wi_ref, ql_ref, kv_ref, wuv_ref, pt_ref, o_ref,
               key_ref, mask_ref, lg_ref, e_ref, l_ref):
    q0 = qb * Q_BLOCK
    ct_w = 256
    w = wi_ref[...]
    score_ref = lg_ref.at[0:Q_BLOCK]

    def col_tile(ct, carry):
        c0 = pl.multiple_of(ct * ct_w, ct_w)
        kt = kit_ref[:, pl.ds(c0, ct_w)]
        acc = jnp.zeros((Q_BLOCK, ct_w), f32)
        for h in range(IDX_HEADS):
            x = jnp.dot(qi_ref[:, h * IDX_DIM:(h + 1) * IDX_DIM], kt, preferred_element_type=f32)
            acc = acc + jnp.maximum(x, 0.0) * w[:, h:h + 1]
        score_ref[:, pl.ds(c0, ct_w)] = acc
        return carry

    lax.fori_loop(0, width // ct_w, col_tile, 0)

    row = lax.broadcasted_iota(i32, (Q_BLOCK, width), 0)
    col = lax.broadcasted_iota(i32, (Q_BLOCK, width), 1)
    adm = (col // CHUNK) <= ((q0 + row) // CHUNK)
    score = jnp.where(adm, score_ref[:, 0:width], -jnp.inf)
    bits = pltpu.bitcast(score, i32)
    key_ref[:, 0:width] = jnp.where(bits < 0, bits ^ 0x7FFFFFFF, bits)

    def bisect(b, thr_u):
        cand_u = thr_u | jnp.left_shift(jnp.int32(1), 31 - b)
        hit = jnp.where(key_ref[:, 0:width] >= (cand_u ^ INT_MIN), 1.0, 0.0)
        cnt = jnp.sum(hit, axis=1, keepdims=True)
        return jnp.where(cnt >= TOPK, cand_u, thr_u)

    thr_u = lax.fori_loop(0, 32, bisect, jnp.zeros((Q_BLOCK, 1), i32))
    sel = (key_ref[:, 0:width] >= (thr_u ^ INT_MIN)) & adm
    mask_ref[:, 0:width] = jnp.where(sel, 0.0, -jnp.inf)

    kv = kv_ref[0:width, :]
    rows = DSA_HEAD_GROUP * Q_BLOCK
    for g in range(A_HEADS // DSA_HEAD_GROUP):
        heads = range(g * DSA_HEAD_GROUP, (g + 1) * DSA_HEAD_GROUP)
        qg = ql_ref[0, g * DSA_HEAD_GROUP:(g + 1) * DSA_HEAD_GROUP].reshape(rows, A_KV_RANK)
        lg_ref[0:rows, 0:width] = lax.dot_general(qg, kv, (((1,), (1,)), ((), ())),
                                                  preferred_element_type=f32)

        @pl.when(qb == 0)
        def _():
            for hh, h in enumerate(heads):
                lg_ref[hh * Q_BLOCK:(hh + 1) * Q_BLOCK, 0:Q_BLOCK] += pt_ref[h, :, NEAR_BACK:NEAR]

        @pl.when(qb > 0)
        def _():
            w0 = pl.multiple_of(q0 - NEAR_BACK, 128)
            for hh, h in enumerate(heads):
                lg_ref[hh * Q_BLOCK:(hh + 1) * Q_BLOCK, pl.ds(w0, NEAR)] += pt_ref[h]

        for hh, h in enumerate(heads):
            rs = slice(hh * Q_BLOCK, (hh + 1) * Q_BLOCK)
            x = lg_ref[rs, 0:width] * ATTN_SCALE + (mask_ref[:, 0:width] + t5_ref[T5_BUCKETS // 2 - 1, h])
            m = jnp.max(x, axis=1, keepdims=True)
            e = jnp.exp(x - m)
            l_ref[rs, :] = jnp.sum(e, axis=1, keepdims=True)
            e_ref[rs, 0:width] = e.astype(bf16)

        ol = jnp.dot(e_ref[0:rows, 0:width], kv, preferred_element_type=f32) / l_ref[0:rows, :]
        for hh, h in enumerate(heads):
            o = jnp.dot(ol[hh * Q_BLOCK:(hh + 1) * Q_BLOCK].astype(bf16), wuv_ref[h],
                        preferred_element_type=f32)
            o_ref[:, h * A_HEAD_DIM:(h + 1) * A_HEAD_DIM] = o.astype(o_ref.dtype)


def _dsa_kernel(t5_ref, qi_ref, kit_ref, wi_ref, ql_ref, kv_ref, wuv_ref, pt_ref, o_ref,
                key_ref, mask_ref, lg_ref, e_ref, l_ref):
    qb = pl.program_id(1)
    blocks_per_step = DSA_COL_STEP // Q_BLOCK
    for n in range(SEQ // DSA_COL_STEP):
        pl.when(qb // blocks_per_step == n)(functools.partial(
            _dsa_block, (n + 1) * DSA_COL_STEP, qb, t5_ref, qi_ref, kit_ref, wi_ref, ql_ref, kv_ref,
            wuv_ref, pt_ref, o_ref, key_ref, mask_ref, lg_ref, e_ref, l_ref))


def _dsa(t5_bias, qq, kidx_t, widx, qlat, ckv, wuv, ptab):
    nqb = SEQ // Q_BLOCK
    return pl.pallas_call(
        _dsa_kernel,
        grid=(BATCH, nqb),
        in_specs=[pl.BlockSpec(memory_space=pltpu.SMEM),
                  pl.BlockSpec((Q_BLOCK, IDX_HEADS * IDX_DIM), lambda b, i: (b * nqb + i, 0)),
                  pl.BlockSpec((IDX_DIM, SEQ), lambda b, i: (0, b)),
                  pl.BlockSpec((Q_BLOCK, 128), lambda b, i: (b * nqb + i, 0)),
                  pl.BlockSpec((1, A_HEADS, Q_BLOCK, A_KV_RANK), lambda b, i: (b * nqb + i, 0, 0, 0)),
                  pl.BlockSpec((SEQ, A_KV_RANK), lambda b, i: (b, 0)),
                  pl.BlockSpec((A_HEADS, A_KV_RANK, A_HEAD_DIM), lambda b, i: (0, 0, 0)),
                  pl.BlockSpec((A_HEADS, Q_BLOCK, NEAR), lambda b, i: (0, 0, 0))],
        out_specs=pl.BlockSpec((Q_BLOCK, A_WIDTH), lambda b, i: (b * nqb + i, 0)),
        out_shape=jax.ShapeDtypeStruct((TOKENS, A_WIDTH), bf16),
        scratch_shapes=[pltpu.VMEM((Q_BLOCK, SEQ), i32),
                        pltpu.VMEM((Q_BLOCK, SEQ), f32),
                        pltpu.VMEM((DSA_HEAD_GROUP * Q_BLOCK, SEQ), f32),
                        pltpu.VMEM((DSA_HEAD_GROUP * Q_BLOCK, SEQ), bf16),
                        pltpu.VMEM((DSA_HEAD_GROUP * Q_BLOCK, 1), f32)],
        compiler_params=_params("arbitrary", "arbitrary"),
        name="dsa_main",
    )(t5_bias, qq, kidx_t, widx, qlat, ckv, wuv, ptab)


def _ret_kernel(q_ref, k_ref, v_ref, g_ref, cc_ref, ss_ref, dm_ref, qd_ref, kd_ref, cd_ref, gr_ref,
                o_ref, qs_ref, ks_ref):
    cc = cc_ref[...]
    ss = ss_ref[...]
    q = q_ref[...]
    k = k_ref[...] * (R_QK_DIM ** -0.5)
    half = R_QK_DIM // 2
    qs_ref[...] = q * cc + pltpu.roll(q, half, 1) * ss
    ks_ref[...] = k * cc + pltpu.roll(k, half, 1) * ss
    dm = dm_ref[0]
    qd = qd_ref[0]
    kd = kd_ref[0]
    cd = cd_ref[0]
    gr = gr_ref[0]

    def chunk(n, state):
        r0 = pl.multiple_of(n * CHUNK, CHUNK)
        qn = qs_ref[pl.ds(r0, CHUNK), :]
        kn = ks_ref[pl.ds(r0, CHUNK), :]
        vn = v_ref[pl.ds(r0, CHUNK), :].astype(bf16)
        att = lax.dot_general(qn.astype(bf16), kn.astype(bf16), (((1,), (1,)), ((), ())),
                              preferred_element_type=f32) * dm
        y = jnp.dot(att.astype(bf16), vn, preferred_element_type=f32)
        y = y + jnp.dot((qn * qd).astype(bf16), state.astype(bf16), preferred_element_type=f32)
        kv = lax.dot_general((kn * kd).astype(bf16), vn, (((0,), (0,)), ((), ())),
                             preferred_element_type=f32)
        y = y * lax.rsqrt(jnp.mean(y * y, axis=-1, keepdims=True) + EPS) * gr
        gate = g_ref[pl.ds(r0, CHUNK), :]
        o_ref[pl.ds(r0, CHUNK), :] = (gate * jax.nn.sigmoid(gate) * y).astype(o_ref.dtype)
        return state * cd + kv

    lax.fori_loop(0, SEQ // CHUNK, chunk, jnp.zeros((R_QK_DIM, R_V_DIM), f32))


def _retention(proj, cc, ss, dmask, qdec, kdec, cdec, g_ret):
    qk_blk = OFF_RQ // R_QK_DIM
    k_blk = OFF_RK // R_QK_DIM
    v_blk = OFF_RV // R_V_DIM
    g_blk = OFF_RG // R_V_DIM
    per_head = lambda b, h: (h, 0, 0)
    return pl.pallas_call(
        _ret_kernel,
        grid=(BATCH, R_HEADS),
        in_specs=[pl.BlockSpec((SEQ, R_QK_DIM), lambda b, h: (b, qk_blk + h)),
                  pl.BlockSpec((SEQ, R_QK_DIM), lambda b, h: (b, k_blk + h)),
                  pl.BlockSpec((SEQ, R_V_DIM), lambda b, h: (b, v_blk + h)),
                  pl.BlockSpec((SEQ, R_V_DIM), lambda b, h: (b, g_blk + h)),
                  pl.BlockSpec((SEQ, R_QK_DIM), lambda b, h: (0, 0)),
                  pl.BlockSpec((SEQ, R_QK_DIM), lambda b, h: (0, 0)),
                  pl.BlockSpec((1, CHUNK, CHUNK), per_head),
                  pl.BlockSpec((1, CHUNK, R_QK_DIM), per_head),
                  pl.BlockSpec((1, CHUNK, R_QK_DIM), per_head),
                  pl.BlockSpec((1, 1, R_V_DIM), per_head),
                  pl.BlockSpec((1, 1, R_V_DIM), per_head)],
        out_specs=pl.BlockSpec((SEQ, R_V_DIM), lambda b, h: (b, h)),
        out_shape=jax.ShapeDtypeStruct((TOKENS, R_WIDTH), bf16),
        scratch_shapes=[pltpu.VMEM((SEQ, R_QK_DIM), f32), pltpu.VMEM((SEQ, R_QK_DIM), f32)],
        compiler_params=_params("arbitrary", "arbitrary"),
        name="retention",
    )(proj, proj, proj, proj, cc, ss, dmask, qdec, kdec, cdec, g_ret)


def _merge_kernel(ya_ref, yr_ref, wa_ref, wr_ref, ga_ref, gr_ref, o_ref):
    pa = jnp.dot(ya_ref[...], wa_ref[...], preferred_element_type=f32)
    pr = jnp.dot(yr_ref[...], wr_ref[...], preferred_element_type=f32)
    o_ref[...] = (ga_ref[...].astype(f32) * pa + gr_ref[...].astype(f32) * pr).astype(o_ref.dtype)


def _merge(ya, yr, wa, wr, gates):
    tm, tn = 1024, 512
    nj = D_MODEL // tn
    return pl.pallas_call(
        _merge_kernel,
        grid=(TOKENS // tm, nj),
        in_specs=[pl.BlockSpec((tm, A_WIDTH), lambda i, j: (i, 0)),
                  pl.BlockSpec((tm, R_WIDTH), lambda i, j: (i, 0)),
                  pl.BlockSpec((A_WIDTH, tn), lambda i, j: (0, j)),
                  pl.BlockSpec((R_WIDTH, tn), lambda i, j: (0, j)),
                  pl.BlockSpec((tm, tn), lambda i, j: (i, j)),
                  pl.BlockSpec((tm, tn), lambda i, j: (i, nj + j))],
        out_specs=pl.BlockSpec((tm, tn), lambda i, j: (i, j)),
        out_shape=jax.ShapeDtypeStruct((TOKENS, D_MODEL), bf16),
        compiler_params=_params("arbitrary", "arbitrary"),
        name="merge_up",
    )(ya, yr, wa, wr, gates, gates)


def _peer_route_kernel(q_ref, k_ref, s1_o, s2_o, a1_o, m2_o, thr_o, v1_ref, v2_ref, cand_ref, ec_ref):
    hq = P_QUERY_DIM // 2
    s1 = jnp.dot(k_ref[0], q_ref[0:hq, :].astype(bf16), preferred_element_type=f32)
    s2 = jnp.dot(k_ref[1], q_ref[hq:P_QUERY_DIM, :].astype(bf16), preferred_element_type=f32)

    def top_values(s, v_ref):
        cur = s
        for r in range(P_TOPK):
            m = jnp.max(cur, axis=0, keepdims=True)
            v_ref[r:r + 1, :] = m
            cur = jnp.where(cur == m, -jnp.inf, cur)

    top_values(s1, v1_ref)
    top_values(s2, v2_ref)
    v1 = v1_ref[...]
    v2 = v2_ref[...]
    m1 = v1[0:1]
    m2 = v2[0:1]
    e1 = jnp.exp(v1 - m1)
    e2 = jnp.exp(v2 - m2)
    for r in range(P_TOPK):
        cand_ref[r * P_TOPK:(r + 1) * P_TOPK, :] = v1[r:r + 1] + v2
        ec_ref[r * P_TOPK:(r + 1) * P_TOPK, :] = e1[r:r + 1] * e2
    cand = cand_ref[...]
    cur = cand
    thr = None
    for r in range(P_TOPK):
        thr = jnp.max(cur, axis=0, keepdims=True)
        cur = jnp.where(cur == thr, -jnp.inf, cur)
    z = jnp.sum(jnp.where(cand >= thr, ec_ref[...], 0.0), axis=0, keepdims=True)
    s1_o[0] = s1
    for lt in range(s2.shape[1] // 128):
        s2_o[0, lt] = s2[:, lt * 128:(lt + 1) * 128]
    a1_o[0] = jnp.exp(s1 - m1) / z
    m2_o[0] = m2
    thr_o[0] = thr


def _peer_route(q_t, keys):
    tl = 512
    big = lambda h, j: (h, 0, j)
    big_shape = jax.ShapeDtypeStruct((P_HEADS, P_NKEYS, TOKENS), f32)
    row_shape = jax.ShapeDtypeStruct((P_HEADS, 1, TOKENS), f32)
    return pl.pallas_call(
        _peer_route_kernel,
        grid=(P_HEADS, TOKENS // tl),
        in_specs=[pl.BlockSpec((P_QUERY_DIM, tl), lambda h, j: (h, j)),
                  pl.BlockSpec((2, P_NKEYS, P_QUERY_DIM // 2), lambda h, j: (0, 0, 0))],
        out_specs=[pl.BlockSpec((1, P_NKEYS, tl), big),
                   pl.BlockSpec((1, tl // 128, P_NKEYS, 128), lambda h, j: (h, j, 0, 0)),
                   pl.BlockSpec((1, P_NKEYS, tl), big),
                   pl.BlockSpec((1, 1, tl), big),
                   pl.BlockSpec((1, 1, tl), big)],
        out_shape=[big_shape, jax.ShapeDtypeStruct((P_HEADS, TOKENS // 128, P_NKEYS, 128), f32),
                   big_shape, row_shape, row_shape],
        scratch_shapes=[pltpu.VMEM((P_TOPK, tl), f32), pltpu.VMEM((P_TOPK, tl), f32),
                        pltpu.VMEM((P_TOPK * P_TOPK, tl), f32), pltpu.VMEM((P_TOPK * P_TOPK, tl), f32)],
        compiler_params=_params("arbitrary", "arbitrary"),
        name="peer_route",
    )(q_t, keys)


PEER_TM = 512
PEER_TE = 512
PEER_NC = P_EXPERTS // PEER_TE
PEER_ROWS = PEER_TE // P_NKEYS
PEER_KT = 64
PEER_KPIECES = 8


def _peer_kernel(ht_ref, u_ref, vt_ref, s1_ref, a1_ref, s2_ref, m2_ref, thr_ref, o_ref,
                 b2_ref, pre_ref, w_ref):
    c = pl.program_id(1)

    @pl.when(c == 0)
    def _():
        o_ref[...] = jnp.zeros_like(o_ref)
        for lt in range(PEER_TM // 128):
            b2_ref[:, lt] = jnp.exp(s2_ref[:, lt] - m2_ref[:, :, lt * 128:(lt + 1) * 128])

    @pl.when(c > 0)
    def _():
        pre = pre_ref[...]
        act = 0.5 * pre * (1.0 + lax.erf(pre * np.float32(np.sqrt(0.5))))
        coef = (w_ref[0:PEER_TE, :] * act).astype(bf16)
        o_ref[...] += jnp.dot(vt_ref[...], coef, preferred_element_type=f32)

    @pl.when(c < PEER_NC)
    def _():
        z = pl.multiple_of(jnp.minimum(c, 0), PEER_TE)
        units = [(j, lt, kt) for j in range(PEER_ROWS) for lt in range(PEER_TM // 128)
                 for kt in range(P_NKEYS // PEER_KT)]
        per_piece = len(units) // PEER_KPIECES
        kw = D_MODEL // PEER_KPIECES
        for kq in range(PEER_KPIECES):
            part = jnp.dot(u_ref[:, kq * kw:(kq + 1) * kw], ht_ref[kq * kw:(kq + 1) * kw, :],
                           preferred_element_type=f32)
            if kq == 0:
                pre_ref[...] = part
            else:
                pre_ref[...] += part
            w_ref[pl.ds(z + PEER_TE, 8), 0:128] = part[PEER_TE - 8:PEER_TE, PEER_TM - 128:PEER_TM]
            for j, lt, kt in units[kq * per_piece:(kq + 1) * per_piece]:
                ls = slice(lt * 128, (lt + 1) * 128)
                ks = slice(kt * PEER_KT, (kt + 1) * PEER_KT)
                wj = jnp.zeros((PEER_KT, 128), f32)
                for h in range(P_HEADS):
                    sel = (s1_ref[h, 0, j:j + 1, ls] + s2_ref[h, lt, ks, :]) >= thr_ref[h, :, ls]
                    wj = wj + jnp.where(sel, b2_ref[h, lt, ks, :], 0.0) * a1_ref[h, 0, j:j + 1, ls]
                w0 = pl.multiple_of(z + (j * P_NKEYS + kt * PEER_KT), PEER_KT)
                w_ref[pl.ds(w0, PEER_KT), ls] = wj


def _peer(h_t, u, v_t, s1, a1, s2, m2, thr):
    tm, te, nc = PEER_TM, PEER_TE, PEER_NC
    s1r = s1.reshape(P_HEADS, nc, PEER_ROWS, TOKENS)
    a1r = a1.reshape(P_HEADS, nc, PEER_ROWS, TOKENS)
    tok3 = lambda i, c: (0, 0, i)
    cur = lambda i, c: (0, jnp.minimum(c, nc - 1), 0, i)
    return pl.pallas_call(
        _peer_kernel,
        grid=(TOKENS // tm, nc + 1),
        in_specs=[pl.BlockSpec((D_MODEL, tm), lambda i, c: (0, i)),
                  pl.BlockSpec((te, D_MODEL), lambda i, c: (jnp.minimum(c, nc - 1), 0)),
                  pl.BlockSpec((D_MODEL, te), lambda i, c: (0, jnp.maximum(c - 1, 0))),
                  pl.BlockSpec((P_HEADS, 1, PEER_ROWS, tm), cur),
                  pl.BlockSpec((P_HEADS, 1, PEER_ROWS, tm), cur),
                  pl.BlockSpec((P_HEADS, tm // 128, P_NKEYS, 128), lambda i, c: (0, i, 0, 0)),
                  pl.BlockSpec((P_HEADS, 1, tm), tok3),
                  pl.BlockSpec((P_HEADS, 1, tm), tok3)],
        out_specs=pl.BlockSpec((D_MODEL, tm), lambda i, c: (0, i)),
        out_shape=jax.ShapeDtypeStruct((D_MODEL, TOKENS), f32),
        scratch_shapes=[pltpu.VMEM((P_HEADS, tm // 128, P_NKEYS, 128), f32),
                        pltpu.VMEM((te, tm), f32), pltpu.VMEM((te + 8, tm), f32)],
        compiler_params=_params("arbitrary", "arbitrary"),
        name="peer_experts",
    )(h_t, u, v_t, s1r, a1r, s2, m2, thr)


def _final_kernel(x_ref, y_ref, gt_ref, g_ref, o_ref):
    x = x_ref[...] + gt_ref[0] * y_ref[...]
    o_ref[...] = x * lax.rsqrt(jnp.mean(x * x, axis=-1, keepdims=True) + EPS) * g_ref[...]


def _final(x1, y, modr, g_final):
    ts = 256
    per_b = SEQ // ts
    return pl.pallas_call(
        _final_kernel,
        grid=(TOKENS // ts,),
        in_specs=[pl.BlockSpec((ts, D_MODEL), lambda i: (i, 0)),
                  pl.BlockSpec((ts, D_MODEL), lambda i: (i, 0)),
                  pl.BlockSpec((1, 1, D_MODEL), lambda i: ((i // per_b) * N_MOD + 5, 0, 0)),
                  pl.BlockSpec((1, D_MODEL), lambda i: (0, 0))],
        out_specs=pl.BlockSpec((ts, D_MODEL), lambda i: (i, 0)),
        out_shape=jax.ShapeDtypeStruct((TOKENS, D_MODEL), f32),
        compiler_params=_params("arbitrary"),
        name="final_norm",
    )(x1, y, modr, g_final)


def _retention_tables():
    half = R_QK_DIM // 2
    inv = 1.0 / (ROT_BASE ** jnp.linspace(0.0, 1.0, half, dtype=f32))
    ang = jnp.arange(SEQ, dtype=f32)[:, None] * inv[None, :]
    cos, sin = jnp.cos(ang), jnp.sin(ang)
    cc = jnp.concatenate([cos, cos], axis=-1)
    ss = jnp.concatenate([-sin, sin], axis=-1)
    log_g = jnp.log(1.0 - jnp.power(2.0, -5.0 - jnp.arange(R_HEADS, dtype=f32)))
    j = jnp.arange(CHUNK, dtype=f32)
    diff = j[:, None] - j[None, :]
    dmask = jnp.where(diff[None] >= 0, jnp.exp(jnp.maximum(diff, 0.0)[None] * log_g[:, None, None]), 0.0)
    kdec = jnp.exp((CHUNK - 1.0 - j)[None, :] * log_g[:, None])
    qdec = jnp.exp((j + 1.0)[None, :] * log_g[:, None])
    cdec = jnp.exp(CHUNK * log_g)
    kdec = jnp.broadcast_to(kdec[:, :, None], (R_HEADS, CHUNK, R_QK_DIM))
    qdec = jnp.broadcast_to(qdec[:, :, None], (R_HEADS, CHUNK, R_QK_DIM))
    cdec = jnp.broadcast_to(cdec[:, None, None], (R_HEADS, 1, R_V_DIM))
    return cc, ss, dmask, qdec, kdec, cdec


def kernel(x, c, w_ada, b_ada, g_mix, w_in, g_cq, g_ckv, w_uq, w_uk, w_uv, w_qi, g_ki, b_ki, t5_bias, g_ret,
           w_up, w_gate, b_gate, w_out, g_ffn, w_pq, sub_keys, u_exp, v_exp, g_final):
    x2d = x.reshape(TOKENS, D_MODEL)

    w_in_l = w_in[0]
    zeros = lambda n: jnp.zeros((D_MODEL, n), w_in_l.dtype)
    w_in_p = jnp.concatenate(
        [w_in_l[:, :1728], zeros(OFF_RQ - 1728), w_in_l[:, 1728:]], axis=1).astype(bf16)
    w_gate_b = w_gate[0].astype(bf16)
    w_q_all = jnp.concatenate(
        [w_qi[0].reshape(A_Q_RANK, IDX_HEADS * IDX_DIM), w_uq[0].reshape(A_Q_RANK, A_WIDTH)], axis=1).astype(bf16)
    wuk_t = jnp.transpose(w_uk[0], (1, 2, 0)).astype(bf16)
    wuv_h = jnp.transpose(w_uv[0], (1, 0, 2)).astype(bf16)
    w_up_a = w_up[0, :A_WIDTH].astype(bf16)
    w_up_r = w_up[0, A_WIDTH:].astype(bf16)
    w_out_b = w_out[0].astype(bf16)
    w_pq_t = w_pq[0].T.astype(bf16)
    keys_b = sub_keys[0].astype(bf16)
    u_b = u_exp[0].astype(bf16)
    v_t = v_exp[0].T.astype(bf16)

    c8 = jnp.pad(c, ((0, 8 - BATCH), (0, 0)))
    mod = _ada(c8, w_ada[0], b_ada[0].reshape(1, N_MOD * D_MODEL))[:BATCH]
    modr = mod.reshape(BATCH * N_MOD, 1, D_MODEL)

    h = _normmod(x2d, g_mix[0].reshape(1, D_MODEL), modr, 0, 1)
    proj = _matmul(h, w_in_p, f32, tm=1024, tn=512, name="in_proj")
    gates = _matmul(h, w_gate_b, bf16, tm=1024, tn=512, epilogue=_gate_epilogue,
                    extra=(b_gate[0].reshape(1, 2 * D_MODEL),),
                    extra_specs=(pl.BlockSpec((1, 512), lambda i, j: (0, j)),), name="gates")

    cqn, ckvn, kidx, widx = _dsa_prep(proj, g_cq[0].reshape(1, -1), g_ckv[0].reshape(1, -1),
                                      g_ki[0].reshape(1, -1), b_ki[0].reshape(1, -1))
    qq = _matmul(cqn, w_q_all, bf16, tm=1024, tn=512, name="q_up")
    qlat = _qlat(qq, wuk_t)
    ptab = _t5_table(t5_bias)
    y_a = _dsa(t5_bias, qq, kidx.T, widx, qlat, ckvn, wuv_h, ptab)

    cc, ss, dmask, qdec, kdec, cdec = _retention_tables()
    y_r = _retention(proj, cc, ss, dmask, qdec, kdec, cdec, g_ret[0].reshape(R_HEADS, 1, R_V_DIM))

    merged = _merge(y_a, y_r, w_up_a, w_up_r, gates)
    per_b = SEQ // 1024
    x1 = _matmul(merged, w_out_b, f32, tm=1024, tn=512, epilogue=_resid_epilogue,
                 extra=(x2d, modr),
                 extra_specs=(pl.BlockSpec((1024, 512), lambda i, j: (i, j)),
                              pl.BlockSpec((1, 1, 512), lambda i, j: ((i // per_b) * N_MOD + 2, 0, j))),
                 name="out_proj")

    h2 = _normmod(x1, g_ffn[0].reshape(1, D_MODEL), modr, 3, 4)
    h2_t = h2.T
    q_t = _matmul(w_pq_t, h2_t, f32, tm=1024, tn=512, name="peer_q")
    s1, s2, a1, m2, thr = _peer_route(q_t, keys_b)
    y_t = _peer(h2_t, u_b, v_t, s1, a1, s2, m2, thr)
    out = _final(x1, y_t.T, modr, g_final.reshape(1, D_MODEL))
    return out.reshape(BATCH, SEQ, D_MODEL)
```

```python
import functools
import math

import numpy as np
import jax
import jax.numpy as jnp
from jax import lax
from jax.experimental import pallas as pl
from jax.experimental.pallas import tpu as pltpu

f32 = jnp.float32
bf16 = jnp.bfloat16
i32 = jnp.int32

D_MODEL = 4096
BATCH = 4
SEQ = 2048
TOKENS = BATCH * SEQ
CHUNK = 64
EPS = 1e-6
N_MOD = 6
A_HEADS = 16
A_HEAD_DIM = 128
A_Q_RANK = 1024
A_KV_RANK = 512
A_WIDTH = A_HEADS * A_HEAD_DIM
IDX_HEADS = 64
IDX_DIM = 128
TOPK = 256
T5_BUCKETS = 32
T5_MAX_DIST = 128
R_HEADS = 8
R_QK_DIM = 128
R_V_DIM = 256
R_QK_WIDTH = R_HEADS * R_QK_DIM
R_WIDTH = R_HEADS * R_V_DIM
ROT_BASE = 10000.0
P_HEADS = 8
P_QUERY_DIM = 256
P_NKEYS = 128
P_TOPK = 16
P_EXPERTS = P_NKEYS * P_NKEYS

PROJ_WIDTH = 8192
OFF_CQ, OFF_CKV, OFF_KI, OFF_WI = 0, 1024, 1536, 1664
OFF_RQ, OFF_RK, OFF_RV, OFF_RG = 2048, 3072, 4096, 6144

Q_BLOCK = 128
NEAR_BACK = 128
NEAR = NEAR_BACK + Q_BLOCK
ATTN_SCALE = A_HEAD_DIM ** -0.5
VMEM_LIMIT = 56 * 1024 * 1024
INT_MIN = -2147483648


def _params(*sem, flags=None):
    return pltpu.CompilerParams(dimension_semantics=sem, vmem_limit_bytes=VMEM_LIMIT, flags=flags)


def _ada_kernel(c_ref, w_ref, b_ref, o_ref):
    c = c_ref[...]
    ca = (c * jax.nn.sigmoid(c)).astype(bf16)
    o_ref[...] = jnp.dot(ca, w_ref[...].astype(bf16), preferred_element_type=f32) + b_ref[...]


def _ada(c8, w, b):
    n = w.shape[1]
    tn = 512
    return pl.pallas_call(
        _ada_kernel,
        grid=(n // tn,),
        in_specs=[pl.BlockSpec((8, D_MODEL), lambda j: (0, 0)),
                  pl.BlockSpec((D_MODEL, tn), lambda j: (0, j)),
                  pl.BlockSpec((1, tn), lambda j: (0, j))],
        out_specs=pl.BlockSpec((8, tn), lambda j: (0, j)),
        out_shape=jax.ShapeDtypeStruct((8, n), f32),
        compiler_params=_params("arbitrary"),
        name="ada_mod",
    )(c8, w, b)


def _normmod_kernel(x_ref, g_ref, sh_ref, sc_ref, o_ref, *, transposed):
    x = x_ref[...]
    y = x * lax.rsqrt(jnp.mean(x * x, axis=-1, keepdims=True) + EPS) * g_ref[...]
    y = y * (1.0 + sc_ref[0]) + sh_ref[0]
    o_ref[...] = (y.T if transposed else y).astype(o_ref.dtype)


def _normmod(x2d, g, modr, shift_slot, scale_slot, transposed=False):
    ts = 256
    per_b = SEQ // ts
    if transposed:
        out_spec = pl.BlockSpec((D_MODEL, ts), lambda i: (0, i))
        out_shape = jax.ShapeDtypeStruct((D_MODEL, TOKENS), bf16)
    else:
        out_spec = pl.BlockSpec((ts, D_MODEL), lambda i: (i, 0))
        out_shape = jax.ShapeDtypeStruct((TOKENS, D_MODEL), bf16)
    return pl.pallas_call(
        functools.partial(_normmod_kernel, transposed=transposed),
        grid=(TOKENS // ts,),
        in_specs=[pl.BlockSpec((ts, D_MODEL), lambda i: (i, 0)),
                  pl.BlockSpec((1, D_MODEL), lambda i: (0, 0)),
                  pl.BlockSpec((1, 1, D_MODEL), lambda i: ((i // per_b) * N_MOD + shift_slot, 0, 0)),
                  pl.BlockSpec((1, 1, D_MODEL), lambda i: ((i // per_b) * N_MOD + scale_slot, 0, 0))],
        out_specs=out_spec,
        out_shape=out_shape,
        compiler_params=_params("arbitrary"),
        name="norm_modulate",
    )(x2d, g, modr, modr)


def _mm_kernel(a_ref, w_ref, *rest, epilogue):
    o_ref = rest[-1]
    acc = jnp.dot(a_ref[...], w_ref[...], preferred_element_type=f32)
    o_ref[...] = epilogue(acc, *rest[:-1]).astype(o_ref.dtype)


def _matmul(a, w, out_dtype, *, tm, tn, epilogue=None, extra=(), extra_specs=(), name):
    m, k = a.shape
    n = w.shape[1]
    if epilogue is None:
        epilogue = lambda acc: acc
    return pl.pallas_call(
        functools.partial(_mm_kernel, epilogue=epilogue),
        grid=(m // tm, n // tn),
        in_specs=[pl.BlockSpec((tm, k), lambda i, j: (i, 0)),
                  pl.BlockSpec((k, tn), lambda i, j: (0, j)),
                  *extra_specs],
        out_specs=pl.BlockSpec((tm, tn), lambda i, j: (i, j)),
        out_shape=jax.ShapeDtypeStruct((m, n), out_dtype),
        compiler_params=_params("arbitrary", "arbitrary"),
        name=name,
    )(a, w, *extra)


def _mm_w32_kernel(a_ref, w_ref, *rest, epilogue):
    o_ref, wb_ref = rest[-2], rest[-1]

    @pl.when(pl.program_id(1) == 0)
    def _():
        wb_ref[...] = w_ref[...].astype(bf16)

    acc = jnp.dot(a_ref[...], wb_ref[...], preferred_element_type=f32)
    o_ref[...] = epilogue(acc, *rest[:-2]).astype(o_ref.dtype)


def _matmul_w32(a, w, out_dtype, *, tm, tn, epilogue=None, extra=(), extra_specs=(), name):
    m, k = a.shape
    n = w.shape[1]
    if epilogue is None:
        epilogue = lambda acc: acc
    return pl.pallas_call(
        functools.partial(_mm_w32_kernel, epilogue=epilogue),
        grid=(n // tn, m // tm),
        in_specs=[pl.BlockSpec((tm, k), lambda j, i: (i, 0)),
                  pl.BlockSpec((k, tn), lambda j, i: (0, j)),
                  *extra_specs],
        out_specs=pl.BlockSpec((tm, tn), lambda j, i: (i, j)),
        out_shape=jax.ShapeDtypeStruct((m, n), out_dtype),
        scratch_shapes=[pltpu.VMEM((k, tn), bf16)],
        compiler_params=_params("arbitrary", "arbitrary"),
        name=name,
    )(a, w, *extra)


def _gate_epilogue(acc, b_ref):
    return jax.nn.sigmoid(acc + b_ref[...])


def _resid_epilogue(acc, x_ref, g_ref):
    return x_ref[...] + g_ref[0] * acc


def _rms(x, g):
    return x * lax.rsqrt(jnp.mean(x * x, axis=-1, keepdims=True) + EPS) * g


def _dsa_prep_kernel(p_ref, gcq_ref, gckv_ref, gki_ref, bki_ref, cq_o, ckv_o, ki_o, wi_o):
    cq_o[...] = _rms(p_ref[:, OFF_CQ:OFF_CQ + A_Q_RANK], gcq_ref[...]).astype(bf16)
    ckv_o[...] = _rms(p_ref[:, OFF_CKV:OFF_CKV + A_KV_RANK], gckv_ref[...]).astype(bf16)
    ki = p_ref[:, OFF_KI:OFF_KI + IDX_DIM]
    mu = jnp.mean(ki, axis=-1, keepdims=True)
    var = jnp.mean(jnp.square(ki - mu), axis=-1, keepdims=True)
    ki_o[...] = ((ki - mu) * lax.rsqrt(var + EPS) * gki_ref[...] + bki_ref[...]).astype(bf16)
    wi_o[...] = p_ref[:, OFF_WI:OFF_WI + 128] * (IDX_HEADS ** -0.5 * IDX_DIM ** -0.5)


def _dsa_prep(proj, g_cq, g_ckv, g_ki, b_ki):
    ts = 512
    row = lambda i: (i, 0)
    fixed = lambda i: (0, 0)
    return pl.pallas_call(
        _dsa_prep_kernel,
        grid=(TOKENS // ts,),
        in_specs=[pl.BlockSpec((ts, 2048), row),
                  pl.BlockSpec((1, A_Q_RANK), fixed),
                  pl.BlockSpec((1, A_KV_RANK), fixed),
                  pl.BlockSpec((1, IDX_DIM), fixed),
                  pl.BlockSpec((1, IDX_DIM), fixed)],
        out_specs=[pl.BlockSpec((ts, A_Q_RANK), row),
                   pl.BlockSpec((ts, A_KV_RANK), row),
                   pl.BlockSpec((ts, IDX_DIM), row),
                   pl.BlockSpec((ts, 128), row)],
        out_shape=[jax.ShapeDtypeStruct((TOKENS, A_Q_RANK), bf16),
                   jax.ShapeDtypeStruct((TOKENS, A_KV_RANK), bf16),
                   jax.ShapeDtypeStruct((TOKENS, IDX_DIM), bf16),
                   jax.ShapeDtypeStruct((TOKENS, 128), f32)],
        compiler_params=_params("arbitrary"),
        name="dsa_prep",
    )(proj, g_cq, g_ckv, g_ki, b_ki)


def _qlat_kernel(q_ref, w_ref, o_ref):
    r = jnp.dot(q_ref[...], w_ref[0], preferred_element_type=f32).astype(o_ref.dtype)
    o_ref[:, 0] = r.reshape(o_ref.shape[0], Q_BLOCK, A_KV_RANK)


def _qlat(qq, wuk_t):
    tm = 1024
    q_col0 = IDX_HEADS
    return pl.pallas_call(
        _qlat_kernel,
        grid=(TOKENS // tm, A_HEADS),
        in_specs=[pl.BlockSpec((tm, A_HEAD_DIM), lambda i, h: (i, q_col0 + h)),
                  pl.BlockSpec((1, A_HEAD_DIM, A_KV_RANK), lambda i, h: (h, 0, 0))],
        out_specs=pl.BlockSpec((tm // Q_BLOCK, 1, Q_BLOCK, A_KV_RANK), lambda i, h: (i, h, 0, 0)),
        out_shape=jax.ShapeDtypeStruct((TOKENS // Q_BLOCK, A_HEADS, Q_BLOCK, A_KV_RANK), bf16),
        compiler_params=_params("arbitrary", "arbitrary"),
        name="q_lat",
    )(qq, wuk_t)


def _t5_kernel(t5_ref, o_ref):
    h = pl.program_id(0)
    half = T5_BUCKETS // 2
    exact = half // 2
    qi = lax.broadcasted_iota(i32, (Q_BLOCK, NEAR), 0)
    kj = lax.broadcasted_iota(i32, (Q_BLOCK, NEAR), 1)
    rel = kj - NEAR_BACK - qi
    n = jnp.abs(rel)
    log_ratio = jnp.log(jnp.maximum(n, 1).astype(f32) / exact) / math.log(T5_MAX_DIST / exact)
    large = jnp.minimum(exact + (log_ratio * (half - exact)).astype(i32), half - 1)
    bucket = jnp.where(rel > 0, half, 0) + jnp.where(n < exact, n, large)
    acc = jnp.zeros((Q_BLOCK, NEAR), f32)
    for k in range(T5_BUCKETS):
        acc = jnp.where(bucket == k, t5_ref[k, h], acc)
    o_ref[0] = (acc - t5_ref[half - 1, h]) * (1.0 / ATTN_SCALE)


def _t5_table(t5_bias):
    return pl.pallas_call(
        _t5_kernel,
        grid=(A_HEADS,),
        in_specs=[pl.BlockSpec(memory_space=pltpu.SMEM)],
        out_specs=pl.BlockSpec((1, Q_BLOCK, NEAR), lambda h: (h, 0, 0)),
        out_shape=jax.ShapeDtypeStruct((A_HEADS, Q_BLOCK, NEAR), f32),
        compiler_params=_params("arbitrary"),
        name="t5_table",
    )(t5_bias)


DSA_COL_STEP = 512
DSA_HEAD_GROUP = 8


def _dsa_block(width, qb, t5_ref, qi_ref, kit_ref, wi_ref, ql_ref, kv_ref, wuv_ref, pt_ref, o_ref,
               key_ref, mask_ref, lg_ref, e_ref, l_ref):
    q0 = qb * Q_BLOCK
    ct_w = 256
    w = wi_ref[...]
    score_ref = lg_ref.at[0:Q_BLOCK]

    def col_tile(ct, carry):
        c0 = pl.multiple_of(ct * ct_w, ct_w)
        kt = kit_ref[:, pl.ds(c0, ct_w)]
        acc = jnp.zeros((Q_BLOCK, ct_w), f32)
        for h in range(IDX_HEADS):
            x = jnp.dot(qi_ref[:, h * IDX_DIM:(h + 1) * IDX_DIM], kt, preferred_element_type=f32)
            acc = acc + jnp.maximum(x, 0.0) * w[:, h:h + 1]
        score_ref[:, pl.ds(c0, ct_w)] = acc
        return carry

    lax.fori_loop(0, width // ct_w, col_tile, 0)

    row = lax.broadcasted_iota(i32, (Q_BLOCK, width), 0)
    col = lax.broadcasted_iota(i32, (Q_BLOCK, width), 1)
    adm = (col // CHUNK) <= ((q0 + row) // CHUNK)
    score = jnp.where(adm, score_ref[:, 0:width], -jnp.inf)
    bits = pltpu.bitcast(score, i32)
    key_ref[:, 0:width] = jnp.where(bits < 0, bits ^ 0x7FFFFFFF, bits)

    def bisect(b, thr_u):
        cand_u = thr_u | jnp.left_shift(jnp.int32(1), 31 - b)
        hit = jnp.where(key_ref[:, 0:width] >= (cand_u ^ INT_MIN), 1.0, 0.0)
        cnt = jnp.sum(hit, axis=1, keepdims=True)
        return jnp.where(cnt >= TOPK, cand_u, thr_u)

    thr_u = lax.fori_loop(0, 32, bisect, jnp.zeros((Q_BLOCK, 1), i32))
    sel = (key_ref[:, 0:width] >= (thr_u ^ INT_MIN)) & adm
    mask_ref[:, 0:width] = jnp.where(sel, 0.0, -jnp.inf)

    kv = kv_ref[0:width, :]
    rows = DSA_HEAD_GROUP * Q_BLOCK
    for g in range(A_HEADS // DSA_HEAD_GROUP):
        heads = range(g * DSA_HEAD_GROUP, (g + 1) * DSA_HEAD_GROUP)
        qg = ql_ref[0, g * DSA_HEAD_GROUP:(g + 1) * DSA_HEAD_GROUP].reshape(rows, A_KV_RANK)
        lg_ref[0:rows, 0:width] = lax.dot_general(qg, kv, (((1,), (1,)), ((), ())),
                                                  preferred_element_type=f32)

        @pl.when(qb == 0)
        def _():
            for hh, h in enumerate(heads):
                lg_ref[hh * Q_BLOCK:(hh + 1) * Q_BLOCK, 0:Q_BLOCK] += pt_ref[h, :, NEAR_BACK:NEAR]

        @pl.when(qb > 0)
        def _():
            w0 = pl.multiple_of(q0 - NEAR_BACK, 128)
            for hh, h in enumerate(heads):
                lg_ref[hh * Q_BLOCK:(hh + 1) * Q_BLOCK, pl.ds(w0, NEAR)] += pt_ref[h]

        for hh, h in enumerate(heads):
            rs = slice(hh * Q_BLOCK, (hh + 1) * Q_BLOCK)
            x = lg_ref[rs, 0:width] * ATTN_SCALE + (mask_ref[:, 0:width] + t5_ref[T5_BUCKETS // 2 - 1, h])
            m = jnp.max(x, axis=1, keepdims=True)
            e = jnp.exp(x - m)
            l_ref[rs, :] = jnp.sum(e, axis=1, keepdims=True)
            e_ref[rs, 0:width] = e.astype(bf16)

        ol = jnp.dot(e_ref[0:rows, 0:width], kv, preferred_element_type=f32) / l_ref[0:rows, :]
        for hh, h in enumerate(heads):
            o = jnp.dot(ol[hh * Q_BLOCK:(hh + 1) * Q_BLOCK].astype(bf16), wuv_ref[h],
                        preferred_element_type=f32)
            o_ref[:, h * A_HEAD_DIM:(h + 1) * A_HEAD_DIM] = o.astype(o_ref.dtype)


def _dsa_kernel(t5_ref, qi_ref, kit_ref, wi_ref, ql_ref, kv_ref, wuv_ref, pt_ref, o_ref,
                key_ref, mask_ref, lg_ref, e_ref, l_ref):
    qb = pl.program_id(1)
    blocks_per_step = DSA_COL_STEP // Q_BLOCK
    for n in range(SEQ // DSA_COL_STEP):
        pl.when(qb // blocks_per_step == n)(functools.partial(
            _dsa_block, (n + 1) * DSA_COL_STEP, qb, t5_ref, qi_ref, kit_ref, wi_ref, ql_ref, kv_ref,
            wuv_ref, pt_ref, o_ref, key_ref, mask_ref, lg_ref, e_ref, l_ref))


def _dsa(t5_bias, qq, kidx_t, widx, qlat, ckv, wuv, ptab):
    nqb = SEQ // Q_BLOCK
    return pl.pallas_call(
        _dsa_kernel,
        grid=(BATCH, nqb),
        in_specs=[pl.BlockSpec(memory_space=pltpu.SMEM),
                  pl.BlockSpec((Q_BLOCK, IDX_HEADS * IDX_DIM), lambda b, i: (b * nqb + i, 0)),
                  pl.BlockSpec((IDX_DIM, SEQ), lambda b, i: (0, b)),
                  pl.BlockSpec((Q_BLOCK, 128), lambda b, i: (b * nqb + i, 0)),
                  pl.BlockSpec((1, A_HEADS, Q_BLOCK, A_KV_RANK), lambda b, i: (b * nqb + i, 0, 0, 0)),
                  pl.BlockSpec((SEQ, A_KV_RANK), lambda b, i: (b, 0)),
                  pl.BlockSpec((A_HEADS, A_KV_RANK, A_HEAD_DIM), lambda b, i: (0, 0, 0)),
                  pl.BlockSpec((A_HEADS, Q_BLOCK, NEAR), lambda b, i: (0, 0, 0))],
        out_specs=pl.BlockSpec((Q_BLOCK, A_WIDTH), lambda b, i: (b * nqb + i, 0)),
        out_shape=jax.ShapeDtypeStruct((TOKENS, A_WIDTH), bf16),
        scratch_shapes=[pltpu.VMEM((Q_BLOCK, SEQ), i32),
                        pltpu.VMEM((Q_BLOCK, SEQ), f32),
                        pltpu.VMEM((DSA_HEAD_GROUP * Q_BLOCK, SEQ), f32),
                        pltpu.VMEM((DSA_HEAD_GROUP * Q_BLOCK, SEQ), bf16),
                        pltpu.VMEM((DSA_HEAD_GROUP * Q_BLOCK, 1), f32)],
        compiler_params=_params("arbitrary", "arbitrary"),
        name="dsa_main",
    )(t5_bias, qq, kidx_t, widx, qlat, ckv, wuv, ptab)


def _ret_kernel(q_ref, k_ref, v_ref, g_ref, cc_ref, ss_ref, dm_ref, qd_ref, kd_ref, cd_ref, gr_ref,
                o_ref, qs_ref, ks_ref):
    cc = cc_ref[...]
    ss = ss_ref[...]
    q = q_ref[...]
    k = k_ref[...] * (R_QK_DIM ** -0.5)
    half = R_QK_DIM // 2
    qs_ref[...] = q * cc + pltpu.roll(q, half, 1) * ss
    ks_ref[...] = k * cc + pltpu.roll(k, half, 1) * ss
    dm = dm_ref[0]
    qd = qd_ref[0]
    kd = kd_ref[0]
    cd = cd_ref[0]
    gr = gr_ref[0]

    def chunk(n, state):
        r0 = pl.multiple_of(n * CHUNK, CHUNK)
        qn = qs_ref[pl.ds(r0, CHUNK), :]
        kn = ks_ref[pl.ds(r0, CHUNK), :]
        vn = v_ref[pl.ds(r0, CHUNK), :].astype(bf16)
        att = lax.dot_general(qn.astype(bf16), kn.astype(bf16), (((1,), (1,)), ((), ())),
                              preferred_element_type=f32) * dm
        y = jnp.dot(att.astype(bf16), vn, preferred_element_type=f32)
        y = y + jnp.dot((qn * qd).astype(bf16), state.astype(bf16), preferred_element_type=f32)
        kv = lax.dot_general((kn * kd).astype(bf16), vn, (((0,), (0,)), ((), ())),
                             preferred_element_type=f32)
        y = y * lax.rsqrt(jnp.mean(y * y, axis=-1, keepdims=True) + EPS) * gr
        gate = g_ref[pl.ds(r0, CHUNK), :]
        o_ref[pl.ds(r0, CHUNK), :] = (gate * jax.nn.sigmoid(gate) * y).astype(o_ref.dtype)
        return state * cd + kv

    lax.fori_loop(0, SEQ // CHUNK, chunk, jnp.zeros((R_QK_DIM, R_V_DIM), f32), unroll=8)


def _retention(proj, cc, ss, dmask, qdec, kdec, cdec, g_ret):
    qk_blk = OFF_RQ // R_QK_DIM
    k_blk = OFF_RK // R_QK_DIM
    v_blk = OFF_RV // R_V_DIM
    g_blk = OFF_RG // R_V_DIM
    per_head = lambda b, h: (h, 0, 0)
    return pl.pallas_call(
        _ret_kernel,
        grid=(BATCH, R_HEADS),
        in_specs=[pl.BlockSpec((SEQ, R_QK_DIM), lambda b, h: (b, qk_blk + h)),
                  pl.BlockSpec((SEQ, R_QK_DIM), lambda b, h: (b, k_blk + h)),
                  pl.BlockSpec((SEQ, R_V_DIM), lambda b, h: (b, v_blk + h)),
                  pl.BlockSpec((SEQ, R_V_DIM), lambda b, h: (b, g_blk + h)),
                  pl.BlockSpec((SEQ, R_QK_DIM), lambda b, h: (0, 0)),
                  pl.BlockSpec((SEQ, R_QK_DIM), lambda b, h: (0, 0)),
                  pl.BlockSpec((1, CHUNK, CHUNK), per_head),
                  pl.BlockSpec((1, CHUNK, R_QK_DIM), per_head),
                  pl.BlockSpec((1, CHUNK, R_QK_DIM), per_head),
                  pl.BlockSpec((1, 1, R_V_DIM), per_head),
                  pl.BlockSpec((1, 1, R_V_DIM), per_head)],
        out_specs=pl.BlockSpec((SEQ, R_V_DIM), lambda b, h: (b, h)),
        out_shape=jax.ShapeDtypeStruct((TOKENS, R_WIDTH), bf16),
        scratch_shapes=[pltpu.VMEM((SEQ, R_QK_DIM), f32), pltpu.VMEM((SEQ, R_QK_DIM), f32)],
        compiler_params=_params("arbitrary", "arbitrary"),
        name="retention",
    )(proj, proj, proj, proj, cc, ss, dmask, qdec, kdec, cdec, g_ret)


def _merge_kernel(ya_ref, yr_ref, wa_ref, wr_ref, ga_ref, gr_ref, o_ref, wab_ref, wrb_ref):
    @pl.when(pl.program_id(1) == 0)
    def _():
        wab_ref[...] = wa_ref[...].astype(bf16)
        wrb_ref[...] = wr_ref[...].astype(bf16)

    pa = jnp.dot(ya_ref[...], wab_ref[...], preferred_element_type=f32)
    pr = jnp.dot(yr_ref[...], wrb_ref[...], preferred_element_type=f32)
    o_ref[...] = (ga_ref[...].astype(f32) * pa + gr_ref[...].astype(f32) * pr).astype(o_ref.dtype)


def _merge(ya, yr, w_up, gates):
    assert A_WIDTH == R_WIDTH
    tm, tn = 1024, 512
    nj = D_MODEL // tn
    return pl.pallas_call(
        _merge_kernel,
        grid=(nj, TOKENS // tm),
        in_specs=[pl.BlockSpec((tm, A_WIDTH), lambda j, i: (i, 0)),
                  pl.BlockSpec((tm, R_WIDTH), lambda j, i: (i, 0)),
                  pl.BlockSpec((A_WIDTH, tn), lambda j, i: (0, j)),
                  pl.BlockSpec((R_WIDTH, tn), lambda j, i: (1, j)),
                  pl.BlockSpec((tm, tn), lambda j, i: (i, j)),
                  pl.BlockSpec((tm, tn), lambda j, i: (i, nj + j))],
        out_specs=pl.BlockSpec((tm, tn), lambda j, i: (i, j)),
        out_shape=jax.ShapeDtypeStruct((TOKENS, D_MODEL), bf16),
        scratch_shapes=[pltpu.VMEM((A_WIDTH, tn), bf16), pltpu.VMEM((R_WIDTH, tn), bf16)],
        compiler_params=_params("arbitrary", "arbitrary"),
        name="merge_up",
    )(ya, yr, w_up, w_up, gates, gates)


ROUTE_CAND_GROUPS = ((0, 16), (1, 8), (2, 8), (3, 8), (4, 8), (5, 8), (6, 8), (7, 8))
ROUTE_CAND_ROWS = sum(n for _, n in ROUTE_CAND_GROUPS) + 8


def _peer_route_kernel(q_ref, k_ref, s1_o, s2_o, a1_o, m2_o, thr_o, v1_ref, v2_ref, cand_ref, ec_ref):
    hq = P_QUERY_DIM // 2
    s1_all = jnp.dot(k_ref[0], q_ref[0:hq, :].astype(bf16), preferred_element_type=f32)
    s2_all = jnp.dot(k_ref[1], q_ref[hq:P_QUERY_DIM, :].astype(bf16), preferred_element_type=f32)

    def top_values(s, v_ref):
        cur = s
        for r in range(P_TOPK):
            m = jnp.max(cur, axis=0, keepdims=True)
            v_ref[r:r + 1, :] = m
            cur = jnp.where(cur == m, -jnp.inf, cur)

    for lt in range(s1_all.shape[1] // 128):
        ls = slice(lt * 128, (lt + 1) * 128)
        s1 = s1_all[:, ls]
        s2 = s2_all[:, ls]
        top_values(s1, v1_ref)
        top_values(s2, v2_ref)
        v1 = v1_ref[...]
        v2 = v2_ref[...]
        m1 = v1[0:1]
        m2 = v2[0:1]
        e1 = jnp.exp(v1 - m1)
        e2 = jnp.exp(v2 - m2)
        off = 0
        for r1, n in ROUTE_CAND_GROUPS:
            cand_ref[off:off + n, :] = v1[r1:r1 + 1] + v2[0:n]
            ec_ref[off:off + n, :] = e1[r1:r1 + 1] * e2[0:n]
            off += n
        cand_ref[off:off + 8, :] = v1[8:16] + v2[0:1]
        ec_ref[off:off + 8, :] = e1[8:16] * e2[0:1]
        cand = cand_ref[...]
        cur = cand
        thr = None
        for r in range(P_TOPK):
            thr = jnp.max(cur, axis=0, keepdims=True)
            cur = jnp.where(cur == thr, -jnp.inf, cur)
        z = jnp.sum(jnp.where(cand >= thr, ec_ref[...], 0.0), axis=0, keepdims=True)
        a1 = jnp.exp(s1 - m1) / z
        for r in range(PEER_NC):
            s1_o[0, r, :, ls] = s1[r * PEER_ROWS:(r + 1) * PEER_ROWS]
            a1_o[0, r, :, ls] = a1[r * PEER_ROWS:(r + 1) * PEER_ROWS]
        s2_o[0, lt] = s2
        m2_o[0, :, ls] = m2
        thr_o[0, :, ls] = thr


def _peer_route(q_t, keys):
    tl = 512
    big = lambda h, j: (h, 0, j)
    chunked = lambda h, j: (h, 0, 0, j)
    chunked_shape = jax.ShapeDtypeStruct((P_HEADS, PEER_NC, PEER_ROWS, TOKENS), f32)
    row_shape = jax.ShapeDtypeStruct((P_HEADS, 1, TOKENS), f32)
    return pl.pallas_call(
        _peer_route_kernel,
        grid=(P_HEADS, TOKENS // tl),
        in_specs=[pl.BlockSpec((P_QUERY_DIM, tl), lambda h, j: (h, j)),
                  pl.BlockSpec((2, P_NKEYS, P_QUERY_DIM // 2), lambda h, j: (0, 0, 0))],
        out_specs=[pl.BlockSpec((1, PEER_NC, PEER_ROWS, tl), chunked),
                   pl.BlockSpec((1, tl // 128, P_NKEYS, 128), lambda h, j: (h, j, 0, 0)),
                   pl.BlockSpec((1, PEER_NC, PEER_ROWS, tl), chunked),
                   pl.BlockSpec((1, 1, tl), big),
                   pl.BlockSpec((1, 1, tl), big)],
        out_shape=[chunked_shape, jax.ShapeDtypeStruct((P_HEADS, TOKENS // 128, P_NKEYS, 128), f32),
                   chunked_shape, row_shape, row_shape],
        scratch_shapes=[pltpu.VMEM((P_TOPK, 128), f32), pltpu.VMEM((P_TOPK, 128), f32),
                        pltpu.VMEM((ROUTE_CAND_ROWS, 128), f32), pltpu.VMEM((ROUTE_CAND_ROWS, 128), f32)],
        compiler_params=_params("arbitrary", "arbitrary"),
        name="peer_route",
    )(q_t, keys)


PEER_TM = 512
PEER_TE = 512
PEER_NC = P_EXPERTS // PEER_TE
PEER_ROWS = PEER_TE // P_NKEYS
PEER_KT = 64
PEER_KPIECES = 8


def _peer_kernel(ht_ref, u_ref, vt_ref, s1_ref, a1_ref, s2_ref, m2_ref, thr_ref, o_ref,
                 b2_ref, pre_ref, w_ref):
    c = pl.program_id(1)

    @pl.when(c == 0)
    def _():
        o_ref[...] = jnp.zeros_like(o_ref)
        for lt in range(PEER_TM // 128):
            b2_ref[:, lt] = jnp.exp(s2_ref[:, lt] - m2_ref[:, :, lt * 128:(lt + 1) * 128])

    @pl.when(c > 0)
    def _():
        pre = pre_ref[...]
        act = 0.5 * pre * (1.0 + lax.erf(pre * np.float32(np.sqrt(0.5))))
        coef = (w_ref[0:PEER_TE, :] * act).astype(bf16)
        o_ref[...] += jnp.dot(vt_ref[...], coef, preferred_element_type=f32)

    @pl.when(c < PEER_NC)
    def _():
        z = pl.multiple_of(jnp.minimum(c, 0), PEER_TE)
        units = [(j, lt, kt) for j in range(PEER_ROWS) for lt in range(PEER_TM // 128)
                 for kt in range(P_NKEYS // PEER_KT)]
        per_piece = len(units) // PEER_KPIECES
        kw = D_MODEL // PEER_KPIECES
        for kq in range(PEER_KPIECES):
            part = jnp.dot(u_ref[:, kq * kw:(kq + 1) * kw], ht_ref[kq * kw:(kq + 1) * kw, :],
                           preferred_element_type=f32)
            if kq == 0:
                pre_ref[...] = part
            else:
                pre_ref[...] += part
            w_ref[pl.ds(z + PEER_TE, 8), 0:128] = part[PEER_TE - 8:PEER_TE, PEER_TM - 128:PEER_TM]
            for j, lt, kt in units[kq * per_piece:(kq + 1) * per_piece]:
                ls = slice(lt * 128, (lt + 1) * 128)
                ks = slice(kt * PEER_KT, (kt + 1) * PEER_KT)
                wj = jnp.zeros((PEER_KT, 128), f32)
                for h in range(P_HEADS):
                    sel = (s1_ref[h, 0, j:j + 1, ls] + s2_ref[h, lt, ks, :]) >= thr_ref[h, :, ls]
                    wj = wj + jnp.where(sel, b2_ref[h, lt, ks, :], 0.0) * a1_ref[h, 0, j:j + 1, ls]
                w0 = pl.multiple_of(z + (j * P_NKEYS + kt * PEER_KT), PEER_KT)
                w_ref[pl.ds(w0, PEER_KT), ls] = wj


def _peer(h_t, u, v_t, s1, a1, s2, m2, thr):
    tm, te, nc = PEER_TM, PEER_TE, PEER_NC
    s1r, a1r = s1, a1
    tok3 = lambda i, c: (0, 0, i)
    cur = lambda i, c: (0, jnp.minimum(c, nc - 1), 0, i)
    return pl.pallas_call(
        _peer_kernel,
        grid=(TOKENS // tm, nc + 1),
        in_specs=[pl.BlockSpec((D_MODEL, tm), lambda i, c: (0, i)),
                  pl.BlockSpec((te, D_MODEL), lambda i, c: (jnp.minimum(c, nc - 1), 0)),
                  pl.BlockSpec((D_MODEL, te), lambda i, c: (0, jnp.maximum(c - 1, 0))),
                  pl.BlockSpec((P_HEADS, 1, PEER_ROWS, tm), cur),
                  pl.BlockSpec((P_HEADS, 1, PEER_ROWS, tm), cur),
                  pl.BlockSpec((P_HEADS, tm // 128, P_NKEYS, 128), lambda i, c: (0, i, 0, 0)),
                  pl.BlockSpec((P_HEADS, 1, tm), tok3),
                  pl.BlockSpec((P_HEADS, 1, tm), tok3)],
        out_specs=pl.BlockSpec((D_MODEL, tm), lambda i, c: (0, i)),
        out_shape=jax.ShapeDtypeStruct((D_MODEL, TOKENS), f32),
        scratch_shapes=[pltpu.VMEM((P_HEADS, tm // 128, P_NKEYS, 128), f32),
                        pltpu.VMEM((te, tm), f32), pltpu.VMEM((te + 8, tm), f32)],
        compiler_params=_params("arbitrary", "arbitrary"),
        name="peer_experts",
    )(h_t, u, v_t, s1r, a1r, s2, m2, thr)


def _final_kernel(x_ref, yt_ref, gt_ref, g_ref, o_ref):
    x = x_ref[...] + gt_ref[0] * yt_ref[...].T
    o_ref[...] = x * lax.rsqrt(jnp.mean(x * x, axis=-1, keepdims=True) + EPS) * g_ref[...]


def _final(x1, y_t, modr, g_final):
    ts = 256
    per_b = SEQ // ts
    return pl.pallas_call(
        _final_kernel,
        grid=(TOKENS // ts,),
        in_specs=[pl.BlockSpec((ts, D_MODEL), lambda i: (i, 0)),
                  pl.BlockSpec((D_MODEL, ts), lambda i: (0, i)),
                  pl.BlockSpec((1, 1, D_MODEL), lambda i: ((i // per_b) * N_MOD + 5, 0, 0)),
                  pl.BlockSpec((1, D_MODEL), lambda i: (0, 0))],
        out_specs=pl.BlockSpec((ts, D_MODEL), lambda i: (i, 0)),
        out_shape=jax.ShapeDtypeStruct((TOKENS, D_MODEL), f32),
        compiler_params=_params("arbitrary"),
        name="final_norm",
    )(x1, y_t, modr, g_final)


def _retention_tables():
    half = R_QK_DIM // 2
    inv = 1.0 / (ROT_BASE ** jnp.linspace(0.0, 1.0, half, dtype=f32))
    ang = jnp.arange(SEQ, dtype=f32)[:, None] * inv[None, :]
    cos, sin = jnp.cos(ang), jnp.sin(ang)
    cc = jnp.concatenate([cos, cos], axis=-1)
    ss = jnp.concatenate([-sin, sin], axis=-1)
    log_g = jnp.log(1.0 - jnp.power(2.0, -5.0 - jnp.arange(R_HEADS, dtype=f32)))
    j = jnp.arange(CHUNK, dtype=f32)
    diff = j[:, None] - j[None, :]
    dmask = jnp.where(diff[None] >= 0, jnp.exp(jnp.maximum(diff, 0.0)[None] * log_g[:, None, None]), 0.0)
    kdec = jnp.exp((CHUNK - 1.0 - j)[None, :] * log_g[:, None])
    qdec = jnp.exp((j + 1.0)[None, :] * log_g[:, None])
    cdec = jnp.exp(CHUNK * log_g)
    kdec = jnp.broadcast_to(kdec[:, :, None], (R_HEADS, CHUNK, R_QK_DIM))
    qdec = jnp.broadcast_to(qdec[:, :, None], (R_HEADS, CHUNK, R_QK_DIM))
    cdec = jnp.broadcast_to(cdec[:, None, None], (R_HEADS, 1, R_V_DIM))
    return cc, ss, dmask, qdec, kdec, cdec


def kernel(x, c, w_ada, b_ada, g_mix, w_in, g_cq, g_ckv, w_uq, w_uk, w_uv, w_qi, g_ki, b_ki, t5_bias, g_ret,
           w_up, w_gate, b_gate, w_out, g_ffn, w_pq, sub_keys, u_exp, v_exp, g_final):
    x2d = x.reshape(TOKENS, D_MODEL)

    w_in_l = w_in[0]
    zeros = lambda n: jnp.zeros((D_MODEL, n), w_in_l.dtype)
    w_in_p = jnp.concatenate(
        [w_in_l[:, :1728], zeros(OFF_RQ - 1728), w_in_l[:, 1728:]], axis=1).astype(bf16)
    w_q_all = jnp.concatenate(
        [w_qi[0].reshape(A_Q_RANK, IDX_HEADS * IDX_DIM), w_uq[0].reshape(A_Q_RANK, A_WIDTH)], axis=1).astype(bf16)
    wuk_t = jnp.transpose(w_uk[0], (1, 2, 0)).astype(bf16)
    wuv_h = jnp.transpose(w_uv[0], (1, 0, 2)).astype(bf16)
    w_pq_t = w_pq[0].T.astype(bf16)
    keys_b = sub_keys[0].astype(bf16)
    u_b = u_exp[0].astype(bf16)
    v_t = v_exp[0].T.astype(bf16)

    c8 = jnp.pad(c, ((0, 8 - BATCH), (0, 0)))
    mod = _ada(c8, w_ada[0], b_ada[0].reshape(1, N_MOD * D_MODEL))[:BATCH]
    modr = mod.reshape(BATCH * N_MOD, 1, D_MODEL)

    h = _normmod(x2d, g_mix[0].reshape(1, D_MODEL), modr, 0, 1)
    proj = _matmul(h, w_in_p, f32, tm=1024, tn=512, name="in_proj")
    gates = _matmul_w32(h, w_gate[0], bf16, tm=1024, tn=512, epilogue=_gate_epilogue,
                        extra=(b_gate[0].reshape(1, 2 * D_MODEL),),
                        extra_specs=(pl.BlockSpec((1, 512), lambda j, i: (0, j)),), name="gates")

    cqn, ckvn, kidx, widx = _dsa_prep(proj, g_cq[0].reshape(1, -1), g_ckv[0].reshape(1, -1),
                                      g_ki[0].reshape(1, -1), b_ki[0].reshape(1, -1))
    qq = _matmul(cqn, w_q_all, bf16, tm=1024, tn=512, name="q_up")
    qlat = _qlat(qq, wuk_t)
    ptab = _t5_table(t5_bias)
    y_a = _dsa(t5_bias, qq, kidx.T, widx, qlat, ckvn, wuv_h, ptab)

    cc, ss, dmask, qdec, kdec, cdec = _retention_tables()
    y_r = _retention(proj, cc, ss, dmask, qdec, kdec, cdec, g_ret[0].reshape(R_HEADS, 1, R_V_DIM))

    merged = _merge(y_a, y_r, w_up[0], gates)
    per_b = SEQ // 1024
    x1 = _matmul_w32(merged, w_out[0], f32, tm=1024, tn=512, epilogue=_resid_epilogue,
                     extra=(x2d, modr),
                     extra_specs=(pl.BlockSpec((1024, 512), lambda j, i: (i, j)),
                                  pl.BlockSpec((1, 1, 512), lambda j, i: ((i // per_b) * N_MOD + 2, 0, j))),
                     name="out_proj")

    h2_t = _normmod(x1, g_ffn[0].reshape(1, D_MODEL), modr, 3, 4, transposed=True)
    q_t = _matmul(w_pq_t, h2_t, f32, tm=1024, tn=512, name="peer_q")
    s1, s2, a1, m2, thr = _peer_route(q_t, keys_b)
    y_t = _peer(h2_t, u_b, v_t, s1, a1, s2, m2, thr)
    out = _final(x1, y_t, modr, g_final.reshape(1, D_MODEL))
    return out.reshape(BATCH, SEQ, D_MODEL)
```

```python
import functools
import math

import numpy as np
import jax
import jax.numpy as jnp
from jax import lax
from jax.experimental import pallas as pl
from jax.experimental.pallas import tpu as pltpu

f32 = jnp.float32
bf16 = jnp.bfloat16
i32 = jnp.int32

D_MODEL = 4096
BATCH = 4
SEQ = 2048
TOKENS = BATCH * SEQ
CHUNK = 64
EPS = 1e-6
N_MOD = 6
A_HEADS = 16
A_HEAD_DIM = 128
A_Q_RANK = 1024
A_KV_RANK = 512
A_WIDTH = A_HEADS * A_HEAD_DIM
IDX_HEADS = 64
IDX_DIM = 128
TOPK = 256
T5_BUCKETS = 32
T5_MAX_DIST = 128
R_HEADS = 8
R_QK_DIM = 128
R_V_DIM = 256
R_QK_WIDTH = R_HEADS * R_QK_DIM
R_WIDTH = R_HEADS * R_V_DIM
ROT_BASE = 10000.0
P_HEADS = 8
P_QUERY_DIM = 256
P_NKEYS = 128
P_TOPK = 16
P_EXPERTS = P_NKEYS * P_NKEYS

PROJ_WIDTH = 8192
OFF_CQ, OFF_CKV, OFF_KI, OFF_WI = 0, 1024, 1536, 1664
OFF_RQ, OFF_RK, OFF_RV, OFF_RG = 2048, 3072, 4096, 6144

Q_BLOCK = 128
NEAR_BACK = 128
NEAR = NEAR_BACK + Q_BLOCK
ATTN_SCALE = A_HEAD_DIM ** -0.5
VMEM_LIMIT = 56 * 1024 * 1024
INT_MIN = -2147483648


def _params(*sem, flags=None):
    return pltpu.CompilerParams(dimension_semantics=sem, vmem_limit_bytes=VMEM_LIMIT, flags=flags)


def _ada_kernel(c_ref, w_ref, b_ref, o_ref):
    c = c_ref[...]
    ca = (c * jax.nn.sigmoid(c)).astype(bf16)
    o_ref[...] = jnp.dot(ca, w_ref[...].astype(bf16), preferred_element_type=f32) + b_ref[...]


def _ada(c8, w, b):
    n = w.shape[1]
    tn = 512
    return pl.pallas_call(
        _ada_kernel,
        grid=(n // tn,),
        in_specs=[pl.BlockSpec((8, D_MODEL), lambda j: (0, 0)),
                  pl.BlockSpec((D_MODEL, tn), lambda j: (0, j)),
                  pl.BlockSpec((1, tn), lambda j: (0, j))],
        out_specs=pl.BlockSpec((8, tn), lambda j: (0, j)),
        out_shape=jax.ShapeDtypeStruct((8, n), f32),
        compiler_params=_params("arbitrary"),
        name="ada_mod",
    )(c8, w, b)


def _normmod_kernel(x_ref, g_ref, sh_ref, sc_ref, o_ref, *, transposed):
    x = x_ref[...]
    y = x * lax.rsqrt(jnp.mean(x * x, axis=-1, keepdims=True) + EPS) * g_ref[...]
    y = y * (1.0 + sc_ref[0]) + sh_ref[0]
    o_ref[...] = (y.T if transposed else y).astype(o_ref.dtype)


def _normmod(x2d, g, modr, shift_slot, scale_slot, transposed=False):
    ts = 256
    per_b = SEQ // ts
    if transposed:
        out_spec = pl.BlockSpec((D_MODEL, ts), lambda i: (0, i))
        out_shape = jax.ShapeDtypeStruct((D_MODEL, TOKENS), bf16)
    else:
        out_spec = pl.BlockSpec((ts, D_MODEL), lambda i: (i, 0))
        out_shape = jax.ShapeDtypeStruct((TOKENS, D_MODEL), bf16)
    return pl.pallas_call(
        functools.partial(_normmod_kernel, transposed=transposed),
        grid=(TOKENS // ts,),
        in_specs=[pl.BlockSpec((ts, D_MODEL), lambda i: (i, 0)),
                  pl.BlockSpec((1, D_MODEL), lambda i: (0, 0)),
                  pl.BlockSpec((1, 1, D_MODEL), lambda i: ((i // per_b) * N_MOD + shift_slot, 0, 0)),
                  pl.BlockSpec((1, 1, D_MODEL), lambda i: ((i // per_b) * N_MOD + scale_slot, 0, 0))],
        out_specs=out_spec,
        out_shape=out_shape,
        compiler_params=_params("arbitrary"),
        name="norm_modulate",
    )(x2d, g, modr, modr)


def _mm_kernel(a_ref, w_ref, *rest, epilogue):
    o_ref = rest[-1]
    acc = jnp.dot(a_ref[...], w_ref[...], preferred_element_type=f32)
    o_ref[...] = epilogue(acc, *rest[:-1]).astype(o_ref.dtype)


def _matmul(a, w, out_dtype, *, tm, tn, epilogue=None, extra=(), extra_specs=(), name):
    m, k = a.shape
    n = w.shape[1]
    if epilogue is None:
        epilogue = lambda acc: acc
    return pl.pallas_call(
        functools.partial(_mm_kernel, epilogue=epilogue),
        grid=(m // tm, n // tn),
        in_specs=[pl.BlockSpec((tm, k), lambda i, j: (i, 0)),
                  pl.BlockSpec((k, tn), lambda i, j: (0, j)),
                  *extra_specs],
        out_specs=pl.BlockSpec((tm, tn), lambda i, j: (i, j)),
        out_shape=jax.ShapeDtypeStruct((m, n), out_dtype),
        compiler_params=_params("arbitrary", "arbitrary"),
        name=name,
    )(a, w, *extra)


def _mm_w32_kernel(a_ref, w_ref, *rest, epilogue):
    o_ref, wb_ref = rest[-2], rest[-1]

    @pl.when(pl.program_id(1) == 0)
    def _():
        wb_ref[...] = w_ref[...].astype(bf16)

    acc = jnp.dot(a_ref[...], wb_ref[...], preferred_element_type=f32)
    o_ref[...] = epilogue(acc, *rest[:-2]).astype(o_ref.dtype)


def _matmul_w32(a, w, out_dtype, *, tm, tn, epilogue=None, extra=(), extra_specs=(), name):
    m, k = a.shape
    n = w.shape[1]
    if epilogue is None:
        epilogue = lambda acc: acc
    return pl.pallas_call(
        functools.partial(_mm_w32_kernel, epilogue=epilogue),
        grid=(n // tn, m // tm),
        in_specs=[pl.BlockSpec((tm, k), lambda j, i: (i, 0)),
                  pl.BlockSpec((k, tn), lambda j, i: (0, j)),
                  *extra_specs],
        out_specs=pl.BlockSpec((tm, tn), lambda j, i: (i, j)),
        out_shape=jax.ShapeDtypeStruct((m, n), out_dtype),
        scratch_shapes=[pltpu.VMEM((k, tn), bf16)],
        compiler_params=_params("arbitrary", "arbitrary"),
        name=name,
    )(a, w, *extra)


def _gate_epilogue(acc, b_ref):
    return jax.nn.sigmoid(acc + b_ref[...])


def _resid_epilogue(acc, x_ref, g_ref):
    return x_ref[...] + g_ref[0] * acc


def _rms(x, g):
    return x * lax.rsqrt(jnp.mean(x * x, axis=-1, keepdims=True) + EPS) * g


def _dsa_prep_kernel(p_ref, gcq_ref, gckv_ref, gki_ref, bki_ref, cq_o, ckv_o, ki_o, wi_o):
    cq_o[...] = _rms(p_ref[:, OFF_CQ:OFF_CQ + A_Q_RANK], gcq_ref[...]).astype(bf16)
    ckv_o[...] = _rms(p_ref[:, OFF_CKV:OFF_CKV + A_KV_RANK], gckv_ref[...]).astype(bf16)
    ki = p_ref[:, OFF_KI:OFF_KI + IDX_DIM]
    mu = jnp.mean(ki, axis=-1, keepdims=True)
    var = jnp.mean(jnp.square(ki - mu), axis=-1, keepdims=True)
    ki_o[...] = ((ki - mu) * lax.rsqrt(var + EPS) * gki_ref[...] + bki_ref[...]).astype(bf16)
    wi_o[...] = p_ref[:, OFF_WI:OFF_WI + 128] * (IDX_HEADS ** -0.5 * IDX_DIM ** -0.5)


def _dsa_prep(proj, g_cq, g_ckv, g_ki, b_ki):
    ts = 512
    row = lambda i: (i, 0)
    fixed = lambda i: (0, 0)
    return pl.pallas_call(
        _dsa_prep_kernel,
        grid=(TOKENS // ts,),
        in_specs=[pl.BlockSpec((ts, 2048), row),
                  pl.BlockSpec((1, A_Q_RANK), fixed),
                  pl.BlockSpec((1, A_KV_RANK), fixed),
                  pl.BlockSpec((1, IDX_DIM), fixed),
                  pl.BlockSpec((1, IDX_DIM), fixed)],
        out_specs=[pl.BlockSpec((ts, A_Q_RANK), row),
                   pl.BlockSpec((ts, A_KV_RANK), row),
                   pl.BlockSpec((ts, IDX_DIM), row),
                   pl.BlockSpec((ts, 128), row)],
        out_shape=[jax.ShapeDtypeStruct((TOKENS, A_Q_RANK), bf16),
                   jax.ShapeDtypeStruct((TOKENS, A_KV_RANK), bf16),
                   jax.ShapeDtypeStruct((TOKENS, IDX_DIM), bf16),
                   jax.ShapeDtypeStruct((TOKENS, 128), f32)],
        compiler_params=_params("arbitrary"),
        name="dsa_prep",
    )(proj, g_cq, g_ckv, g_ki, b_ki)


def _qlat_kernel(q_ref, w_ref, o_ref):
    for h in range(A_HEADS):
        r = jnp.dot(q_ref[:, h * A_HEAD_DIM:(h + 1) * A_HEAD_DIM], w_ref[h], preferred_element_type=f32)
        o_ref[:, h] = r.astype(o_ref.dtype).reshape(o_ref.shape[0], Q_BLOCK, A_KV_RANK)


def _qlat(qq, wuk_t):
    tm = 512
    q_blk = IDX_HEADS * IDX_DIM // A_WIDTH
    return pl.pallas_call(
        _qlat_kernel,
        grid=(TOKENS // tm,),
        in_specs=[pl.BlockSpec((tm, A_WIDTH), lambda i: (i, q_blk)),
                  pl.BlockSpec((A_HEADS, A_HEAD_DIM, A_KV_RANK), lambda i: (0, 0, 0))],
        out_specs=pl.BlockSpec((tm // Q_BLOCK, A_HEADS, Q_BLOCK, A_KV_RANK), lambda i: (i, 0, 0, 0)),
        out_shape=jax.ShapeDtypeStruct((TOKENS // Q_BLOCK, A_HEADS, Q_BLOCK, A_KV_RANK), bf16),
        compiler_params=_params("arbitrary"),
        name="q_lat",
    )(qq, wuk_t)


def _t5_kernel(t5_ref, o_ref):
    h = pl.program_id(0)
    half = T5_BUCKETS // 2
    exact = half // 2
    qi = lax.broadcasted_iota(i32, (Q_BLOCK, NEAR), 0)
    kj = lax.broadcasted_iota(i32, (Q_BLOCK, NEAR), 1)
    rel = kj - NEAR_BACK - qi
    n = jnp.abs(rel)
    log_ratio = jnp.log(jnp.maximum(n, 1).astype(f32) / exact) / math.log(T5_MAX_DIST / exact)
    large = jnp.minimum(exact + (log_ratio * (half - exact)).astype(i32), half - 1)
    bucket = jnp.where(rel > 0, half, 0) + jnp.where(n < exact, n, large)
    acc = jnp.zeros((Q_BLOCK, NEAR), f32)
    for k in range(T5_BUCKETS):
        acc = jnp.where(bucket == k, t5_ref[k, h], acc)
    o_ref[0] = (acc - t5_ref[half - 1, h]) * (1.0 / ATTN_SCALE)


def _t5_table(t5_bias):
    return pl.pallas_call(
        _t5_kernel,
        grid=(A_HEADS,),
        in_specs=[pl.BlockSpec(memory_space=pltpu.SMEM)],
        out_specs=pl.BlockSpec((1, Q_BLOCK, NEAR), lambda h: (h, 0, 0)),
        out_shape=jax.ShapeDtypeStruct((A_HEADS, Q_BLOCK, NEAR), f32),
        compiler_params=_params("arbitrary"),
        name="t5_table",
    )(t5_bias)


DSA_COL_STEP = 512
DSA_HEAD_GROUP = 8
DSA_ROWS = DSA_HEAD_GROUP * Q_BLOCK


def _dsa_block(width, qb, t5_ref, qi_ref, kit_ref, wi_ref, ql_ref, kv_ref, wuv_ref, pt_ref, o_ref,
               key_ref, mask_ref, lg_ref, e_ref, l_ref):
    q0 = qb * Q_BLOCK
    ct_w = 256
    w = wi_ref[...]
    score_ref = lg_ref.at[0:Q_BLOCK]

    def col_tile(ct, carry):
        c0 = pl.multiple_of(ct * ct_w, ct_w)
        kt = kit_ref[:, pl.ds(c0, ct_w)]
        acc = jnp.zeros((Q_BLOCK, ct_w), f32)
        for h in range(IDX_HEADS):
            x = jnp.dot(qi_ref[:, h * IDX_DIM:(h + 1) * IDX_DIM], kt, preferred_element_type=f32)
            acc = acc + jnp.maximum(x, 0.0) * w[:, h:h + 1]
        score_ref[:, pl.ds(c0, ct_w)] = acc
        return carry

    lax.fori_loop(0, width // ct_w, col_tile, 0)

    row = lax.broadcasted_iota(i32, (Q_BLOCK, width), 0)
    col = lax.broadcasted_iota(i32, (Q_BLOCK, width), 1)
    adm = (col // CHUNK) <= ((q0 + row) // CHUNK)
    score = jnp.where(adm, score_ref[:, 0:width], -jnp.inf)
    bits = pltpu.bitcast(score, i32)
    key_ref[:, 0:width] = jnp.where(bits < 0, bits ^ 0x7FFFFFFF, bits)

    def bisect(b, thr_u):
        cand_u = thr_u | jnp.left_shift(jnp.int32(1), 31 - b)
        hit = jnp.where(key_ref[:, 0:width] >= (cand_u ^ INT_MIN), 1.0, 0.0)
        cnt = jnp.sum(hit, axis=1, keepdims=True)
        return jnp.where(cnt >= TOPK, cand_u, thr_u)

    thr_u = lax.fori_loop(0, 32, bisect, jnp.zeros((Q_BLOCK, 1), i32))
    sel = (key_ref[:, 0:width] >= (thr_u ^ INT_MIN)) & adm
    mask_ref[:, 0:width] = jnp.where(sel, 0.0, -jnp.inf)

    far_bucket = T5_BUCKETS // 2 - 1

    kv = kv_ref[0:width, :]

    def head_group(g, carry):
        h0 = g * DSA_HEAD_GROUP
        qg = ql_ref[0, pl.ds(h0, DSA_HEAD_GROUP)].reshape(DSA_ROWS, A_KV_RANK)
        lg_ref[:, 0:width] = lax.dot_general(qg, kv, (((1,), (1,)), ((), ())), preferred_element_type=f32)

        @pl.when(qb == 0)
        def _():
            for hh in range(DSA_HEAD_GROUP):
                lg_ref[hh * Q_BLOCK:(hh + 1) * Q_BLOCK, 0:Q_BLOCK] += pt_ref[h0 + hh, :, NEAR_BACK:NEAR]

        @pl.when(qb > 0)
        def _():
            w0 = pl.multiple_of(q0 - NEAR_BACK, 128)
            for hh in range(DSA_HEAD_GROUP):
                lg_ref[hh * Q_BLOCK:(hh + 1) * Q_BLOCK, pl.ds(w0, NEAR)] += pt_ref[h0 + hh]

        for hh in range(DSA_HEAD_GROUP):
            rs = slice(hh * Q_BLOCK, (hh + 1) * Q_BLOCK)
            x = lg_ref[rs, 0:width] * ATTN_SCALE + (mask_ref[:, 0:width] + t5_ref[far_bucket, h0 + hh])
            m = jnp.max(x, axis=1, keepdims=True)
            e = jnp.exp(x - m)
            l_ref[rs, :] = jnp.sum(e, axis=1, keepdims=True)
            e_ref[rs, 0:width] = e.astype(bf16)

        ol = jnp.dot(e_ref[:, 0:width], kv, preferred_element_type=f32) / l_ref[...]
        for hh in range(DSA_HEAD_GROUP):
            o = jnp.dot(ol[hh * Q_BLOCK:(hh + 1) * Q_BLOCK].astype(bf16), wuv_ref[h0 + hh],
                        preferred_element_type=f32)
            c0 = pl.multiple_of((h0 + hh) * A_HEAD_DIM, A_HEAD_DIM)
            o_ref[:, pl.ds(c0, A_HEAD_DIM)] = o.astype(o_ref.dtype)
        return carry

    lax.fori_loop(0, A_HEADS // DSA_HEAD_GROUP, head_group, 0)


def _dsa_kernel(t5_ref, qi_ref, kit_ref, wi_ref, ql_ref, kv_ref, wuv_ref, pt_ref, o_ref,
                key_ref, mask_ref, lg_ref, e_ref, l_ref):
    qb = pl.program_id(1)
    blocks_per_step = DSA_COL_STEP // Q_BLOCK
    for n in range(SEQ // DSA_COL_STEP):
        pl.when(qb // blocks_per_step == n)(functools.partial(
            _dsa_block, (n + 1) * DSA_COL_STEP, qb, t5_ref, qi_ref, kit_ref, wi_ref, ql_ref, kv_ref,
            wuv_ref, pt_ref, o_ref, key_ref, mask_ref, lg_ref, e_ref, l_ref))


def _dsa(t5_bias, qq, kidx_t, widx, qlat, ckv, wuv, ptab):
    nqb = SEQ // Q_BLOCK
    return pl.pallas_call(
        _dsa_kernel,
        grid=(BATCH, nqb),
        in_specs=[pl.BlockSpec(memory_space=pltpu.SMEM),
                  pl.BlockSpec((Q_BLOCK, IDX_HEADS * IDX_DIM), lambda b, i: (b * nqb + i, 0)),
                  pl.BlockSpec((IDX_DIM, SEQ), lambda b, i: (0, b)),
                  pl.BlockSpec((Q_BLOCK, 128), lambda b, i: (b * nqb + i, 0)),
                  pl.BlockSpec((1, A_HEADS, Q_BLOCK, A_KV_RANK), lambda b, i: (b * nqb + i, 0, 0, 0)),
                  pl.BlockSpec((SEQ, A_KV_RANK), lambda b, i: (b, 0)),
                  pl.BlockSpec((A_HEADS, A_KV_RANK, A_HEAD_DIM), lambda b, i: (0, 0, 0)),
                  pl.BlockSpec((A_HEADS, Q_BLOCK, NEAR), lambda b, i: (0, 0, 0))],
        out_specs=pl.BlockSpec((Q_BLOCK, A_WIDTH), lambda b, i: (b * nqb + i, 0)),
        out_shape=jax.ShapeDtypeStruct((TOKENS, A_WIDTH), bf16),
        scratch_shapes=[pltpu.VMEM((Q_BLOCK, SEQ), i32),
                        pltpu.VMEM((Q_BLOCK, SEQ), f32),
                        pltpu.VMEM((DSA_ROWS, SEQ), f32),
                        pltpu.VMEM((DSA_ROWS, SEQ), bf16),
                        pltpu.VMEM((DSA_ROWS, 1), f32)],
        compiler_params=_params("arbitrary", "arbitrary"),
        name="dsa_main",
    )(t5_bias, qq, kidx_t, widx, qlat, ckv, wuv, ptab)


def _ret_kernel(q_ref, k_ref, v_ref, g_ref, cc_ref, ss_ref, dm_ref, qd_ref, kd_ref, cd_ref, gr_ref,
                o_ref, qs_ref, ks_ref):
    cc = cc_ref[...]
    ss = ss_ref[...]
    q = q_ref[...]
    k = k_ref[...] * (R_QK_DIM ** -0.5)
    half = R_QK_DIM // 2
    qs_ref[...] = q * cc + pltpu.roll(q, half, 1) * ss
    ks_ref[...] = k * cc + pltpu.roll(k, half, 1) * ss
    dm = dm_ref[0]
    qd = qd_ref[0]
    kd = kd_ref[0]
    cd = cd_ref[0]
    gr = gr_ref[0]

    def chunk(n, state):
        r0 = pl.multiple_of(n * CHUNK, CHUNK)
        qn = qs_ref[pl.ds(r0, CHUNK), :]
        kn = ks_ref[pl.ds(r0, CHUNK), :]
        vn = v_ref[pl.ds(r0, CHUNK), :].astype(bf16)
        att = lax.dot_general(qn.astype(bf16), kn.astype(bf16), (((1,), (1,)), ((), ())),
                              preferred_element_type=f32) * dm
        y = jnp.dot(att.astype(bf16), vn, preferred_element_type=f32)
        y = y + jnp.dot((qn * qd).astype(bf16), state.astype(bf16), preferred_element_type=f32)
        kv = lax.dot_general((kn * kd).astype(bf16), vn, (((0,), (0,)), ((), ())),
                             preferred_element_type=f32)
        y = y * lax.rsqrt(jnp.mean(y * y, axis=-1, keepdims=True) + EPS) * gr
        gate = g_ref[pl.ds(r0, CHUNK), :]
        o_ref[pl.ds(r0, CHUNK), :] = (gate * jax.nn.sigmoid(gate) * y).astype(o_ref.dtype)
        return state * cd + kv

    lax.fori_loop(0, SEQ // CHUNK, chunk, jnp.zeros((R_QK_DIM, R_V_DIM), f32), unroll=8)


def _retention(proj, cc, ss, dmask, qdec, kdec, cdec, g_ret):
    qk_blk = OFF_RQ // R_QK_DIM
    k_blk = OFF_RK // R_QK_DIM
    v_blk = OFF_RV // R_V_DIM
    g_blk = OFF_RG // R_V_DIM
    per_head = lambda b, h: (h, 0, 0)
    return pl.pallas_call(
        _ret_kernel,
        grid=(BATCH, R_HEADS),
        in_specs=[pl.BlockSpec((SEQ, R_QK_DIM), lambda b, h: (b, qk_blk + h)),
                  pl.BlockSpec((SEQ, R_QK_DIM), lambda b, h: (b, k_blk + h)),
                  pl.BlockSpec((SEQ, R_V_DIM), lambda b, h: (b, v_blk + h)),
                  pl.BlockSpec((SEQ, R_V_DIM), lambda b, h: (b, g_blk + h)),
                  pl.BlockSpec((SEQ, R_QK_DIM), lambda b, h: (0, 0)),
                  pl.BlockSpec((SEQ, R_QK_DIM), lambda b, h: (0, 0)),
                  pl.BlockSpec((1, CHUNK, CHUNK), per_head),
                  pl.BlockSpec((1, CHUNK, R_QK_DIM), per_head),
                  pl.BlockSpec((1, CHUNK, R_QK_DIM), per_head),
                  pl.BlockSpec((1, 1, R_V_DIM), per_head),
                  pl.BlockSpec((1, 1, R_V_DIM), per_head)],
        out_specs=pl.BlockSpec((SEQ, R_V_DIM), lambda b, h: (b, h)),
        out_shape=jax.ShapeDtypeStruct((TOKENS, R_WIDTH), bf16),
        scratch_shapes=[pltpu.VMEM((SEQ, R_QK_DIM), f32), pltpu.VMEM((SEQ, R_QK_DIM), f32)],
        compiler_params=_params("arbitrary", "arbitrary"),
        name="retention",
    )(proj, proj, proj, proj, cc, ss, dmask, qdec, kdec, cdec, g_ret)


def _merge_kernel(ya_ref, yr_ref, wa_ref, wr_ref, ga_ref, gr_ref, o_ref, wab_ref, wrb_ref):
    @pl.when(pl.program_id(1) == 0)
    def _():
        wab_ref[...] = wa_ref[...].astype(bf16)
        wrb_ref[...] = wr_ref[...].astype(bf16)

    pa = jnp.dot(ya_ref[...], wab_ref[...], preferred_element_type=f32)
    pr = jnp.dot(yr_ref[...], wrb_ref[...], preferred_element_type=f32)
    o_ref[...] = (ga_ref[...].astype(f32) * pa + gr_ref[...].astype(f32) * pr).astype(o_ref.dtype)


def _merge(ya, yr, w_up, gates):
    assert A_WIDTH == R_WIDTH
    tm, tn = 1024, 512
    nj = D_MODEL // tn
    return pl.pallas_call(
        _merge_kernel,
        grid=(nj, TOKENS // tm),
        in_specs=[pl.BlockSpec((tm, A_WIDTH), lambda j, i: (i, 0)),
                  pl.BlockSpec((tm, R_WIDTH), lambda j, i: (i, 0)),
                  pl.BlockSpec((A_WIDTH, tn), lambda j, i: (0, j)),
                  pl.BlockSpec((R_WIDTH, tn), lambda j, i: (1, j)),
                  pl.BlockSpec((tm, tn), lambda j, i: (i, j)),
                  pl.BlockSpec((tm, tn), lambda j, i: (i, nj + j))],
        out_specs=pl.BlockSpec((tm, tn), lambda j, i: (i, j)),
        out_shape=jax.ShapeDtypeStruct((TOKENS, D_MODEL), bf16),
        scratch_shapes=[pltpu.VMEM((A_WIDTH, tn), bf16), pltpu.VMEM((R_WIDTH, tn), bf16)],
        compiler_params=_params("arbitrary", "arbitrary"),
        name="merge_up",
    )(ya, yr, w_up, w_up, gates, gates)


ROUTE_CAND_GROUPS = ((0, 16), (1, 8), (2, 8), (3, 8), (4, 8), (5, 8), (6, 8), (7, 8))
ROUTE_CAND_ROWS = sum(n for _, n in ROUTE_CAND_GROUPS) + 8


def _peer_route_kernel(q_ref, k_ref, s1_o, s2_o, a1_o, m2_o, thr_o, v1_ref, v2_ref, cand_ref, ec_ref):
    hq = P_QUERY_DIM // 2
    s1_all = jnp.dot(k_ref[0], q_ref[0:hq, :].astype(bf16), preferred_element_type=f32)
    s2_all = jnp.dot(k_ref[1], q_ref[hq:P_QUERY_DIM, :].astype(bf16), preferred_element_type=f32)

    def top_values(s, v_ref):
        cur = s
        for r in range(P_TOPK):
            m = jnp.max(cur, axis=0, keepdims=True)
            v_ref[r:r + 1, :] = m
            cur = jnp.where(cur == m, -jnp.inf, cur)

    for lt in range(s1_all.shape[1] // 128):
        ls = slice(lt * 128, (lt + 1) * 128)
        s1 = s1_all[:, ls]
        s2 = s2_all[:, ls]
        top_values(s1, v1_ref)
        top_values(s2, v2_ref)
        v1 = v1_ref[...]
        v2 = v2_ref[...]
        m1 = v1[0:1]
        m2 = v2[0:1]
        e1 = jnp.exp(v1 - m1)
        e2 = jnp.exp(v2 - m2)
        off = 0
        for r1, n in ROUTE_CAND_GROUPS:
            cand_ref[off:off + n, :] = v1[r1:r1 + 1] + v2[0:n]
            ec_ref[off:off + n, :] = e1[r1:r1 + 1] * e2[0:n]
            off += n
        cand_ref[off:off + 8, :] = v1[8:16] + v2[0:1]
        ec_ref[off:off + 8, :] = e1[8:16] * e2[0:1]
        cand = cand_ref[...]
        cur = cand
        thr = None
        for r in range(P_TOPK):
            thr = jnp.max(cur, axis=0, keepdims=True)
            cur = jnp.where(cur == thr, -jnp.inf, cur)
        z = jnp.sum(jnp.where(cand >= thr, ec_ref[...], 0.0), axis=0, keepdims=True)
        a1 = jnp.exp(s1 - m1) / z
        for r in range(PEER_NC):
            s1_o[0, r, :, ls] = s1[r * PEER_ROWS:(r + 1) * PEER_ROWS]
            a1_o[0, r, :, ls] = a1[r * PEER_ROWS:(r + 1) * PEER_ROWS]
        s2_o[0, lt] = s2
        m2_o[0, :, ls] = m2
        thr_o[0, :, ls] = thr


def _peer_route(q_t, keys):
    tl = 512
    big = lambda h, j: (h, 0, j)
    chunked = lambda h, j: (h, 0, 0, j)
    chunked_shape = jax.ShapeDtypeStruct((P_HEADS, PEER_NC, PEER_ROWS, TOKENS), f32)
    row_shape = jax.ShapeDtypeStruct((P_HEADS, 1, TOKENS), f32)
    return pl.pallas_call(
        _peer_route_kernel,
        grid=(P_HEADS, TOKENS // tl),
        in_specs=[pl.BlockSpec((P_QUERY_DIM, tl), lambda h, j: (h, j)),
                  pl.BlockSpec((2, P_NKEYS, P_QUERY_DIM // 2), lambda h, j: (0, 0, 0))],
        out_specs=[pl.BlockSpec((1, PEER_NC, PEER_ROWS, tl), chunked),
                   pl.BlockSpec((1, tl // 128, P_NKEYS, 128), lambda h, j: (h, j, 0, 0)),
                   pl.BlockSpec((1, PEER_NC, PEER_ROWS, tl), chunked),
                   pl.BlockSpec((1, 1, tl), big),
                   pl.BlockSpec((1, 1, tl), big)],
        out_shape=[chunked_shape, jax.ShapeDtypeStruct((P_HEADS, TOKENS // 128, P_NKEYS, 128), f32),
                   chunked_shape, row_shape, row_shape],
        scratch_shapes=[pltpu.VMEM((P_TOPK, 128), f32), pltpu.VMEM((P_TOPK, 128), f32),
                        pltpu.VMEM((ROUTE_CAND_ROWS, 128), f32), pltpu.VMEM((ROUTE_CAND_ROWS, 128), f32)],
        compiler_params=_params("arbitrary", "arbitrary"),
        name="peer_route",
    )(q_t, keys)


PEER_TM = 512
PEER_TE = 512
PEER_NC = P_EXPERTS // PEER_TE
PEER_ROWS = PEER_TE // P_NKEYS
PEER_KT = 64
PEER_KPIECES = 8


def _peer_kernel(ht_ref, u_ref, vt_ref, s1_ref, a1_ref, s2_ref, m2_ref, thr_ref, o_ref,
                 b2_ref, pre_ref, w_ref):
    c = pl.program_id(1)

    @pl.when(c == 0)
    def _():
        o_ref[...] = jnp.zeros_like(o_ref)
        for lt in range(PEER_TM // 128):
            b2_ref[:, lt] = jnp.exp(s2_ref[:, lt] - m2_ref[:, :, lt * 128:(lt + 1) * 128])

    @pl.when(c > 0)
    def _():
        pre = pre_ref[...]
        act = 0.5 * pre * (1.0 + lax.erf(pre * np.float32(np.sqrt(0.5))))
        coef = (w_ref[0:PEER_TE, :] * act).astype(bf16)
        o_ref[...] += jnp.dot(vt_ref[...], coef, preferred_element_type=f32)

    @pl.when(c < PEER_NC)
    def _():
        z = pl.multiple_of(jnp.minimum(c, 0), PEER_TE)
        units = [(j, lt, kt) for j in range(PEER_ROWS) for lt in range(PEER_TM // 128)
                 for kt in range(P_NKEYS // PEER_KT)]
        per_piece = len(units) // PEER_KPIECES
        kw = D_MODEL // PEER_KPIECES
        for kq in range(PEER_KPIECES):
            part = jnp.dot(u_ref[:, kq * kw:(kq + 1) * kw], ht_ref[kq * kw:(kq + 1) * kw, :],
                           preferred_element_type=f32)
            if kq == 0:
                pre_ref[...] = part
            else:
                pre_ref[...] += part
            w_ref[pl.ds(z + PEER_TE, 8), 0:128] = part[PEER_TE - 8:PEER_TE, PEER_TM - 128:PEER_TM]
            for j, lt, kt in units[kq * per_piece:(kq + 1) * per_piece]:
                ls = slice(lt * 128, (lt + 1) * 128)
                ks = slice(kt * PEER_KT, (kt + 1) * PEER_KT)
                wj = jnp.zeros((PEER_KT, 128), f32)
                for h in range(P_HEADS):
                    sel = (s1_ref[h, 0, j:j + 1, ls] + s2_ref[h, lt, ks, :]) >= thr_ref[h, :, ls]
                    wj = wj + jnp.where(sel, b2_ref[h, lt, ks, :], 0.0) * a1_ref[h, 0, j:j + 1, ls]
                w0 = pl.multiple_of(z + (j * P_NKEYS + kt * PEER_KT), PEER_KT)
                w_ref[pl.ds(w0, PEER_KT), ls] = wj


def _peer(h_t, u, v_t, s1, a1, s2, m2, thr):
    tm, te, nc = PEER_TM, PEER_TE, PEER_NC
    s1r, a1r = s1, a1
    tok3 = lambda i, c: (0, 0, i)
    cur = lambda i, c: (0, jnp.minimum(c, nc - 1), 0, i)
    return pl.pallas_call(
        _peer_kernel,
        grid=(TOKENS // tm, nc + 1),
        in_specs=[pl.BlockSpec((D_MODEL, tm), lambda i, c: (0, i)),
                  pl.BlockSpec((te, D_MODEL), lambda i, c: (jnp.minimum(c, nc - 1), 0)),
                  pl.BlockSpec((D_MODEL, te), lambda i, c: (0, jnp.maximum(c - 1, 0))),
                  pl.BlockSpec((P_HEADS, 1, PEER_ROWS, tm), cur),
                  pl.BlockSpec((P_HEADS, 1, PEER_ROWS, tm), cur),
                  pl.BlockSpec((P_HEADS, tm // 128, P_NKEYS, 128), lambda i, c: (0, i, 0, 0)),
                  pl.BlockSpec((P_HEADS, 1, tm), tok3),
                  pl.BlockSpec((P_HEADS, 1, tm), tok3)],
        out_specs=pl.BlockSpec((D_MODEL, tm), lambda i, c: (0, i)),
        out_shape=jax.ShapeDtypeStruct((D_MODEL, TOKENS), f32),
        scratch_shapes=[pltpu.VMEM((P_HEADS, tm // 128, P_NKEYS, 128), f32),
                        pltpu.VMEM((te, tm), f32), pltpu.VMEM((te + 8, tm), f32)],
        compiler_params=_params("arbitrary", "arbitrary"),
        name="peer_experts",
    )(h_t, u, v_t, s1r, a1r, s2, m2, thr)


def _final_kernel(x_ref, yt_ref, gt_ref, g_ref, o_ref):
    x = x_ref[...] + gt_ref[0] * yt_ref[...].T
    o_ref[...] = x * lax.rsqrt(jnp.mean(x * x, axis=-1, keepdims=True) + EPS) * g_ref[...]


def _final(x1, y_t, modr, g_final):
    ts = 256
    per_b = SEQ // ts
    return pl.pallas_call(
        _final_kernel,
        grid=(TOKENS // ts,),
        in_specs=[pl.BlockSpec((ts, D_MODEL), lambda i: (i, 0)),
                  pl.BlockSpec((D_MODEL, ts), lambda i: (0, i)),
                  pl.BlockSpec((1, 1, D_MODEL), lambda i: ((i // per_b) * N_MOD + 5, 0, 0)),
                  pl.BlockSpec((1, D_MODEL), lambda i: (0, 0))],
        out_specs=pl.BlockSpec((ts, D_MODEL), lambda i: (i, 0)),
        out_shape=jax.ShapeDtypeStruct((TOKENS, D_MODEL), f32),
        compiler_params=_params("arbitrary"),
        name="final_norm",
    )(x1, y_t, modr, g_final)


def _retention_tables():
    half = R_QK_DIM // 2
    inv = 1.0 / (ROT_BASE ** jnp.linspace(0.0, 1.0, half, dtype=f32))
    ang = jnp.arange(SEQ, dtype=f32)[:, None] * inv[None, :]
    cos, sin = jnp.cos(ang), jnp.sin(ang)
    cc = jnp.concatenate([cos, cos], axis=-1)
    ss = jnp.concatenate([-sin, sin], axis=-1)
    log_g = jnp.log(1.0 - jnp.power(2.0, -5.0 - jnp.arange(R_HEADS, dtype=f32)))
    j = jnp.arange(CHUNK, dtype=f32)
    diff = j[:, None] - j[None, :]
    dmask = jnp.where(diff[None] >= 0, jnp.exp(jnp.maximum(diff, 0.0)[None] * log_g[:, None, None]), 0.0)
    kdec = jnp.exp((CHUNK - 1.0 - j)[None, :] * log_g[:, None])
    qdec = jnp.exp((j + 1.0)[None, :] * log_g[:, None])
    cdec = jnp.exp(CHUNK * log_g)
    kdec = jnp.broadcast_to(kdec[:, :, None], (R_HEADS, CHUNK, R_QK_DIM))
    qdec = jnp.broadcast_to(qdec[:, :, None], (R_HEADS, CHUNK, R_QK_DIM))
    cdec = jnp.broadcast_to(cdec[:, None, None], (R_HEADS, 1, R_V_DIM))
    return cc, ss, dmask, qdec, kdec, cdec


def kernel(x, c, w_ada, b_ada, g_mix, w_in, g_cq, g_ckv, w_uq, w_uk, w_uv, w_qi, g_ki, b_ki, t5_bias, g_ret,
           w_up, w_gate, b_gate, w_out, g_ffn, w_pq, sub_keys, u_exp, v_exp, g_final):
    x2d = x.reshape(TOKENS, D_MODEL)

    w_in_l = w_in[0]
    zeros = lambda n: jnp.zeros((D_MODEL, n), w_in_l.dtype)
    w_in_p = jnp.concatenate(
        [w_in_l[:, :1728], zeros(OFF_RQ - 1728), w_in_l[:, 1728:]], axis=1).astype(bf16)
    w_q_all = jnp.concatenate(
        [w_qi[0].reshape(A_Q_RANK, IDX_HEADS * IDX_DIM), w_uq[0].reshape(A_Q_RANK, A_WIDTH)], axis=1).astype(bf16)
    wuk_t = jnp.transpose(w_uk[0], (1, 2, 0)).astype(bf16)
    wuv_h = jnp.transpose(w_uv[0], (1, 0, 2)).astype(bf16)
    w_pq_t = w_pq[0].T.astype(bf16)
    keys_b = sub_keys[0].astype(bf16)
    u_b = u_exp[0].astype(bf16)
    v_t = v_exp[0].T.astype(bf16)

    c8 = jnp.pad(c, ((0, 8 - BATCH), (0, 0)))
    mod = _ada(c8, w_ada[0], b_ada[0].reshape(1, N_MOD * D_MODEL))[:BATCH]
    modr = mod.reshape(BATCH * N_MOD, 1, D_MODEL)

    h = _normmod(x2d, g_mix[0].reshape(1, D_MODEL), modr, 0, 1)
    proj = _matmul(h, w_in_p, f32, tm=1024, tn=512, name="in_proj")
    gates = _matmul_w32(h, w_gate[0], bf16, tm=1024, tn=512, epilogue=_gate_epilogue,
                        extra=(b_gate[0].reshape(1, 2 * D_MODEL),),
                        extra_specs=(pl.BlockSpec((1, 512), lambda j, i: (0, j)),), name="gates")

    cqn, ckvn, kidx, widx = _dsa_prep(proj, g_cq[0].reshape(1, -1), g_ckv[0].reshape(1, -1),
                                      g_ki[0].reshape(1, -1), b_ki[0].reshape(1, -1))
    qq = _matmul(cqn, w_q_all, bf16, tm=1024, tn=1024, name="q_up")
    qlat = _qlat(qq, wuk_t)
    ptab = _t5_table(t5_bias)
    y_a = _dsa(t5_bias, qq, kidx.T, widx, qlat, ckvn, wuv_h, ptab)

    cc, ss, dmask, qdec, kdec, cdec = _retention_tables()
    y_r = _retention(proj, cc, ss, dmask, qdec, kdec, cdec, g_ret[0].reshape(R_HEADS, 1, R_V_DIM))

    merged = _merge(y_a, y_r, w_up[0], gates)
    per_b = SEQ // 1024
    x1 = _matmul_w32(merged, w_out[0], f32, tm=1024, tn=512, epilogue=_resid_epilogue,
                     extra=(x2d, modr),
                     extra_specs=(pl.BlockSpec((1024, 512), lambda j, i: (i, j)),
                                  pl.BlockSpec((1, 1, 512), lambda j, i: ((i // per_b) * N_MOD + 2, 0, j))),
                     name="out_proj")

    h2_t = _normmod(x1, g_ffn[0].reshape(1, D_MODEL), modr, 3, 4, transposed=True)
    q_t = _matmul(w_pq_t, h2_t, f32, tm=1024, tn=512, name="peer_q")
    s1, s2, a1, m2, thr = _peer_route(q_t, keys_b)
    y_t = _peer(h2_t, u_b, v_t, s1, a1, s2, m2, thr)
    out = _final(x1, y_t, modr, g_final.reshape(1, D_MODEL))
    return out.reshape(BATCH, SEQ, D_MODEL)
```

```python
import functools
import math

import numpy as np
import jax
import jax.numpy as jnp
from jax import lax
from jax.experimental import pallas as pl
from jax.experimental.pallas import tpu as pltpu

f32 = jnp.float32
bf16 = jnp.bfloat16
i32 = jnp.int32

D_MODEL = 4096
BATCH = 4
SEQ = 2048
TOKENS = BATCH * SEQ
CHUNK = 64
EPS = 1e-6
N_MOD = 6
A_HEADS = 16
A_HEAD_DIM = 128
A_Q_RANK = 1024
A_KV_RANK = 512
A_WIDTH = A_HEADS * A_HEAD_DIM
IDX_HEADS = 64
IDX_DIM = 128
TOPK = 256
T5_BUCKETS = 32
T5_MAX_DIST = 128
R_HEADS = 8
R_QK_DIM = 128
R_V_DIM = 256
R_QK_WIDTH = R_HEADS * R_QK_DIM
R_WIDTH = R_HEADS * R_V_DIM
ROT_BASE = 10000.0
P_HEADS = 8
P_QUERY_DIM = 256
P_NKEYS = 128
P_TOPK = 16
P_EXPERTS = P_NKEYS * P_NKEYS

PROJ_WIDTH = 8192
OFF_CQ, OFF_CKV, OFF_KI, OFF_WI = 0, 1024, 1536, 1664
OFF_RQ, OFF_RK, OFF_RV, OFF_RG = 2048, 3072, 4096, 6144

Q_BLOCK = 128
NEAR_BACK = 128
NEAR = NEAR_BACK + Q_BLOCK
ATTN_SCALE = A_HEAD_DIM ** -0.5
VMEM_LIMIT = 56 * 1024 * 1024
INT_MIN = -2147483648


def _params(*sem, flags=None):
    return pltpu.CompilerParams(dimension_semantics=sem, vmem_limit_bytes=VMEM_LIMIT, flags=flags)


def _cast_kernel(x_ref, o_ref, *, transposed):
    x = x_ref[...]
    o_ref[...] = (x.T if transposed else x).astype(o_ref.dtype)


def _cast_bf16(x, transposed=False):
    r, c = x.shape
    tr = 512
    if transposed:
        out_spec = pl.BlockSpec((c, tr), lambda i: (0, i))
        out_shape = jax.ShapeDtypeStruct((c, r), bf16)
    else:
        out_spec = pl.BlockSpec((tr, c), lambda i: (i, 0))
        out_shape = jax.ShapeDtypeStruct((r, c), bf16)
    return pl.pallas_call(
        functools.partial(_cast_kernel, transposed=transposed),
        grid=(r // tr,),
        in_specs=[pl.BlockSpec((tr, c), lambda i: (i, 0))],
        out_specs=out_spec,
        out_shape=out_shape,
        compiler_params=_params("arbitrary"),
        name="cast_bf16_t" if transposed else "cast_bf16",
    )(x)


def _w_in_kernel(x_ref, o_ref):
    n_a = OFF_WI + IDX_HEADS
    o_ref[:, 0:OFF_WI] = x_ref[:, 0:OFF_WI].astype(bf16)
    tail = jnp.concatenate([x_ref[:, OFF_WI:n_a], jnp.zeros((x_ref.shape[0], OFF_WI + 128 - n_a), f32)], axis=1)
    o_ref[:, OFF_WI:OFF_WI + 128] = tail.astype(bf16)
    o_ref[:, OFF_WI + 128:OFF_RQ] = jnp.zeros((x_ref.shape[0], OFF_RQ - OFF_WI - 128), bf16)
    o_ref[:, OFF_RQ:PROJ_WIDTH] = x_ref[:, n_a:n_a + PROJ_WIDTH - OFF_RQ].astype(bf16)


def _w_in_layout(w_in_l):
    r, c = w_in_l.shape
    tr = 256
    return pl.pallas_call(
        _w_in_kernel,
        grid=(r // tr,),
        in_specs=[pl.BlockSpec((tr, c), lambda i: (i, 0))],
        out_specs=pl.BlockSpec((tr, PROJ_WIDTH), lambda i: (i, 0)),
        out_shape=jax.ShapeDtypeStruct((r, PROJ_WIDTH), bf16),
        compiler_params=_params("arbitrary"),
        name="w_in_layout",
    )(w_in_l)


def _ada_kernel(c_ref, w_ref, b_ref, o_ref):
    c = c_ref[...]
    ca = (c * jax.nn.sigmoid(c)).astype(bf16)
    o_ref[...] = jnp.dot(ca, w_ref[...].astype(bf16), preferred_element_type=f32) + b_ref[...]


def _ada(c8, w, b):
    n = w.shape[1]
    tn = 512
    return pl.pallas_call(
        _ada_kernel,
        grid=(n // tn,),
        in_specs=[pl.BlockSpec((8, D_MODEL), lambda j: (0, 0)),
                  pl.BlockSpec((D_MODEL, tn), lambda j: (0, j)),
                  pl.BlockSpec((1, tn), lambda j: (0, j))],
        out_specs=pl.BlockSpec((8, tn), lambda j: (0, j)),
        out_shape=jax.ShapeDtypeStruct((8, n), f32),
        compiler_params=_params("arbitrary"),
        name="ada_mod",
    )(c8, w, b)


def _normmod_kernel(x_ref, g_ref, sh_ref, sc_ref, o_ref, *, transposed):
    x = x_ref[...]
    y = x * lax.rsqrt(jnp.mean(x * x, axis=-1, keepdims=True) + EPS) * g_ref[...]
    y = y * (1.0 + sc_ref[0]) + sh_ref[0]
    o_ref[...] = (y.T if transposed else y).astype(o_ref.dtype)


def _normmod(x2d, g, modr, shift_slot, scale_slot, transposed=False):
    ts = 256
    per_b = SEQ // ts
    if transposed:
        out_spec = pl.BlockSpec((D_MODEL, ts), lambda i: (0, i))
        out_shape = jax.ShapeDtypeStruct((D_MODEL, TOKENS), bf16)
    else:
        out_spec = pl.BlockSpec((ts, D_MODEL), lambda i: (i, 0))
        out_shape = jax.ShapeDtypeStruct((TOKENS, D_MODEL), bf16)
    return pl.pallas_call(
        functools.partial(_normmod_kernel, transposed=transposed),
        grid=(TOKENS // ts,),
        in_specs=[pl.BlockSpec((ts, D_MODEL), lambda i: (i, 0)),
                  pl.BlockSpec((1, D_MODEL), lambda i: (0, 0)),
                  pl.BlockSpec((1, 1, D_MODEL), lambda i: ((i // per_b) * N_MOD + shift_slot, 0, 0)),
                  pl.BlockSpec((1, 1, D_MODEL), lambda i: ((i // per_b) * N_MOD + scale_slot, 0, 0))],
        out_specs=out_spec,
        out_shape=out_shape,
        compiler_params=_params("arbitrary"),
        name="norm_modulate",
    )(x2d, g, modr, modr)


def _mm_kernel(a_ref, w_ref, *rest, epilogue):
    o_ref = rest[-1]
    acc = jnp.dot(a_ref[...], w_ref[...], preferred_element_type=f32)
    o_ref[...] = epilogue(acc, *rest[:-1]).astype(o_ref.dtype)


def _matmul(a, w, out_dtype, *, tm, tn, epilogue=None, extra=(), extra_specs=(), name):
    m, k = a.shape
    n = w.shape[1]
    if epilogue is None:
        epilogue = lambda acc: acc
    return pl.pallas_call(
        functools.partial(_mm_kernel, epilogue=epilogue),
        grid=(m // tm, n // tn),
        in_specs=[pl.BlockSpec((tm, k), lambda i, j: (i, 0)),
                  pl.BlockSpec((k, tn), lambda i, j: (0, j)),
                  *extra_specs],
        out_specs=pl.BlockSpec((tm, tn), lambda i, j: (i, j)),
        out_shape=jax.ShapeDtypeStruct((m, n), out_dtype),
        compiler_params=_params("arbitrary", "arbitrary"),
        name=name,
    )(a, w, *extra)


def _mm_w32_kernel(a_ref, w_ref, *rest, epilogue):
    o_ref, wb_ref = rest[-2], rest[-1]

    @pl.when(pl.program_id(1) == 0)
    def _():
        wb_ref[...] = w_ref[...].astype(bf16)

    acc = jnp.dot(a_ref[...], wb_ref[...], preferred_element_type=f32)
    o_ref[...] = epilogue(acc, *rest[:-2]).astype(o_ref.dtype)


def _matmul_w32(a, w, out_dtype, *, tm, tn, epilogue=None, extra=(), extra_specs=(), name):
    m, k = a.shape
    n = w.shape[1]
    if epilogue is None:
        epilogue = lambda acc: acc
    return pl.pallas_call(
        functools.partial(_mm_w32_kernel, epilogue=epilogue),
        grid=(n // tn, m // tm),
        in_specs=[pl.BlockSpec((tm, k), lambda j, i: (i, 0)),
                  pl.BlockSpec((k, tn), lambda j, i: (0, j)),
                  *extra_specs],
        out_specs=pl.BlockSpec((tm, tn), lambda j, i: (i, j)),
        out_shape=jax.ShapeDtypeStruct((m, n), out_dtype),
        scratch_shapes=[pltpu.VMEM((k, tn), bf16)],
        compiler_params=_params("arbitrary", "arbitrary"),
        name=name,
    )(a, w, *extra)


def _gate_epilogue(acc, b_ref):
    return jax.nn.sigmoid(acc + b_ref[...])


def _resid_epilogue(acc, x_ref, g_ref):
    return x_ref[...] + g_ref[0] * acc


def _rms(x, g):
    return x * lax.rsqrt(jnp.mean(x * x, axis=-1, keepdims=True) + EPS) * g


def _dsa_prep_kernel(p_ref, gcq_ref, gckv_ref, gki_ref, bki_ref, cq_o, ckv_o, ki_o, wi_o):
    cq_o[...] = _rms(p_ref[:, OFF_CQ:OFF_CQ + A_Q_RANK], gcq_ref[...]).astype(bf16)
    ckv_o[...] = _rms(p_ref[:, OFF_CKV:OFF_CKV + A_KV_RANK], gckv_ref[...]).astype(bf16)
    ki = p_ref[:, OFF_KI:OFF_KI + IDX_DIM]
    mu = jnp.mean(ki, axis=-1, keepdims=True)
    var = jnp.mean(jnp.square(ki - mu), axis=-1, keepdims=True)
    ki_o[...] = ((ki - mu) * lax.rsqrt(var + EPS) * gki_ref[...] + bki_ref[...]).astype(bf16)
    wi_o[...] = p_ref[:, OFF_WI:OFF_WI + 128] * (IDX_HEADS ** -0.5 * IDX_DIM ** -0.5)


def _dsa_prep(proj, g_cq, g_ckv, g_ki, b_ki):
    ts = 512
    row = lambda i: (i, 0)
    fixed = lambda i: (0, 0)
    return pl.pallas_call(
        _dsa_prep_kernel,
        grid=(TOKENS // ts,),
        in_specs=[pl.BlockSpec((ts, 2048), row),
                  pl.BlockSpec((1, A_Q_RANK), fixed),
                  pl.BlockSpec((1, A_KV_RANK), fixed),
                  pl.BlockSpec((1, IDX_DIM), fixed),
                  pl.BlockSpec((1, IDX_DIM), fixed)],
        out_specs=[pl.BlockSpec((ts, A_Q_RANK), row),
                   pl.BlockSpec((ts, A_KV_RANK), row),
                   pl.BlockSpec((ts, IDX_DIM), row),
                   pl.BlockSpec((ts, 128), row)],
        out_shape=[jax.ShapeDtypeStruct((TOKENS, A_Q_RANK), bf16),
                   jax.ShapeDtypeStruct((TOKENS, A_KV_RANK), bf16),
                   jax.ShapeDtypeStruct((TOKENS, IDX_DIM), bf16),
                   jax.ShapeDtypeStruct((TOKENS, 128), f32)],
        compiler_params=_params("arbitrary"),
        name="dsa_prep",
    )(proj, g_cq, g_ckv, g_ki, b_ki)


def _qlat_kernel(q_ref, w_ref, o_ref):
    for h in range(A_HEADS):
        r = jnp.dot(q_ref[:, h * A_HEAD_DIM:(h + 1) * A_HEAD_DIM], w_ref[h], preferred_element_type=f32)
        o_ref[:, h] = r.astype(o_ref.dtype).reshape(o_ref.shape[0], Q_BLOCK, A_KV_RANK)


def _qlat(qq, wuk_t):
    tm = 512
    q_blk = IDX_HEADS * IDX_DIM // A_WIDTH
    return pl.pallas_call(
        _qlat_kernel,
        grid=(TOKENS // tm,),
        in_specs=[pl.BlockSpec((tm, A_WIDTH), lambda i: (i, q_blk)),
                  pl.BlockSpec((A_HEADS, A_HEAD_DIM, A_KV_RANK), lambda i: (0, 0, 0))],
        out_specs=pl.BlockSpec((tm // Q_BLOCK, A_HEADS, Q_BLOCK, A_KV_RANK), lambda i: (i, 0, 0, 0)),
        out_shape=jax.ShapeDtypeStruct((TOKENS // Q_BLOCK, A_HEADS, Q_BLOCK, A_KV_RANK), bf16),
        compiler_params=_params("arbitrary"),
        name="q_lat",
    )(qq, wuk_t)


def _t5_kernel(t5_ref, o_ref):
    h = pl.program_id(0)
    half = T5_BUCKETS // 2
    exact = half // 2
    qi = lax.broadcasted_iota(i32, (Q_BLOCK, NEAR), 0)
    kj = lax.broadcasted_iota(i32, (Q_BLOCK, NEAR), 1)
    rel = kj - NEAR_BACK - qi
    n = jnp.abs(rel)
    log_ratio = jnp.log(jnp.maximum(n, 1).astype(f32) / exact) / math.log(T5_MAX_DIST / exact)
    large = jnp.minimum(exact + (log_ratio * (half - exact)).astype(i32), half - 1)
    bucket = jnp.where(rel > 0, half, 0) + jnp.where(n < exact, n, large)
    acc = jnp.zeros((Q_BLOCK, NEAR), f32)
    for k in range(T5_BUCKETS):
        acc = jnp.where(bucket == k, t5_ref[k, h], acc)
    o_ref[0] = (acc - t5_ref[half - 1, h]) * (1.0 / ATTN_SCALE)


def _t5_table(t5_bias):
    return pl.pallas_call(
        _t5_kernel,
        grid=(A_HEADS,),
        in_specs=[pl.BlockSpec(memory_space=pltpu.SMEM)],
        out_specs=pl.BlockSpec((1, Q_BLOCK, NEAR), lambda h: (h, 0, 0)),
        out_shape=jax.ShapeDtypeStruct((A_HEADS, Q_BLOCK, NEAR), f32),
        compiler_params=_params("arbitrary"),
        name="t5_table",
    )(t5_bias)


DSA_COL_STEP = 512
DSA_HEAD_GROUP = 8
DSA_ROWS = DSA_HEAD_GROUP * Q_BLOCK


def _dsa_block(width, qb, t5_ref, qi_ref, kit_ref, wi_ref, ql_ref, kv_ref, wuv_ref, pt_ref, o_ref,
               key_ref, mask_ref, lg_ref, e_ref, l_ref):
    q0 = qb * Q_BLOCK
    ct_w = 256
    w = wi_ref[...]
    score_ref = lg_ref.at[0:Q_BLOCK]

    def col_tile(ct, carry):
        c0 = pl.multiple_of(ct * ct_w, ct_w)
        kt = kit_ref[:, pl.ds(c0, ct_w)]
        acc = jnp.zeros((Q_BLOCK, ct_w), f32)
        for h in range(IDX_HEADS):
            x = jnp.dot(qi_ref[:, h * IDX_DIM:(h + 1) * IDX_DIM], kt, preferred_element_type=f32)
            acc = acc + jnp.maximum(x, 0.0) * w[:, h:h + 1]
        score_ref[:, pl.ds(c0, ct_w)] = acc
        return carry

    lax.fori_loop(0, width // ct_w, col_tile, 0)

    row = lax.broadcasted_iota(i32, (Q_BLOCK, width), 0)
    col = lax.broadcasted_iota(i32, (Q_BLOCK, width), 1)
    adm = (col // CHUNK) <= ((q0 + row) // CHUNK)
    score = jnp.where(adm, score_ref[:, 0:width], -jnp.inf)
    bits = pltpu.bitcast(score, i32)
    key_ref[:, 0:width] = jnp.where(bits < 0, bits ^ 0x7FFFFFFF, bits)

    def bisect(b, thr_u):
        cand_u = thr_u | jnp.left_shift(jnp.int32(1), 31 - b)
        hit = jnp.where(key_ref[:, 0:width] >= (cand_u ^ INT_MIN), 1.0, 0.0)
        cnt = jnp.sum(hit, axis=1, keepdims=True)
        return jnp.where(cnt >= TOPK, cand_u, thr_u)

    thr_u = lax.fori_loop(0, 32, bisect, jnp.zeros((Q_BLOCK, 1), i32))
    sel = (key_ref[:, 0:width] >= (thr_u ^ INT_MIN)) & adm
    mask_ref[:, 0:width] = jnp.where(sel, 0.0, -jnp.inf)

    far_bucket = T5_BUCKETS // 2 - 1

    kv = kv_ref[0:width, :]

    def head_group(g, carry):
        h0 = g * DSA_HEAD_GROUP
        qg = ql_ref[0, pl.ds(h0, DSA_HEAD_GROUP)].reshape(DSA_ROWS, A_KV_RANK)
        lg_ref[:, 0:width] = lax.dot_general(qg, kv, (((1,), (1,)), ((), ())), preferred_element_type=f32)

        @pl.when(qb == 0)
        def _():
            for hh in range(DSA_HEAD_GROUP):
                lg_ref[hh * Q_BLOCK:(hh + 1) * Q_BLOCK, 0:Q_BLOCK] += pt_ref[h0 + hh, :, NEAR_BACK:NEAR]

        @pl.when(qb > 0)
        def _():
            w0 = pl.multiple_of(q0 - NEAR_BACK, 128)
            for hh in range(DSA_HEAD_GROUP):
                lg_ref[hh * Q_BLOCK:(hh + 1) * Q_BLOCK, pl.ds(w0, NEAR)] += pt_ref[h0 + hh]

        for hh in range(DSA_HEAD_GROUP):
            rs = slice(hh * Q_BLOCK, (hh + 1) * Q_BLOCK)
            x = lg_ref[rs, 0:width] * ATTN_SCALE + (mask_ref[:, 0:width] + t5_ref[far_bucket, h0 + hh])
            m = jnp.max(x, axis=1, keepdims=True)
            e = jnp.exp(x - m)
            l_ref[rs, :] = jnp.sum(e, axis=1, keepdims=True)
            e_ref[rs, 0:width] = e.astype(bf16)

        ol = jnp.dot(e_ref[:, 0:width], kv, preferred_element_type=f32) / l_ref[...]
        for hh in range(DSA_HEAD_GROUP):
            o = jnp.dot(ol[hh * Q_BLOCK:(hh + 1) * Q_BLOCK].astype(bf16), wuv_ref[h0 + hh],
                        preferred_element_type=f32)
            c0 = pl.multiple_of((h0 + hh) * A_HEAD_DIM, A_HEAD_DIM)
            o_ref[:, pl.ds(c0, A_HEAD_DIM)] = o.astype(o_ref.dtype)
        return carry

    lax.fori_loop(0, A_HEADS // DSA_HEAD_GROUP, head_group, 0)


def _dsa_kernel(t5_ref, qi_ref, kit_ref, wi_ref, ql_ref, kv_ref, wuv_ref, pt_ref, o_ref,
                key_ref, mask_ref, lg_ref, e_ref, l_ref):
    qb = pl.program_id(1)
    blocks_per_step = DSA_COL_STEP // Q_BLOCK
    for n in range(SEQ // DSA_COL_STEP):
        pl.when(qb // blocks_per_step == n)(functools.partial(
            _dsa_block, (n + 1) * DSA_COL_STEP, qb, t5_ref, qi_ref, kit_ref, wi_ref, ql_ref, kv_ref,
            wuv_ref, pt_ref, o_ref, key_ref, mask_ref, lg_ref, e_ref, l_ref))


def _dsa(t5_bias, qq, kidx_t, widx, qlat, ckv, wuv, ptab):
    nqb = SEQ // Q_BLOCK
    return pl.pallas_call(
        _dsa_kernel,
        grid=(BATCH, nqb),
        in_specs=[pl.BlockSpec(memory_space=pltpu.SMEM),
                  pl.BlockSpec((Q_BLOCK, IDX_HEADS * IDX_DIM), lambda b, i: (b * nqb + i, 0)),
                  pl.BlockSpec((IDX_DIM, SEQ), lambda b, i: (0, b)),
                  pl.BlockSpec((Q_BLOCK, 128), lambda b, i: (b * nqb + i, 0)),
                  pl.BlockSpec((1, A_HEADS, Q_BLOCK, A_KV_RANK), lambda b, i: (b * nqb + i, 0, 0, 0)),
                  pl.BlockSpec((SEQ, A_KV_RANK), lambda b, i: (b, 0)),
                  pl.BlockSpec((A_HEADS, A_KV_RANK, A_HEAD_DIM), lambda b, i: (0, 0, 0)),
                  pl.BlockSpec((A_HEADS, Q_BLOCK, NEAR), lambda b, i: (0, 0, 0))],
        out_specs=pl.BlockSpec((Q_BLOCK, A_WIDTH), lambda b, i: (b * nqb + i, 0)),
        out_shape=jax.ShapeDtypeStruct((TOKENS, A_WIDTH), bf16),
        scratch_shapes=[pltpu.VMEM((Q_BLOCK, SEQ), i32),
                        pltpu.VMEM((Q_BLOCK, SEQ), f32),
                        pltpu.VMEM((DSA_ROWS, SEQ), f32),
                        pltpu.VMEM((DSA_ROWS, SEQ), bf16),
                        pltpu.VMEM((DSA_ROWS, 1), f32)],
        compiler_params=_params("arbitrary", "arbitrary"),
        name="dsa_main",
    )(t5_bias, qq, kidx_t, widx, qlat, ckv, wuv, ptab)


def _ret_kernel(q_ref, k_ref, v_ref, g_ref, cc_ref, ss_ref, dm_ref, qd_ref, kd_ref, cd_ref, gr_ref,
                o_ref, qs_ref, ks_ref):
    cc = cc_ref[...]
    ss = ss_ref[...]
    q = q_ref[...]
    k = k_ref[...] * (R_QK_DIM ** -0.5)
    half = R_QK_DIM // 2
    qs_ref[...] = q * cc + pltpu.roll(q, half, 1) * ss
    ks_ref[...] = k * cc + pltpu.roll(k, half, 1) * ss
    dm = dm_ref[0]
    qd = qd_ref[0]
    kd = kd_ref[0]
    cd = cd_ref[0]
    gr = gr_ref[0]

    def chunk(n, state):
        r0 = pl.multiple_of(n * CHUNK, CHUNK)
        qn = qs_ref[pl.ds(r0, CHUNK), :]
        kn = ks_ref[pl.ds(r0, CHUNK), :]
        vn = v_ref[pl.ds(r0, CHUNK), :].astype(bf16)
        att = lax.dot_general(qn.astype(bf16), kn.astype(bf16), (((1,), (1,)), ((), ())),
                              preferred_element_type=f32) * dm
        y = jnp.dot(att.astype(bf16), vn, preferred_element_type=f32)
        y = y + jnp.dot((qn * qd).astype(bf16), state.astype(bf16), preferred_element_type=f32)
        kv = lax.dot_general((kn * kd).astype(bf16), vn, (((0,), (0,)), ((), ())),
                             preferred_element_type=f32)
        y = y * lax.rsqrt(jnp.mean(y * y, axis=-1, keepdims=True) + EPS) * gr
        gate = g_ref[pl.ds(r0, CHUNK), :]
        o_ref[pl.ds(r0, CHUNK), :] = (gate * jax.nn.sigmoid(gate) * y).astype(o_ref.dtype)
        return state * cd + kv

    lax.fori_loop(0, SEQ // CHUNK, chunk, jnp.zeros((R_QK_DIM, R_V_DIM), f32), unroll=8)


def _retention(proj, cc, ss, dmask, qdec, kdec, cdec, g_ret):
    qk_blk = OFF_RQ // R_QK_DIM
    k_blk = OFF_RK // R_QK_DIM
    v_blk = OFF_RV // R_V_DIM
    g_blk = OFF_RG // R_V_DIM
    per_head = lambda b, h: (h, 0, 0)
    return pl.pallas_call(
        _ret_kernel,
        grid=(BATCH, R_HEADS),
        in_specs=[pl.BlockSpec((SEQ, R_QK_DIM), lambda b, h: (b, qk_blk + h)),
                  pl.BlockSpec((SEQ, R_QK_DIM), lambda b, h: (b, k_blk + h)),
                  pl.BlockSpec((SEQ, R_V_DIM), lambda b, h: (b, v_blk + h)),
                  pl.BlockSpec((SEQ, R_V_DIM), lambda b, h: (b, g_blk + h)),
                  pl.BlockSpec((SEQ, R_QK_DIM), lambda b, h: (0, 0)),
                  pl.BlockSpec((SEQ, R_QK_DIM), lambda b, h: (0, 0)),
                  pl.BlockSpec((1, CHUNK, CHUNK), per_head),
                  pl.BlockSpec((1, CHUNK, R_QK_DIM), per_head),
                  pl.BlockSpec((1, CHUNK, R_QK_DIM), per_head),
                  pl.BlockSpec((1, 1, R_V_DIM), per_head),
                  pl.BlockSpec((1, 1, R_V_DIM), per_head)],
        out_specs=pl.BlockSpec((SEQ, R_V_DIM), lambda b, h: (b, h)),
        out_shape=jax.ShapeDtypeStruct((TOKENS, R_WIDTH), bf16),
        scratch_shapes=[pltpu.VMEM((SEQ, R_QK_DIM), f32), pltpu.VMEM((SEQ, R_QK_DIM), f32)],
        compiler_params=_params("arbitrary", "arbitrary"),
        name="retention",
    )(proj, proj, proj, proj, cc, ss, dmask, qdec, kdec, cdec, g_ret)


def _merge_kernel(ya_ref, yr_ref, wa_ref, wr_ref, ga_ref, gr_ref, o_ref, wab_ref, wrb_ref):
    @pl.when(pl.program_id(1) == 0)
    def _():
        wab_ref[...] = wa_ref[...].astype(bf16)
        wrb_ref[...] = wr_ref[...].astype(bf16)

    pa = jnp.dot(ya_ref[...], wab_ref[...], preferred_element_type=f32)
    pr = jnp.dot(yr_ref[...], wrb_ref[...], preferred_element_type=f32)
    o_ref[...] = (ga_ref[...].astype(f32) * pa + gr_ref[...].astype(f32) * pr).astype(o_ref.dtype)


def _merge(ya, yr, w_up, gates):
    assert A_WIDTH == R_WIDTH
    tm, tn = 1024, 512
    nj = D_MODEL // tn
    return pl.pallas_call(
        _merge_kernel,
        grid=(nj, TOKENS // tm),
        in_specs=[pl.BlockSpec((tm, A_WIDTH), lambda j, i: (i, 0)),
                  pl.BlockSpec((tm, R_WIDTH), lambda j, i: (i, 0)),
                  pl.BlockSpec((A_WIDTH, tn), lambda j, i: (0, j)),
                  pl.BlockSpec((R_WIDTH, tn), lambda j, i: (1, j)),
                  pl.BlockSpec((tm, tn), lambda j, i: (i, j)),
                  pl.BlockSpec((tm, tn), lambda j, i: (i, nj + j))],
        out_specs=pl.BlockSpec((tm, tn), lambda j, i: (i, j)),
        out_shape=jax.ShapeDtypeStruct((TOKENS, D_MODEL), bf16),
        scratch_shapes=[pltpu.VMEM((A_WIDTH, tn), bf16), pltpu.VMEM((R_WIDTH, tn), bf16)],
        compiler_params=_params("arbitrary", "arbitrary"),
        name="merge_up",
    )(ya, yr, w_up, w_up, gates, gates)


ROUTE_CAND_GROUPS = ((0, 16), (1, 8), (2, 8), (3, 8), (4, 8), (5, 8), (6, 8), (7, 8))
ROUTE_CAND_ROWS = sum(n for _, n in ROUTE_CAND_GROUPS) + 8


def _peer_route_kernel(q_ref, k_ref, s1_o, s2_o, a1_o, m2_o, thr_o, v1_ref, v2_ref, cand_ref, ec_ref):
    hq = P_QUERY_DIM // 2
    s1_all = jnp.dot(k_ref[0], q_ref[0:hq, :].astype(bf16), preferred_element_type=f32)
    s2_all = jnp.dot(k_ref[1], q_ref[hq:P_QUERY_DIM, :].astype(bf16), preferred_element_type=f32)

    def top_values(s, v_ref):
        cur = s
        for r in range(P_TOPK):
            m = jnp.max(cur, axis=0, keepdims=True)
            v_ref[r:r + 1, :] = m
            cur = jnp.where(cur == m, -jnp.inf, cur)

    for lt in range(s1_all.shape[1] // 128):
        ls = slice(lt * 128, (lt + 1) * 128)
        s1 = s1_all[:, ls]
        s2 = s2_all[:, ls]
        top_values(s1, v1_ref)
        top_values(s2, v2_ref)
        v1 = v1_ref[...]
        v2 = v2_ref[...]
        m1 = v1[0:1]
        m2 = v2[0:1]
        e1 = jnp.exp(v1 - m1)
        e2 = jnp.exp(v2 - m2)
        off = 0
        for r1, n in ROUTE_CAND_GROUPS:
            cand_ref[off:off + n, :] = v1[r1:r1 + 1] + v2[0:n]
            ec_ref[off:off + n, :] = e1[r1:r1 + 1] * e2[0:n]
            off += n
        cand_ref[off:off + 8, :] = v1[8:16] + v2[0:1]
        ec_ref[off:off + 8, :] = e1[8:16] * e2[0:1]
        cand = cand_ref[...]
        cur = cand
        thr = None
        for r in range(P_TOPK):
            thr = jnp.max(cur, axis=0, keepdims=True)
            cur = jnp.where(cur == thr, -jnp.inf, cur)
        z = jnp.sum(jnp.where(cand >= thr, ec_ref[...], 0.0), axis=0, keepdims=True)
        a1 = jnp.exp(s1 - m1) / z
        for r in range(PEER_NC):
            s1_o[0, r, :, ls] = s1[r * PEER_ROWS:(r + 1) * PEER_ROWS]
            a1_o[0, r, :, ls] = a1[r * PEER_ROWS:(r + 1) * PEER_ROWS]
        s2_o[0, lt] = s2
        m2_o[0, :, ls] = m2
        thr_o[0, :, ls] = thr


def _peer_route(q_t, keys):
    tl = 512
    big = lambda h, j: (h, 0, j)
    chunked = lambda h, j: (h, 0, 0, j)
    chunked_shape = jax.ShapeDtypeStruct((P_HEADS, PEER_NC, PEER_ROWS, TOKENS), f32)
    row_shape = jax.ShapeDtypeStruct((P_HEADS, 1, TOKENS), f32)
    return pl.pallas_call(
        _peer_route_kernel,
        grid=(P_HEADS, TOKENS // tl),
        in_specs=[pl.BlockSpec((P_QUERY_DIM, tl), lambda h, j: (h, j)),
                  pl.BlockSpec((2, P_NKEYS, P_QUERY_DIM // 2), lambda h, j: (0, 0, 0))],
        out_specs=[pl.BlockSpec((1, PEER_NC, PEER_ROWS, tl), chunked),
                   pl.BlockSpec((1, tl // 128, P_NKEYS, 128), lambda h, j: (h, j, 0, 0)),
                   pl.BlockSpec((1, PEER_NC, PEER_ROWS, tl), chunked),
                   pl.BlockSpec((1, 1, tl), big),
                   pl.BlockSpec((1, 1, tl), big)],
        out_shape=[chunked_shape, jax.ShapeDtypeStruct((P_HEADS, TOKENS // 128, P_NKEYS, 128), f32),
                   chunked_shape, row_shape, row_shape],
        scratch_shapes=[pltpu.VMEM((P_TOPK, 128), f32), pltpu.VMEM((P_TOPK, 128), f32),
                        pltpu.VMEM((ROUTE_CAND_ROWS, 128), f32), pltpu.VMEM((ROUTE_CAND_ROWS, 128), f32)],
        compiler_params=_params("arbitrary", "arbitrary"),
        name="peer_route",
    )(q_t, keys)


PEER_TM = 512
PEER_TE = 512
PEER_NC = P_EXPERTS // PEER_TE
PEER_ROWS = PEER_TE // P_NKEYS
PEER_KT = 64
PEER_KPIECES = 8


def _peer_kernel(ht_ref, u_ref, vt_ref, s1_ref, a1_ref, s2_ref, m2_ref, thr_ref, o_ref,
                 b2_ref, pre_ref, w_ref):
    c = pl.program_id(1)

    @pl.when(c == 0)
    def _():
        o_ref[...] = jnp.zeros_like(o_ref)
        for lt in range(PEER_TM // 128):
            b2_ref[:, lt] = jnp.exp(s2_ref[:, lt] - m2_ref[:, :, lt * 128:(lt + 1) * 128])

    @pl.when(c > 0)
    def _():
        pre = pre_ref[...]
        act = 0.5 * pre * (1.0 + lax.erf(pre * np.float32(np.sqrt(0.5))))
        coef = (w_ref[0:PEER_TE, :] * act).astype(bf16)
        o_ref[...] += jnp.dot(vt_ref[...], coef, preferred_element_type=f32)

    @pl.when(c < PEER_NC)
    def _():
        z = pl.multiple_of(jnp.minimum(c, 0), PEER_TE)
        units = [(j, lt, kt) for j in range(PEER_ROWS) for lt in range(PEER_TM // 128)
                 for kt in range(P_NKEYS // PEER_KT)]
        per_piece = len(units) // PEER_KPIECES
        kw = D_MODEL // PEER_KPIECES
        for kq in range(PEER_KPIECES):
            part = jnp.dot(u_ref[:, kq * kw:(kq + 1) * kw], ht_ref[kq * kw:(kq + 1) * kw, :],
                           preferred_element_type=f32)
            if kq == 0:
                pre_ref[...] = part
            else:
                pre_ref[...] += part
            w_ref[pl.ds(z + PEER_TE, 8), 0:128] = part[PEER_TE - 8:PEER_TE, PEER_TM - 128:PEER_TM]
            for j, lt, kt in units[kq * per_piece:(kq + 1) * per_piece]:
                ls = slice(lt * 128, (lt + 1) * 128)
                ks = slice(kt * PEER_KT, (kt + 1) * PEER_KT)
                wj = jnp.zeros((PEER_KT, 128), f32)
                for h in range(P_HEADS):
                    sel = (s1_ref[h, 0, j:j + 1, ls] + s2_ref[h, lt, ks, :]) >= thr_ref[h, :, ls]
                    wj = wj + jnp.where(sel, b2_ref[h, lt, ks, :], 0.0) * a1_ref[h, 0, j:j + 1, ls]
                w0 = pl.multiple_of(z + (j * P_NKEYS + kt * PEER_KT), PEER_KT)
                w_ref[pl.ds(w0, PEER_KT), ls] = wj


def _peer(h_t, u, v_t, s1, a1, s2, m2, thr):
    tm, te, nc = PEER_TM, PEER_TE, PEER_NC
    s1r, a1r = s1, a1
    tok3 = lambda i, c: (0, 0, i)
    cur = lambda i, c: (0, jnp.minimum(c, nc - 1), 0, i)
    return pl.pallas_call(
        _peer_kernel,
        grid=(TOKENS // tm, nc + 1),
        in_specs=[pl.BlockSpec((D_MODEL, tm), lambda i, c: (0, i)),
                  pl.BlockSpec((te, D_MODEL), lambda i, c: (jnp.minimum(c, nc - 1), 0)),
                  pl.BlockSpec((D_MODEL, te), lambda i, c: (0, jnp.maximum(c - 1, 0))),
                  pl.BlockSpec((P_HEADS, 1, PEER_ROWS, tm), cur),
                  pl.BlockSpec((P_HEADS, 1, PEER_ROWS, tm), cur),
                  pl.BlockSpec((P_HEADS, tm // 128, P_NKEYS, 128), lambda i, c: (0, i, 0, 0)),
                  pl.BlockSpec((P_HEADS, 1, tm), tok3),
                  pl.BlockSpec((P_HEADS, 1, tm), tok3)],
        out_specs=pl.BlockSpec((D_MODEL, tm), lambda i, c: (0, i)),
        out_shape=jax.ShapeDtypeStruct((D_MODEL, TOKENS), f32),
        scratch_shapes=[pltpu.VMEM((P_HEADS, tm // 128, P_NKEYS, 128), f32),
                        pltpu.VMEM((te, tm), f32), pltpu.VMEM((te + 8, tm), f32)],
        compiler_params=_params("arbitrary", "arbitrary"),
        name="peer_experts",
    )(h_t, u, v_t, s1r, a1r, s2, m2, thr)


def _final_kernel(x_ref, yt_ref, gt_ref, g_ref, o_ref):
    x = x_ref[...] + gt_ref[0] * yt_ref[...].T
    o_ref[...] = x * lax.rsqrt(jnp.mean(x * x, axis=-1, keepdims=True) + EPS) * g_ref[...]


def _final(x1, y_t, modr, g_final):
    ts = 256
    per_b = SEQ // ts
    return pl.pallas_call(
        _final_kernel,
        grid=(TOKENS // ts,),
        in_specs=[pl.BlockSpec((ts, D_MODEL), lambda i: (i, 0)),
                  pl.BlockSpec((D_MODEL, ts), lambda i: (0, i)),
                  pl.BlockSpec((1, 1, D_MODEL), lambda i: ((i // per_b) * N_MOD + 5, 0, 0)),
                  pl.BlockSpec((1, D_MODEL), lambda i: (0, 0))],
        out_specs=pl.BlockSpec((ts, D_MODEL), lambda i: (i, 0)),
        out_shape=jax.ShapeDtypeStruct((TOKENS, D_MODEL), f32),
        compiler_params=_params("arbitrary"),
        name="final_norm",
    )(x1, y_t, modr, g_final)


def _retention_tables():
    half = R_QK_DIM // 2
    inv = 1.0 / (ROT_BASE ** jnp.linspace(0.0, 1.0, half, dtype=f32))
    ang = jnp.arange(SEQ, dtype=f32)[:, None] * inv[None, :]
    cos, sin = jnp.cos(ang), jnp.sin(ang)
    cc = jnp.concatenate([cos, cos], axis=-1)
    ss = jnp.concatenate([-sin, sin], axis=-1)
    log_g = jnp.log(1.0 - jnp.power(2.0, -5.0 - jnp.arange(R_HEADS, dtype=f32)))
    j = jnp.arange(CHUNK, dtype=f32)
    diff = j[:, None] - j[None, :]
    dmask = jnp.where(diff[None] >= 0, jnp.exp(jnp.maximum(diff, 0.0)[None] * log_g[:, None, None]), 0.0)
    kdec = jnp.exp((CHUNK - 1.0 - j)[None, :] * log_g[:, None])
    qdec = jnp.exp((j + 1.0)[None, :] * log_g[:, None])
    cdec = jnp.exp(CHUNK * log_g)
    kdec = jnp.broadcast_to(kdec[:, :, None], (R_HEADS, CHUNK, R_QK_DIM))
    qdec = jnp.broadcast_to(qdec[:, :, None], (R_HEADS, CHUNK, R_QK_DIM))
    cdec = jnp.broadcast_to(cdec[:, None, None], (R_HEADS, 1, R_V_DIM))
    return cc, ss, dmask, qdec, kdec, cdec


def kernel(x, c, w_ada, b_ada, g_mix, w_in, g_cq, g_ckv, w_uq, w_uk, w_uv, w_qi, g_ki, b_ki, t5_bias, g_ret,
           w_up, w_gate, b_gate, w_out, g_ffn, w_pq, sub_keys, u_exp, v_exp, g_final):
    x2d = x.reshape(TOKENS, D_MODEL)

    w_in_p = _w_in_layout(w_in[0])
    w_q_all = jnp.concatenate(
        [w_qi[0].reshape(A_Q_RANK, IDX_HEADS * IDX_DIM), w_uq[0].reshape(A_Q_RANK, A_WIDTH)], axis=1).astype(bf16)
    wuk_t = jnp.transpose(w_uk[0], (1, 2, 0)).astype(bf16)
    wuv_h = jnp.transpose(w_uv[0], (1, 0, 2)).astype(bf16)
    w_pq_t = w_pq[0].T.astype(bf16)
    keys_b = sub_keys[0].astype(bf16)
    u_b = _cast_bf16(u_exp[0])
    v_t = _cast_bf16(v_exp[0], transposed=True)

    c8 = jnp.pad(c, ((0, 8 - BATCH), (0, 0)))
    mod = _ada(c8, w_ada[0], b_ada[0].reshape(1, N_MOD * D_MODEL))[:BATCH]
    modr = mod.reshape(BATCH * N_MOD, 1, D_MODEL)

    h = _normmod(x2d, g_mix[0].reshape(1, D_MODEL), modr, 0, 1)
    proj = _matmul(h, w_in_p, f32, tm=1024, tn=512, name="in_proj")
    gates = _matmul_w32(h, w_gate[0], bf16, tm=1024, tn=512, epilogue=_gate_epilogue,
                        extra=(b_gate[0].reshape(1, 2 * D_MODEL),),
                        extra_specs=(pl.BlockSpec((1, 512), lambda j, i: (0, j)),), name="gates")

    cqn, ckvn, kidx, widx = _dsa_prep(proj, g_cq[0].reshape(1, -1), g_ckv[0].reshape(1, -1),
                                      g_ki[0].reshape(1, -1), b_ki[0].reshape(1, -1))
    qq = _matmul(cqn, w_q_all, bf16, tm=1024, tn=1024, name="q_up")
    qlat = _qlat(qq, wuk_t)
    ptab = _t5_table(t5_bias)
    y_a = _dsa(t5_bias, qq, kidx.T, widx, qlat, ckvn, wuv_h, ptab)

    cc, ss, dmask, qdec, kdec, cdec = _retention_tables()
    y_r = _retention(proj, cc, ss, dmask, qdec, kdec, cdec, g_ret[0].reshape(R_HEADS, 1, R_V_DIM))

    merged = _merge(y_a, y_r, w_up[0], gates)
    per_b = SEQ // 1024
    x1 = _matmul_w32(merged, w_out[0], f32, tm=1024, tn=512, epilogue=_resid_epilogue,
                     extra=(x2d, modr),
                     extra_specs=(pl.BlockSpec((1024, 512), lambda j, i: (i, j)),
                                  pl.BlockSpec((1, 1, 512), lambda j, i: ((i // per_b) * N_MOD + 2, 0, j))),
                     name="out_proj")

    h2_t = _normmod(x1, g_ffn[0].reshape(1, D_MODEL), modr, 3, 4, transposed=True)
    q_t = _matmul(w_pq_t, h2_t, f32, tm=1024, tn=512, name="peer_q")
    s1, s2, a1, m2, thr = _peer_route(q_t, keys_b)
    y_t = _peer(h2_t, u_b, v_t, s1, a1, s2, m2, thr)
    out = _final(x1, y_t, modr, g_final.reshape(1, D_MODEL))
    return out.reshape(BATCH, SEQ, D_MODEL)
```

```python
import functools
import math

import numpy as np
import jax
import jax.numpy as jnp
from jax import lax
from jax.experimental import pallas as pl
from jax.experimental.pallas import tpu as pltpu

f32 = jnp.float32
bf16 = jnp.bfloat16
i32 = jnp.int32

D_MODEL = 4096
BATCH = 4
SEQ = 2048
TOKENS = BATCH * SEQ
CHUNK = 64
EPS = 1e-6
N_MOD = 6
A_HEADS = 16
A_HEAD_DIM = 128
A_Q_RANK = 1024
A_KV_RANK = 512
A_WIDTH = A_HEADS * A_HEAD_DIM
IDX_HEADS = 64
IDX_DIM = 128
TOPK = 256
T5_BUCKETS = 32
T5_MAX_DIST = 128
R_HEADS = 8
R_QK_DIM = 128
R_V_DIM = 256
R_QK_WIDTH = R_HEADS * R_QK_DIM
R_WIDTH = R_HEADS * R_V_DIM
ROT_BASE = 10000.0
P_HEADS = 8
P_QUERY_DIM = 256
P_NKEYS = 128
P_TOPK = 16
P_EXPERTS = P_NKEYS * P_NKEYS

PROJ_WIDTH = 8192
OFF_CQ, OFF_CKV, OFF_KI, OFF_WI = 0, 1024, 1536, 1664
OFF_RQ, OFF_RK, OFF_RV, OFF_RG = 2048, 3072, 4096, 6144

Q_BLOCK = 128
NEAR_BACK = 128
NEAR = NEAR_BACK + Q_BLOCK
ATTN_SCALE = A_HEAD_DIM ** -0.5
VMEM_LIMIT = 56 * 1024 * 1024
INT_MIN = -2147483648


def _params(*sem, flags=None):
    return pltpu.CompilerParams(dimension_semantics=sem, vmem_limit_bytes=VMEM_LIMIT, flags=flags)


def _cast_kernel(x_ref, o_ref, *, transposed):
    x = x_ref[...]
    o_ref[...] = (x.T if transposed else x).astype(o_ref.dtype)


def _cast_bf16(x, transposed=False):
    r, c = x.shape
    tr = 512
    if transposed:
        out_spec = pl.BlockSpec((c, tr), lambda i: (0, i))
        out_shape = jax.ShapeDtypeStruct((c, r), bf16)
    else:
        out_spec = pl.BlockSpec((tr, c), lambda i: (i, 0))
        out_shape = jax.ShapeDtypeStruct((r, c), bf16)
    return pl.pallas_call(
        functools.partial(_cast_kernel, transposed=transposed),
        grid=(r // tr,),
        in_specs=[pl.BlockSpec((tr, c), lambda i: (i, 0))],
        out_specs=out_spec,
        out_shape=out_shape,
        compiler_params=_params("arbitrary"),
        name="cast_bf16_t" if transposed else "cast_bf16",
    )(x)


W_IN_SLAB = 64
W_IN_TILE = 512
W_IN_SLABS = W_IN_TILE // W_IN_SLAB
W_IN_USED = OFF_WI + IDX_HEADS
W_IN_PAD_TILE = W_IN_USED // W_IN_TILE
W_IN_SHIFT = (OFF_RQ - W_IN_USED) // W_IN_SLAB


def _w_in_kernel(*refs):
    o_ref = refs[-1]
    j = pl.program_id(0)
    first_pad = (W_IN_USED - W_IN_PAD_TILE * W_IN_TILE) // W_IN_SLAB
    parts = []
    for k in range(W_IN_SLABS):
        x = refs[k][...]
        if k >= first_pad:
            x = jnp.where(j == W_IN_PAD_TILE, 0.0, x)
        parts.append(x)
    o_ref[...] = jnp.concatenate(parts, axis=0).T.astype(bf16)


def _w_in_layout(w_in_t):
    def slab(k):
        def index(j):
            src = jnp.where(j <= W_IN_PAD_TILE, j * W_IN_SLABS + k, j * W_IN_SLABS + k - W_IN_SHIFT)
            return (src, 0)
        return pl.BlockSpec((W_IN_SLAB, D_MODEL), index)

    return pl.pallas_call(
        _w_in_kernel,
        grid=(PROJ_WIDTH // W_IN_TILE,),
        in_specs=[slab(k) for k in range(W_IN_SLABS)],
        out_specs=pl.BlockSpec((D_MODEL, W_IN_TILE), lambda j: (0, j)),
        out_shape=jax.ShapeDtypeStruct((D_MODEL, PROJ_WIDTH), bf16),
        compiler_params=_params("arbitrary"),
        name="w_in_layout",
    )(*([w_in_t] * W_IN_SLABS))


def _ada_kernel(c_ref, w_ref, b_ref, o_ref):
    c = c_ref[...]
    ca = (c * jax.nn.sigmoid(c)).astype(bf16)
    o_ref[...] = jnp.dot(ca, w_ref[...].astype(bf16), preferred_element_type=f32) + b_ref[...]


def _ada(c8, w, b):
    n = w.shape[1]
    tn = 512
    return pl.pallas_call(
        _ada_kernel,
        grid=(n // tn,),
        in_specs=[pl.BlockSpec((8, D_MODEL), lambda j: (0, 0)),
                  pl.BlockSpec((D_MODEL, tn), lambda j: (0, j)),
                  pl.BlockSpec((1, tn), lambda j: (0, j))],
        out_specs=pl.BlockSpec((8, tn), lambda j: (0, j)),
        out_shape=jax.ShapeDtypeStruct((8, n), f32),
        compiler_params=_params("arbitrary"),
        name="ada_mod",
    )(c8, w, b)


def _normmod_kernel(x_ref, g_ref, sh_ref, sc_ref, o_ref, *, transposed):
    x = x_ref[...]
    y = x * lax.rsqrt(jnp.mean(x * x, axis=-1, keepdims=True) + EPS) * g_ref[...]
    y = y * (1.0 + sc_ref[0]) + sh_ref[0]
    o_ref[...] = (y.T if transposed else y).astype(o_ref.dtype)


def _normmod(x2d, g, modr, shift_slot, scale_slot, transposed=False):
    ts = 256
    per_b = SEQ // ts
    if transposed:
        out_spec = pl.BlockSpec((D_MODEL, ts), lambda i: (0, i))
        out_shape = jax.ShapeDtypeStruct((D_MODEL, TOKENS), bf16)
    else:
        out_spec = pl.BlockSpec((ts, D_MODEL), lambda i: (i, 0))
        out_shape = jax.ShapeDtypeStruct((TOKENS, D_MODEL), bf16)
    return pl.pallas_call(
        functools.partial(_normmod_kernel, transposed=transposed),
        grid=(TOKENS // ts,),
        in_specs=[pl.BlockSpec((ts, D_MODEL), lambda i: (i, 0)),
                  pl.BlockSpec((1, D_MODEL), lambda i: (0, 0)),
                  pl.BlockSpec((1, 1, D_MODEL), lambda i: ((i // per_b) * N_MOD + shift_slot, 0, 0)),
                  pl.BlockSpec((1, 1, D_MODEL), lambda i: ((i // per_b) * N_MOD + scale_slot, 0, 0))],
        out_specs=out_spec,
        out_shape=out_shape,
        compiler_params=_params("arbitrary"),
        name="norm_modulate",
    )(x2d, g, modr, modr)


def _mm_kernel(a_ref, w_ref, *rest, epilogue):
    o_ref = rest[-1]
    acc = jnp.dot(a_ref[...], w_ref[...], preferred_element_type=f32)
    o_ref[...] = epilogue(acc, *rest[:-1]).astype(o_ref.dtype)


def _matmul(a, w, out_dtype, *, tm, tn, epilogue=None, extra=(), extra_specs=(), name):
    m, k = a.shape
    n = w.shape[1]
    if epilogue is None:
        epilogue = lambda acc: acc
    return pl.pallas_call(
        functools.partial(_mm_kernel, epilogue=epilogue),
        grid=(m // tm, n // tn),
        in_specs=[pl.BlockSpec((tm, k), lambda i, j: (i, 0)),
                  pl.BlockSpec((k, tn), lambda i, j: (0, j)),
                  *extra_specs],
        out_specs=pl.BlockSpec((tm, tn), lambda i, j: (i, j)),
        out_shape=jax.ShapeDtypeStruct((m, n), out_dtype),
        compiler_params=_params("arbitrary", "arbitrary"),
        name=name,
    )(a, w, *extra)


def _mm_w32_kernel(a_ref, w_ref, *rest, epilogue):
    o_ref, wb_ref = rest[-2], rest[-1]

    @pl.when(pl.program_id(1) == 0)
    def _():
        wb_ref[...] = w_ref[...].astype(bf16)

    acc = jnp.dot(a_ref[...], wb_ref[...], preferred_element_type=f32)
    o_ref[...] = epilogue(acc, *rest[:-2]).astype(o_ref.dtype)


def _matmul_w32(a, w, out_dtype, *, tm, tn, epilogue=None, extra=(), extra_specs=(), name):
    m, k = a.shape
    n = w.shape[1]
    if epilogue is None:
        epilogue = lambda acc: acc
    return pl.pallas_call(
        functools.partial(_mm_w32_kernel, epilogue=epilogue),
        grid=(n // tn, m // tm),
        in_specs=[pl.BlockSpec((tm, k), lambda j, i: (i, 0)),
                  pl.BlockSpec((k, tn), lambda j, i: (0, j)),
                  *extra_specs],
        out_specs=pl.BlockSpec((tm, tn), lambda j, i: (i, j)),
        out_shape=jax.ShapeDtypeStruct((m, n), out_dtype),
        scratch_shapes=[pltpu.VMEM((k, tn), bf16)],
        compiler_params=_params("arbitrary", "arbitrary"),
        name=name,
    )(a, w, *extra)


def _gate_epilogue(acc, b_ref):
    return jax.nn.sigmoid(acc + b_ref[...])


def _resid_epilogue(acc, x_ref, g_ref):
    return x_ref[...] + g_ref[0] * acc


def _rms(x, g):
    return x * lax.rsqrt(jnp.mean(x * x, axis=-1, keepdims=True) + EPS) * g


def _dsa_prep_kernel(p_ref, gcq_ref, gckv_ref, gki_ref, bki_ref, cq_o, ckv_o, ki_o, wi_o):
    cq_o[...] = _rms(p_ref[:, OFF_CQ:OFF_CQ + A_Q_RANK], gcq_ref[...]).astype(bf16)
    ckv_o[...] = _rms(p_ref[:, OFF_CKV:OFF_CKV + A_KV_RANK], gckv_ref[...]).astype(bf16)
    ki = p_ref[:, OFF_KI:OFF_KI + IDX_DIM]
    mu = jnp.mean(ki, axis=-1, keepdims=True)
    var = jnp.mean(jnp.square(ki - mu), axis=-1, keepdims=True)
    ki_o[...] = ((ki - mu) * lax.rsqrt(var + EPS) * gki_ref[...] + bki_ref[...]).astype(bf16)
    wi_o[...] = p_ref[:, OFF_WI:OFF_WI + 128] * (IDX_HEADS ** -0.5 * IDX_DIM ** -0.5)


def _dsa_prep(proj, g_cq, g_ckv, g_ki, b_ki):
    ts = 512
    row = lambda i: (i, 0)
    fixed = lambda i: (0, 0)
    return pl.pallas_call(
        _dsa_prep_kernel,
        grid=(TOKENS // ts,),
        in_specs=[pl.BlockSpec((ts, 2048), row),
                  pl.BlockSpec((1, A_Q_RANK), fixed),
                  pl.BlockSpec((1, A_KV_RANK), fixed),
                  pl.BlockSpec((1, IDX_DIM), fixed),
                  pl.BlockSpec((1, IDX_DIM), fixed)],
        out_specs=[pl.BlockSpec((ts, A_Q_RANK), row),
                   pl.BlockSpec((ts, A_KV_RANK), row),
                   pl.BlockSpec((ts, IDX_DIM), row),
                   pl.BlockSpec((ts, 128), row)],
        out_shape=[jax.ShapeDtypeStruct((TOKENS, A_Q_RANK), bf16),
                   jax.ShapeDtypeStruct((TOKENS, A_KV_RANK), bf16),
                   jax.ShapeDtypeStruct((TOKENS, IDX_DIM), bf16),
                   jax.ShapeDtypeStruct((TOKENS, 128), f32)],
        compiler_params=_params("arbitrary"),
        name="dsa_prep",
    )(proj, g_cq, g_ckv, g_ki, b_ki)


def _qlat_kernel(q_ref, w_ref, o_ref):
    for h in range(A_HEADS):
        r = jnp.dot(q_ref[:, h * A_HEAD_DIM:(h + 1) * A_HEAD_DIM], w_ref[h], preferred_element_type=f32)
        o_ref[:, h] = r.astype(o_ref.dtype).reshape(o_ref.shape[0], Q_BLOCK, A_KV_RANK)


def _qlat(qq, wuk_t):
    tm = 512
    q_blk = IDX_HEADS * IDX_DIM // A_WIDTH
    return pl.pallas_call(
        _qlat_kernel,
        grid=(TOKENS // tm,),
        in_specs=[pl.BlockSpec((tm, A_WIDTH), lambda i: (i, q_blk)),
                  pl.BlockSpec((A_HEADS, A_HEAD_DIM, A_KV_RANK), lambda i: (0, 0, 0))],
        out_specs=pl.BlockSpec((tm // Q_BLOCK, A_HEADS, Q_BLOCK, A_KV_RANK), lambda i: (i, 0, 0, 0)),
        out_shape=jax.ShapeDtypeStruct((TOKENS // Q_BLOCK, A_HEADS, Q_BLOCK, A_KV_RANK), bf16),
        compiler_params=_params("arbitrary"),
        name="q_lat",
    )(qq, wuk_t)


def _t5_kernel(t5_ref, o_ref):
    h = pl.program_id(0)
    half = T5_BUCKETS // 2
    exact = half // 2
    qi = lax.broadcasted_iota(i32, (Q_BLOCK, NEAR), 0)
    kj = lax.broadcasted_iota(i32, (Q_BLOCK, NEAR), 1)
    rel = kj - NEAR_BACK - qi
    n = jnp.abs(rel)
    log_ratio = jnp.log(jnp.maximum(n, 1).astype(f32) / exact) / math.log(T5_MAX_DIST / exact)
    large = jnp.minimum(exact + (log_ratio * (half - exact)).astype(i32), half - 1)
    bucket = jnp.where(rel > 0, half, 0) + jnp.where(n < exact, n, large)
    acc = jnp.zeros((Q_BLOCK, NEAR), f32)
    for k in range(T5_BUCKETS):
        acc = jnp.where(bucket == k, t5_ref[k, h], acc)
    o_ref[0] = (acc - t5_ref[half - 1, h]) * (1.0 / ATTN_SCALE)


def _t5_table(t5_bias):
    return pl.pallas_call(
        _t5_kernel,
        grid=(A_HEADS,),
        in_specs=[pl.BlockSpec(memory_space=pltpu.SMEM)],
        out_specs=pl.BlockSpec((1, Q_BLOCK, NEAR), lambda h: (h, 0, 0)),
        out_shape=jax.ShapeDtypeStruct((A_HEADS, Q_BLOCK, NEAR), f32),
        compiler_params=_params("arbitrary"),
        name="t5_table",
    )(t5_bias)


DSA_COL_STEP = 512
DSA_HEAD_GROUP = 8
DSA_ROWS = DSA_HEAD_GROUP * Q_BLOCK


def _dsa_block(width, qb, t5_ref, qi_ref, kit_ref, wi_ref, ql_ref, kv_ref, wuv_ref, pt_ref, o_ref,
               key_ref, mask_ref, lg_ref, e_ref, l_ref):
    q0 = qb * Q_BLOCK
    ct_w = 256
    w = wi_ref[...]
    score_ref = lg_ref.at[0:Q_BLOCK]

    def col_tile(ct, carry):
        c0 = pl.multiple_of(ct * ct_w, ct_w)
        kt = kit_ref[:, pl.ds(c0, ct_w)]
        acc = jnp.zeros((Q_BLOCK, ct_w), f32)
        for h in range(IDX_HEADS):
            x = jnp.dot(qi_ref[:, h * IDX_DIM:(h + 1) * IDX_DIM], kt, preferred_element_type=f32)
            acc = acc + jnp.maximum(x, 0.0) * w[:, h:h + 1]
        score_ref[:, pl.ds(c0, ct_w)] = acc
        return carry

    lax.fori_loop(0, width // ct_w, col_tile, 0)

    row = lax.broadcasted_iota(i32, (Q_BLOCK, width), 0)
    col = lax.broadcasted_iota(i32, (Q_BLOCK, width), 1)
    adm = (col // CHUNK) <= ((q0 + row) // CHUNK)
    score = jnp.where(adm, score_ref[:, 0:width], -jnp.inf)
    bits = pltpu.bitcast(score, i32)
    key_ref[:, 0:width] = jnp.where(bits < 0, bits ^ 0x7FFFFFFF, bits)

    def bisect(b, thr_u):
        cand_u = thr_u | jnp.left_shift(jnp.int32(1), 31 - b)
        hit = jnp.where(key_ref[:, 0:width] >= (cand_u ^ INT_MIN), 1.0, 0.0)
        cnt = jnp.sum(hit, axis=1, keepdims=True)
        return jnp.where(cnt >= TOPK, cand_u, thr_u)

    thr_u = lax.fori_loop(0, 32, bisect, jnp.zeros((Q_BLOCK, 1), i32))
    sel = (key_ref[:, 0:width] >= (thr_u ^ INT_MIN)) & adm
    mask_ref[:, 0:width] = jnp.where(sel, 0.0, -jnp.inf)

    far_bucket = T5_BUCKETS // 2 - 1

    kv = kv_ref[0:width, :]

    def head_group(g, carry):
        h0 = g * DSA_HEAD_GROUP
        qg = ql_ref[0, pl.ds(h0, DSA_HEAD_GROUP)].reshape(DSA_ROWS, A_KV_RANK)
        lg_ref[:, 0:width] = lax.dot_general(qg, kv, (((1,), (1,)), ((), ())), preferred_element_type=f32)

        @pl.when(qb == 0)
        def _():
            for hh in range(DSA_HEAD_GROUP):
                lg_ref[hh * Q_BLOCK:(hh + 1) * Q_BLOCK, 0:Q_BLOCK] += pt_ref[h0 + hh, :, NEAR_BACK:NEAR]

        @pl.when(qb > 0)
        def _():
            w0 = pl.multiple_of(q0 - NEAR_BACK, 128)
            for hh in range(DSA_HEAD_GROUP):
                lg_ref[hh * Q_BLOCK:(hh + 1) * Q_BLOCK, pl.ds(w0, NEAR)] += pt_ref[h0 + hh]

        for hh in range(DSA_HEAD_GROUP):
            rs = slice(hh * Q_BLOCK, (hh + 1) * Q_BLOCK)
            x = lg_ref[rs, 0:width] * ATTN_SCALE + (mask_ref[:, 0:width] + t5_ref[far_bucket, h0 + hh])
            m = jnp.max(x, axis=1, keepdims=True)
            e = jnp.exp(x - m)
            l_ref[rs, :] = jnp.sum(e, axis=1, keepdims=True)
            e_ref[rs, 0:width] = e.astype(bf16)

        ol = jnp.dot(e_ref[:, 0:width], kv, preferred_element_type=f32) / l_ref[...]
        for hh in range(DSA_HEAD_GROUP):
            o = jnp.dot(ol[hh * Q_BLOCK:(hh + 1) * Q_BLOCK].astype(bf16), wuv_ref[h0 + hh],
                        preferred_element_type=f32)
            c0 = pl.multiple_of((h0 + hh) * A_HEAD_DIM, A_HEAD_DIM)
            o_ref[:, pl.ds(c0, A_HEAD_DIM)] = o.astype(o_ref.dtype)
        return carry

    lax.fori_loop(0, A_HEADS // DSA_HEAD_GROUP, head_group, 0)


def _dsa_kernel(t5_ref, qi_ref, kit_ref, wi_ref, ql_ref, kv_ref, wuv_ref, pt_ref, o_ref,
                key_ref, mask_ref, lg_ref, e_ref, l_ref):
    qb = pl.program_id(1)
    blocks_per_step = DSA_COL_STEP // Q_BLOCK
    for n in range(SEQ // DSA_COL_STEP):
        pl.when(qb // blocks_per_step == n)(functools.partial(
            _dsa_block, (n + 1) * DSA_COL_STEP, qb, t5_ref, qi_ref, kit_ref, wi_ref, ql_ref, kv_ref,
            wuv_ref, pt_ref, o_ref, key_ref, mask_ref, lg_ref, e_ref, l_ref))


def _dsa(t5_bias, qq, kidx_t, widx, qlat, ckv, wuv, ptab):
    nqb = SEQ // Q_BLOCK
    return pl.pallas_call(
        _dsa_kernel,
        grid=(BATCH, nqb),
        in_specs=[pl.BlockSpec(memory_space=pltpu.SMEM),
                  pl.BlockSpec((Q_BLOCK, IDX_HEADS * IDX_DIM), lambda b, i: (b * nqb + i, 0)),
                  pl.BlockSpec((IDX_DIM, SEQ), lambda b, i: (0, b)),
                  pl.BlockSpec((Q_BLOCK, 128), lambda b, i: (b * nqb + i, 0)),
                  pl.BlockSpec((1, A_HEADS, Q_BLOCK, A_KV_RANK), lambda b, i: (b * nqb + i, 0, 0, 0)),
                  pl.BlockSpec((SEQ, A_KV_RANK), lambda b, i: (b, 0)),
                  pl.BlockSpec((A_HEADS, A_KV_RANK, A_HEAD_DIM), lambda b, i: (0, 0, 0)),
                  pl.BlockSpec((A_HEADS, Q_BLOCK, NEAR), lambda b, i: (0, 0, 0))],
        out_specs=pl.BlockSpec((Q_BLOCK, A_WIDTH), lambda b, i: (b * nqb + i, 0)),
        out_shape=jax.ShapeDtypeStruct((TOKENS, A_WIDTH), bf16),
        scratch_shapes=[pltpu.VMEM((Q_BLOCK, SEQ), i32),
                        pltpu.VMEM((Q_BLOCK, SEQ), f32),
                        pltpu.VMEM((DSA_ROWS, SEQ), f32),
                        pltpu.VMEM((DSA_ROWS, SEQ), bf16),
                        pltpu.VMEM((DSA_ROWS, 1), f32)],
        compiler_params=_params("arbitrary", "arbitrary"),
        name="dsa_main",
    )(t5_bias, qq, kidx_t, widx, qlat, ckv, wuv, ptab)


def _ret_kernel(q_ref, k_ref, v_ref, g_ref, cc_ref, ss_ref, dm_ref, qd_ref, kd_ref, cd_ref, gr_ref,
                o_ref, qs_ref, ks_ref):
    cc = cc_ref[...]
    ss = ss_ref[...]
    q = q_ref[...]
    k = k_ref[...] * (R_QK_DIM ** -0.5)
    half = R_QK_DIM // 2
    qs_ref[...] = q * cc + pltpu.roll(q, half, 1) * ss
    ks_ref[...] = k * cc + pltpu.roll(k, half, 1) * ss
    dm = dm_ref[0]
    qd = qd_ref[0]
    kd = kd_ref[0]
    cd = cd_ref[0]
    gr = gr_ref[0]

    def chunk(n, state):
        r0 = pl.multiple_of(n * CHUNK, CHUNK)
        qn = qs_ref[pl.ds(r0, CHUNK), :]
        kn = ks_ref[pl.ds(r0, CHUNK), :]
        vn = v_ref[pl.ds(r0, CHUNK), :].astype(bf16)
        att = lax.dot_general(qn.astype(bf16), kn.astype(bf16), (((1,), (1,)), ((), ())),
                              preferred_element_type=f32) * dm
        y = jnp.dot(att.astype(bf16), vn, preferred_element_type=f32)
        y = y + jnp.dot((qn * qd).astype(bf16), state.astype(bf16), preferred_element_type=f32)
        kv = lax.dot_general((kn * kd).astype(bf16), vn, (((0,), (0,)), ((), ())),
                             preferred_element_type=f32)
        y = y * lax.rsqrt(jnp.mean(y * y, axis=-1, keepdims=True) + EPS) * gr
        gate = g_ref[pl.ds(r0, CHUNK), :]
        o_ref[pl.ds(r0, CHUNK), :] = (gate * jax.nn.sigmoid(gate) * y).astype(o_ref.dtype)
        return state * cd + kv

    lax.fori_loop(0, SEQ // CHUNK, chunk, jnp.zeros((R_QK_DIM, R_V_DIM), f32), unroll=8)


def _retention(proj, cc, ss, dmask, qdec, kdec, cdec, g_ret):
    qk_blk = OFF_RQ // R_QK_DIM
    k_blk = OFF_RK // R_QK_DIM
    v_blk = OFF_RV // R_V_DIM
    g_blk = OFF_RG // R_V_DIM
    per_head = lambda b, h: (h, 0, 0)
    return pl.pallas_call(
        _ret_kernel,
        grid=(BATCH, R_HEADS),
        in_specs=[pl.BlockSpec((SEQ, R_QK_DIM), lambda b, h: (b, qk_blk + h)),
                  pl.BlockSpec((SEQ, R_QK_DIM), lambda b, h: (b, k_blk + h)),
                  pl.BlockSpec((SEQ, R_V_DIM), lambda b, h: (b, v_blk + h)),
                  pl.BlockSpec((SEQ, R_V_DIM), lambda b, h: (b, g_blk + h)),
                  pl.BlockSpec((SEQ, R_QK_DIM), lambda b, h: (0, 0)),
                  pl.BlockSpec((SEQ, R_QK_DIM), lambda b, h: (0, 0)),
                  pl.BlockSpec((1, CHUNK, CHUNK), per_head),
                  pl.BlockSpec((1, CHUNK, R_QK_DIM), per_head),
                  pl.BlockSpec((1, CHUNK, R_QK_DIM), per_head),
                  pl.BlockSpec((1, 1, R_V_DIM), per_head),
                  pl.BlockSpec((1, 1, R_V_DIM), per_head)],
        out_specs=pl.BlockSpec((SEQ, R_V_DIM), lambda b, h: (b, h)),
        out_shape=jax.ShapeDtypeStruct((TOKENS, R_WIDTH), bf16),
        scratch_shapes=[pltpu.VMEM((SEQ, R_QK_DIM), f32), pltpu.VMEM((SEQ, R_QK_DIM), f32)],
        compiler_params=_params("arbitrary", "arbitrary"),
        name="retention",
    )(proj, proj, proj, proj, cc, ss, dmask, qdec, kdec, cdec, g_ret)


def _merge_kernel(ya_ref, yr_ref, wa_ref, wr_ref, ga_ref, gr_ref, o_ref, wab_ref, wrb_ref):
    @pl.when(pl.program_id(1) == 0)
    def _():
        wab_ref[...] = wa_ref[...].astype(bf16)
        wrb_ref[...] = wr_ref[...].astype(bf16)

    pa = jnp.dot(ya_ref[...], wab_ref[...], preferred_element_type=f32)
    pr = jnp.dot(yr_ref[...], wrb_ref[...], preferred_element_type=f32)
    o_ref[...] = (ga_ref[...].astype(f32) * pa + gr_ref[...].astype(f32) * pr).astype(o_ref.dtype)


def _merge(ya, yr, w_up, gates):
    assert A_WIDTH == R_WIDTH
    tm, tn = 1024, 512
    nj = D_MODEL // tn
    return pl.pallas_call(
        _merge_kernel,
        grid=(nj, TOKENS // tm),
        in_specs=[pl.BlockSpec((tm, A_WIDTH), lambda j, i: (i, 0)),
                  pl.BlockSpec((tm, R_WIDTH), lambda j, i: (i, 0)),
                  pl.BlockSpec((A_WIDTH, tn), lambda j, i: (0, j)),
                  pl.BlockSpec((R_WIDTH, tn), lambda j, i: (1, j)),
                  pl.BlockSpec((tm, tn), lambda j, i: (i, j)),
                  pl.BlockSpec((tm, tn), lambda j, i: (i, nj + j))],
        out_specs=pl.BlockSpec((tm, tn), lambda j, i: (i, j)),
        out_shape=jax.ShapeDtypeStruct((TOKENS, D_MODEL), bf16),
        scratch_shapes=[pltpu.VMEM((A_WIDTH, tn), bf16), pltpu.VMEM((R_WIDTH, tn), bf16)],
        compiler_params=_params("arbitrary", "arbitrary"),
        name="merge_up",
    )(ya, yr, w_up, w_up, gates, gates)


ROUTE_CAND_GROUPS = ((0, 16), (1, 8), (2, 8), (3, 8), (4, 8), (5, 8), (6, 8), (7, 8))
ROUTE_CAND_ROWS = sum(n for _, n in ROUTE_CAND_GROUPS) + 8


def _peer_route_kernel(q_ref, k_ref, s1_o, s2_o, a1_o, m2_o, thr_o, v1_ref, v2_ref, cand_ref, ec_ref):
    hq = P_QUERY_DIM // 2
    s1_all = jnp.dot(k_ref[0], q_ref[0:hq, :].astype(bf16), preferred_element_type=f32)
    s2_all = jnp.dot(k_ref[1], q_ref[hq:P_QUERY_DIM, :].astype(bf16), preferred_element_type=f32)

    def top_values(s, v_ref):
        cur = s
        for r in range(P_TOPK):
            m = jnp.max(cur, axis=0, keepdims=True)
            v_ref[r:r + 1, :] = m
            cur = jnp.where(cur == m, -jnp.inf, cur)

    for lt in range(s1_all.shape[1] // 128):
        ls = slice(lt * 128, (lt + 1) * 128)
        s1 = s1_all[:, ls]
        s2 = s2_all[:, ls]
        top_values(s1, v1_ref)
        top_values(s2, v2_ref)
        v1 = v1_ref[...]
        v2 = v2_ref[...]
        m1 = v1[0:1]
        m2 = v2[0:1]
        e1 = jnp.exp(v1 - m1)
        e2 = jnp.exp(v2 - m2)
        off = 0
        for r1, n in ROUTE_CAND_GROUPS:
            cand_ref[off:off + n, :] = v1[r1:r1 + 1] + v2[0:n]
            ec_ref[off:off + n, :] = e1[r1:r1 + 1] * e2[0:n]
            off += n
        cand_ref[off:off + 8, :] = v1[8:16] + v2[0:1]
        ec_ref[off:off + 8, :] = e1[8:16] * e2[0:1]
        cand = cand_ref[...]
        cur = cand
        thr = None
        for r in range(P_TOPK):
            thr = jnp.max(cur, axis=0, keepdims=True)
            cur = jnp.where(cur == thr, -jnp.inf, cur)
        z = jnp.sum(jnp.where(cand >= thr, ec_ref[...], 0.0), axis=0, keepdims=True)
        a1 = 0.5 * jnp.exp(s1 - m1) / z
        for r in range(PEER_NC):
            s1_o[0, r, :, ls] = s1[r * PEER_ROWS:(r + 1) * PEER_ROWS]
            a1_o[0, r, :, ls] = a1[r * PEER_ROWS:(r + 1) * PEER_ROWS]
        s2_o[0, lt] = s2
        m2_o[0, :, ls] = m2
        thr_o[0, :, ls] = thr


def _peer_route(q_t, keys):
    tl = 512
    big = lambda h, j: (h, 0, j)
    chunked = lambda h, j: (h, 0, 0, j)
    chunked_shape = jax.ShapeDtypeStruct((P_HEADS, PEER_NC, PEER_ROWS, TOKENS), f32)
    row_shape = jax.ShapeDtypeStruct((P_HEADS, 1, TOKENS), f32)
    return pl.pallas_call(
        _peer_route_kernel,
        grid=(P_HEADS, TOKENS // tl),
        in_specs=[pl.BlockSpec((P_QUERY_DIM, tl), lambda h, j: (h, j)),
                  pl.BlockSpec((2, P_NKEYS, P_QUERY_DIM // 2), lambda h, j: (0, 0, 0))],
        out_specs=[pl.BlockSpec((1, PEER_NC, PEER_ROWS, tl), chunked),
                   pl.BlockSpec((1, tl // 128, P_NKEYS, 128), lambda h, j: (h, j, 0, 0)),
                   pl.BlockSpec((1, PEER_NC, PEER_ROWS, tl), chunked),
                   pl.BlockSpec((1, 1, tl), big),
                   pl.BlockSpec((1, 1, tl), big)],
        out_shape=[chunked_shape, jax.ShapeDtypeStruct((P_HEADS, TOKENS // 128, P_NKEYS, 128), f32),
                   chunked_shape, row_shape, row_shape],
        scratch_shapes=[pltpu.VMEM((P_TOPK, 128), f32), pltpu.VMEM((P_TOPK, 128), f32),
                        pltpu.VMEM((ROUTE_CAND_ROWS, 128), f32), pltpu.VMEM((ROUTE_CAND_ROWS, 128), f32)],
        compiler_params=_params("arbitrary", "arbitrary"),
        name="peer_route",
    )(q_t, keys)


PEER_TM = 512
PEER_TE = 512
PEER_NC = P_EXPERTS // PEER_TE
PEER_ROWS = PEER_TE // P_NKEYS
PEER_KT = 64
PEER_KPIECES = 8


def _peer_kernel(ht_ref, u_ref, vt_ref, s1_ref, a1_ref, s2_ref, m2_ref, thr_ref, o_ref,
                 b2_ref, pre_ref, w_ref):
    c = pl.program_id(1)

    @pl.when(c == 0)
    def _():
        o_ref[...] = jnp.zeros_like(o_ref)
        for lt in range(PEER_TM // 128):
            b2_ref[:, lt] = jnp.exp(s2_ref[:, lt] - m2_ref[:, :, lt * 128:(lt + 1) * 128])

    @pl.when(c > 0)
    def _():
        pre = pre_ref[...]
        act2 = pre * (1.0 + lax.erf(pre * np.float32(np.sqrt(0.5))))
        coef = (w_ref[0:PEER_TE, :] * act2).astype(bf16)
        o_ref[...] += jnp.dot(vt_ref[...], coef, preferred_element_type=f32)

    @pl.when(c < PEER_NC)
    def _():
        z = pl.multiple_of(jnp.minimum(c, 0), PEER_TE)
        units = [(j, lt, kt) for j in range(PEER_ROWS) for lt in range(PEER_TM // 128)
                 for kt in range(P_NKEYS // PEER_KT)]
        per_piece = len(units) // PEER_KPIECES
        kw = D_MODEL // PEER_KPIECES
        for kq in range(PEER_KPIECES):
            part = jnp.dot(u_ref[:, kq * kw:(kq + 1) * kw], ht_ref[kq * kw:(kq + 1) * kw, :],
                           preferred_element_type=f32)
            if kq == 0:
                pre_ref[...] = part
            else:
                pre_ref[...] += part
            w_ref[pl.ds(z + PEER_TE, 8), 0:128] = part[PEER_TE - 8:PEER_TE, PEER_TM - 128:PEER_TM]
            for j, lt, kt in units[kq * per_piece:(kq + 1) * per_piece]:
                ls = slice(lt * 128, (lt + 1) * 128)
                ks = slice(kt * PEER_KT, (kt + 1) * PEER_KT)
                wj = jnp.zeros((PEER_KT, 128), f32)
                for h in range(P_HEADS):
                    sel = (s1_ref[h, 0, j:j + 1, ls] + s2_ref[h, lt, ks, :]) >= thr_ref[h, :, ls]
                    wj = wj + jnp.where(sel, b2_ref[h, lt, ks, :], 0.0) * a1_ref[h, 0, j:j + 1, ls]
                w0 = pl.multiple_of(z + (j * P_NKEYS + kt * PEER_KT), PEER_KT)
                w_ref[pl.ds(w0, PEER_KT), ls] = wj


def _peer(h_t, u, v_t, s1, a1, s2, m2, thr):
    tm, te, nc = PEER_TM, PEER_TE, PEER_NC
    s1r, a1r = s1, a1
    tok3 = lambda i, c: (0, 0, i)
    cur = lambda i, c: (0, jnp.minimum(c, nc - 1), 0, i)
    return pl.pallas_call(
        _peer_kernel,
        grid=(TOKENS // tm, nc + 1),
        in_specs=[pl.BlockSpec((D_MODEL, tm), lambda i, c: (0, i)),
                  pl.BlockSpec((te, D_MODEL), lambda i, c: (jnp.minimum(c, nc - 1), 0)),
                  pl.BlockSpec((D_MODEL, te), lambda i, c: (0, jnp.maximum(c - 1, 0))),
                  pl.BlockSpec((P_HEADS, 1, PEER_ROWS, tm), cur),
                  pl.BlockSpec((P_HEADS, 1, PEER_ROWS, tm), cur),
                  pl.BlockSpec((P_HEADS, tm // 128, P_NKEYS, 128), lambda i, c: (0, i, 0, 0)),
                  pl.BlockSpec((P_HEADS, 1, tm), tok3),
                  pl.BlockSpec((P_HEADS, 1, tm), tok3)],
        out_specs=pl.BlockSpec((D_MODEL, tm), lambda i, c: (0, i)),
        out_shape=jax.ShapeDtypeStruct((D_MODEL, TOKENS), f32),
        scratch_shapes=[pltpu.VMEM((P_HEADS, tm // 128, P_NKEYS, 128), f32),
                        pltpu.VMEM((te, tm), f32), pltpu.VMEM((te + 8, tm), f32)],
        compiler_params=_params("arbitrary", "arbitrary"),
        name="peer_experts",
    )(h_t, u, v_t, s1r, a1r, s2, m2, thr)


def _final_kernel(x_ref, yt_ref, gt_ref, g_ref, o_ref):
    x = x_ref[...] + gt_ref[0] * yt_ref[...].T
    o_ref[...] = x * lax.rsqrt(jnp.mean(x * x, axis=-1, keepdims=True) + EPS) * g_ref[...]


def _final(x1, y_t, modr, g_final):
    ts = 256
    per_b = SEQ // ts
    return pl.pallas_call(
        _final_kernel,
        grid=(TOKENS // ts,),
        in_specs=[pl.BlockSpec((ts, D_MODEL), lambda i: (i, 0)),
                  pl.BlockSpec((D_MODEL, ts), lambda i: (0, i)),
                  pl.BlockSpec((1, 1, D_MODEL), lambda i: ((i // per_b) * N_MOD + 5, 0, 0)),
                  pl.BlockSpec((1, D_MODEL), lambda i: (0, 0))],
        out_specs=pl.BlockSpec((ts, D_MODEL), lambda i: (i, 0)),
        out_shape=jax.ShapeDtypeStruct((TOKENS, D_MODEL), f32),
        compiler_params=_params("arbitrary"),
        name="final_norm",
    )(x1, y_t, modr, g_final)


def _retention_tables():
    half = R_QK_DIM // 2
    inv = 1.0 / (ROT_BASE ** jnp.linspace(0.0, 1.0, half, dtype=f32))
    ang = jnp.arange(SEQ, dtype=f32)[:, None] * inv[None, :]
    cos, sin = jnp.cos(ang), jnp.sin(ang)
    cc = jnp.concatenate([cos, cos], axis=-1)
    ss = jnp.concatenate([-sin, sin], axis=-1)
    log_g = jnp.log(1.0 - jnp.power(2.0, -5.0 - jnp.arange(R_HEADS, dtype=f32)))
    j = jnp.arange(CHUNK, dtype=f32)
    diff = j[:, None] - j[None, :]
    dmask = jnp.where(diff[None] >= 0, jnp.exp(jnp.maximum(diff, 0.0)[None] * log_g[:, None, None]), 0.0)
    kdec = jnp.exp((CHUNK - 1.0 - j)[None, :] * log_g[:, None])
    qdec = jnp.exp((j + 1.0)[None, :] * log_g[:, None])
    cdec = jnp.exp(CHUNK * log_g)
    kdec = jnp.broadcast_to(kdec[:, :, None], (R_HEADS, CHUNK, R_QK_DIM))
    qdec = jnp.broadcast_to(qdec[:, :, None], (R_HEADS, CHUNK, R_QK_DIM))
    cdec = jnp.broadcast_to(cdec[:, None, None], (R_HEADS, 1, R_V_DIM))
    return cc, ss, dmask, qdec, kdec, cdec


def kernel(x, c, w_ada, b_ada, g_mix, w_in, g_cq, g_ckv, w_uq, w_uk, w_uv, w_qi, g_ki, b_ki, t5_bias, g_ret,
           w_up, w_gate, b_gate, w_out, g_ffn, w_pq, sub_keys, u_exp, v_exp, g_final):
    x2d = x.reshape(TOKENS, D_MODEL)

    w_in_p = _w_in_layout(w_in[0].T)
    w_q_all = jnp.concatenate(
        [w_qi[0].reshape(A_Q_RANK, IDX_HEADS * IDX_DIM), w_uq[0].reshape(A_Q_RANK, A_WIDTH)], axis=1).astype(bf16)
    wuk_t = jnp.transpose(w_uk[0], (1, 2, 0)).astype(bf16)
    wuv_h = jnp.transpose(w_uv[0], (1, 0, 2)).astype(bf16)
    w_pq_t = w_pq[0].T.astype(bf16)
    keys_b = sub_keys[0].astype(bf16)
    u_b = _cast_bf16(u_exp[0])
    v_t = _cast_bf16(v_exp[0], transposed=True)

    c8 = jnp.pad(c, ((0, 8 - BATCH), (0, 0)))
    mod = _ada(c8, w_ada[0], b_ada[0].reshape(1, N_MOD * D_MODEL))[:BATCH]
    modr = mod.reshape(BATCH * N_MOD, 1, D_MODEL)

    h = _normmod(x2d, g_mix[0].reshape(1, D_MODEL), modr, 0, 1)
    proj = _matmul(h, w_in_p, f32, tm=1024, tn=512, name="in_proj")
    gates = _matmul_w32(h, w_gate[0], bf16, tm=1024, tn=512, epilogue=_gate_epilogue,
                        extra=(b_gate[0].reshape(1, 2 * D_MODEL),),
                        extra_specs=(pl.BlockSpec((1, 512), lambda j, i: (0, j)),), name="gates")

    cqn, ckvn, kidx, widx = _dsa_prep(proj, g_cq[0].reshape(1, -1), g_ckv[0].reshape(1, -1),
                                      g_ki[0].reshape(1, -1), b_ki[0].reshape(1, -1))
    qq = _matmul(cqn, w_q_all, bf16, tm=1024, tn=1024, name="q_up")
    qlat = _qlat(qq, wuk_t)
    ptab = _t5_table(t5_bias)
    y_a = _dsa(t5_bias, qq, kidx.T, widx, qlat, ckvn, wuv_h, ptab)

    cc, ss, dmask, qdec, kdec, cdec = _retention_tables()
    y_r = _retention(proj, cc, ss, dmask, qdec, kdec, cdec, g_ret[0].reshape(R_HEADS, 1, R_V_DIM))

    merged = _merge(y_a, y_r, w_up[0], gates)
    per_b = SEQ // 1024
    x1 = _matmul_w32(merged, w_out[0], f32, tm=1024, tn=512, epilogue=_resid_epilogue,
                     extra=(x2d, modr),
                     extra_specs=(pl.BlockSpec((1024, 512), lambda j, i: (i, j)),
                                  pl.BlockSpec((1, 1, 512), lambda j, i: ((i // per_b) * N_MOD + 2, 0, j))),
                     name="out_proj")

    h2_t = _normmod(x1, g_ffn[0].reshape(1, D_MODEL), modr, 3, 4, transposed=True)
    q_t = _matmul(w_pq_t, h2_t, f32, tm=1024, tn=512, name="peer_q")
    s1, s2, a1, m2, thr = _peer_route(q_t, keys_b)
    y_t = _peer(h2_t, u_b, v_t, s1, a1, s2, m2, thr)
    out = _final(x1, y_t, modr, g_final.reshape(1, D_MODEL))
    return out.reshape(BATCH, SEQ, D_MODEL)
```

```python
import functools
import math

import numpy as np
import jax
import jax.numpy as jnp
from jax import lax
from jax.experimental import pallas as pl
from jax.experimental.pallas import tpu as pltpu

f32 = jnp.float32
bf16 = jnp.bfloat16
i32 = jnp.int32

D_MODEL = 4096
BATCH = 4
SEQ = 2048
TOKENS = BATCH * SEQ
CHUNK = 64
EPS = 1e-6
N_MOD = 6
A_HEADS = 16
A_HEAD_DIM = 128
A_Q_RANK = 1024
A_KV_RANK = 512
A_WIDTH = A_HEADS * A_HEAD_DIM
IDX_HEADS = 64
IDX_DIM = 128
TOPK = 256
T5_BUCKETS = 32
T5_MAX_DIST = 128
R_HEADS = 8
R_QK_DIM = 128
R_V_DIM = 256
R_QK_WIDTH = R_HEADS * R_QK_DIM
R_WIDTH = R_HEADS * R_V_DIM
ROT_BASE = 10000.0
P_HEADS = 8
P_QUERY_DIM = 256
P_NKEYS = 128
P_TOPK = 16
P_EXPERTS = P_NKEYS * P_NKEYS

PROJ_WIDTH = 8192
OFF_CQ, OFF_CKV, OFF_KI, OFF_WI = 0, 1024, 1536, 1664
OFF_RQ, OFF_RK, OFF_RV, OFF_RG = 2048, 3072, 4096, 6144

Q_BLOCK = 128
NEAR_BACK = 128
NEAR = NEAR_BACK + Q_BLOCK
ATTN_SCALE = A_HEAD_DIM ** -0.5
VMEM_LIMIT = 56 * 1024 * 1024
INT_MIN = -2147483648


def _params(*sem, flags=None):
    return pltpu.CompilerParams(dimension_semantics=sem, vmem_limit_bytes=VMEM_LIMIT, flags=flags)


W_IN_SLAB = 64
W_IN_TILE = 512
W_IN_SLABS = W_IN_TILE // W_IN_SLAB
W_IN_USED = OFF_WI + IDX_HEADS
W_IN_PAD_TILE = W_IN_USED // W_IN_TILE
W_IN_SHIFT = (OFF_RQ - W_IN_USED) // W_IN_SLAB


def _w_in_kernel(*refs):
    o_ref = refs[-1]
    j = pl.program_id(0)
    first_pad = (W_IN_USED - W_IN_PAD_TILE * W_IN_TILE) // W_IN_SLAB
    parts = []
    for k in range(W_IN_SLABS):
        x = refs[k][...]
        if k >= first_pad:
            x = jnp.where(j == W_IN_PAD_TILE, 0.0, x)
        parts.append(x)
    o_ref[...] = jnp.concatenate(parts, axis=0).T.astype(bf16)


def _w_in_layout(w_in_t):
    def slab(k):
        def index(j):
            src = jnp.where(j <= W_IN_PAD_TILE, j * W_IN_SLABS + k, j * W_IN_SLABS + k - W_IN_SHIFT)
            return (src, 0)
        return pl.BlockSpec((W_IN_SLAB, D_MODEL), index)

    return pl.pallas_call(
        _w_in_kernel,
        grid=(PROJ_WIDTH // W_IN_TILE,),
        in_specs=[slab(k) for k in range(W_IN_SLABS)],
        out_specs=pl.BlockSpec((D_MODEL, W_IN_TILE), lambda j: (0, j)),
        out_shape=jax.ShapeDtypeStruct((D_MODEL, PROJ_WIDTH), bf16),
        compiler_params=_params("arbitrary"),
        name="w_in_layout",
    )(*([w_in_t] * W_IN_SLABS))


def _ada_kernel(c_ref, w_ref, b_ref, o_ref):
    c = c_ref[...]
    ca = (c * jax.nn.sigmoid(c)).astype(bf16)
    o_ref[...] = jnp.dot(ca, w_ref[...].astype(bf16), preferred_element_type=f32) + b_ref[...]


def _ada(c8, w, b):
    n = w.shape[1]
    tn = 512
    return pl.pallas_call(
        _ada_kernel,
        grid=(n // tn,),
        in_specs=[pl.BlockSpec((8, D_MODEL), lambda j: (0, 0)),
                  pl.BlockSpec((D_MODEL, tn), lambda j: (0, j)),
                  pl.BlockSpec((1, tn), lambda j: (0, j))],
        out_specs=pl.BlockSpec((8, tn), lambda j: (0, j)),
        out_shape=jax.ShapeDtypeStruct((8, n), f32),
        compiler_params=_params("arbitrary"),
        name="ada_mod",
    )(c8, w, b)


def _normmod_kernel(x_ref, g_ref, sh_ref, sc_ref, o_ref, *, transposed):
    x = x_ref[...]
    y = x * lax.rsqrt(jnp.mean(x * x, axis=-1, keepdims=True) + EPS) * g_ref[...]
    y = y * (1.0 + sc_ref[0]) + sh_ref[0]
    o_ref[...] = (y.T if transposed else y).astype(o_ref.dtype)


def _normmod(x2d, g, modr, shift_slot, scale_slot, transposed=False):
    ts = 256
    per_b = SEQ // ts
    if transposed:
        out_spec = pl.BlockSpec((D_MODEL, ts), lambda i: (0, i))
        out_shape = jax.ShapeDtypeStruct((D_MODEL, TOKENS), bf16)
    else:
        out_spec = pl.BlockSpec((ts, D_MODEL), lambda i: (i, 0))
        out_shape = jax.ShapeDtypeStruct((TOKENS, D_MODEL), bf16)
    return pl.pallas_call(
        functools.partial(_normmod_kernel, transposed=transposed),
        grid=(TOKENS // ts,),
        in_specs=[pl.BlockSpec((ts, D_MODEL), lambda i: (i, 0)),
                  pl.BlockSpec((1, D_MODEL), lambda i: (0, 0)),
                  pl.BlockSpec((1, 1, D_MODEL), lambda i: ((i // per_b) * N_MOD + shift_slot, 0, 0)),
                  pl.BlockSpec((1, 1, D_MODEL), lambda i: ((i // per_b) * N_MOD + scale_slot, 0, 0))],
        out_specs=out_spec,
        out_shape=out_shape,
        compiler_params=_params("arbitrary"),
        name="norm_modulate",
    )(x2d, g, modr, modr)


SIDE_CHUNK = 512


def _side_cast_specs(side, transposed, n_steps, step_of):
    r, c = side.shape
    rows = r // n_steps
    in_spec = pl.BlockSpec((rows, c), lambda g0, g1: (step_of(g0, g1), 0))
    if transposed:
        per = SIDE_CHUNK // rows
        out_spec = pl.BlockSpec((1, c, rows), lambda g0, g1: (step_of(g0, g1) // per, 0, step_of(g0, g1) % per))
        out_shape = jax.ShapeDtypeStruct((r // SIDE_CHUNK, c, SIDE_CHUNK), bf16)
    else:
        out_spec = pl.BlockSpec((rows, c), lambda g0, g1: (step_of(g0, g1), 0))
        out_shape = jax.ShapeDtypeStruct((r, c), bf16)
    return in_spec, out_spec, out_shape


def _side_cast(side_ref, so_ref, transposed):
    x = side_ref[...]
    if transposed:
        so_ref[0] = x.T.astype(bf16)
    else:
        so_ref[...] = x.astype(bf16)


def _mm_kernel(a_ref, w_ref, *rest, epilogue, n_extra, side_transposed):
    extra = rest[:n_extra]
    rest = rest[n_extra:]
    if side_transposed is None:
        (o_ref,) = rest
    else:
        side_ref, o_ref, so_ref = rest
        _side_cast(side_ref, so_ref, side_transposed)
    acc = jnp.dot(a_ref[...], w_ref[...], preferred_element_type=f32)
    o_ref[...] = epilogue(acc, *extra).astype(o_ref.dtype)


def _matmul(a, w, out_dtype, *, tm, tn, epilogue=None, extra=(), extra_specs=(), name,
            side=None, side_transposed=False):
    m, k = a.shape
    n = w.shape[1]
    if epilogue is None:
        epilogue = lambda acc: acc
    grid = (m // tm, n // tn)
    in_specs = [pl.BlockSpec((tm, k), lambda i, j: (i, 0)),
                pl.BlockSpec((k, tn), lambda i, j: (0, j)),
                *extra_specs]
    out_specs = pl.BlockSpec((tm, tn), lambda i, j: (i, j))
    out_shape = jax.ShapeDtypeStruct((m, n), out_dtype)
    operands = [a, w, *extra]
    if side is not None:
        s_in, s_out, s_shape = _side_cast_specs(side, side_transposed, grid[0] * grid[1],
                                                lambda i, j: i * grid[1] + j)
        in_specs.append(s_in)
        out_specs, out_shape = [out_specs, s_out], [out_shape, s_shape]
        operands.append(side)
    return pl.pallas_call(
        functools.partial(_mm_kernel, epilogue=epilogue, n_extra=len(extra),
                          side_transposed=side_transposed if side is not None else None),
        grid=grid,
        in_specs=in_specs,
        out_specs=out_specs,
        out_shape=out_shape,
        compiler_params=_params("arbitrary", "arbitrary"),
        name=name,
    )(*operands)


def _mm_w32_kernel(a_ref, w_ref, *rest, epilogue, n_extra, side_transposed):
    extra = rest[:n_extra]
    rest = rest[n_extra:]
    if side_transposed is None:
        o_ref, wb_ref = rest
    else:
        side_ref, o_ref, so_ref, wb_ref = rest
        _side_cast(side_ref, so_ref, side_transposed)

    @pl.when(pl.program_id(1) == 0)
    def _():
        wb_ref[...] = w_ref[...].astype(bf16)

    acc = jnp.dot(a_ref[...], wb_ref[...], preferred_element_type=f32)
    o_ref[...] = epilogue(acc, *extra).astype(o_ref.dtype)


def _matmul_w32(a, w, out_dtype, *, tm, tn, epilogue=None, extra=(), extra_specs=(), name,
                side=None, side_transposed=False):
    m, k = a.shape
    n = w.shape[1]
    if epilogue is None:
        epilogue = lambda acc: acc
    grid = (n // tn, m // tm)
    in_specs = [pl.BlockSpec((tm, k), lambda j, i: (i, 0)),
                pl.BlockSpec((k, tn), lambda j, i: (0, j)),
                *extra_specs]
    out_specs = pl.BlockSpec((tm, tn), lambda j, i: (i, j))
    out_shape = jax.ShapeDtypeStruct((m, n), out_dtype)
    operands = [a, w, *extra]
    if side is not None:
        s_in, s_out, s_shape = _side_cast_specs(side, side_transposed, grid[0] * grid[1],
                                                lambda j, i: j * grid[1] + i)
        in_specs.append(s_in)
        out_specs, out_shape = [out_specs, s_out], [out_shape, s_shape]
        operands.append(side)
    return pl.pallas_call(
        functools.partial(_mm_w32_kernel, epilogue=epilogue, n_extra=len(extra),
                          side_transposed=side_transposed if side is not None else None),
        grid=grid,
        in_specs=in_specs,
        out_specs=out_specs,
        out_shape=out_shape,
        scratch_shapes=[pltpu.VMEM((k, tn), bf16)],
        compiler_params=_params("arbitrary", "arbitrary"),
        name=name,
    )(*operands)


def _gate_epilogue(acc, b_ref):
    return jax.nn.sigmoid(acc + b_ref[...])


def _resid_epilogue(acc, x_ref, g_ref):
    return x_ref[...] + g_ref[0] * acc


def _rms(x, g):
    return x * lax.rsqrt(jnp.mean(x * x, axis=-1, keepdims=True) + EPS) * g


def _dsa_prep_kernel(p_ref, gcq_ref, gckv_ref, gki_ref, bki_ref, cq_o, ckv_o, ki_o, wi_o):
    cq_o[...] = _rms(p_ref[:, OFF_CQ:OFF_CQ + A_Q_RANK], gcq_ref[...]).astype(bf16)
    ckv_o[...] = _rms(p_ref[:, OFF_CKV:OFF_CKV + A_KV_RANK], gckv_ref[...]).astype(bf16)
    ki = p_ref[:, OFF_KI:OFF_KI + IDX_DIM]
    mu = jnp.mean(ki, axis=-1, keepdims=True)
    var = jnp.mean(jnp.square(ki - mu), axis=-1, keepdims=True)
    ki_o[...] = ((ki - mu) * lax.rsqrt(var + EPS) * gki_ref[...] + bki_ref[...]).astype(bf16)
    wi_o[...] = p_ref[:, OFF_WI:OFF_WI + 128] * (IDX_HEADS ** -0.5 * IDX_DIM ** -0.5)


def _dsa_prep(proj, g_cq, g_ckv, g_ki, b_ki):
    ts = 512
    row = lambda i: (i, 0)
    fixed = lambda i: (0, 0)
    return pl.pallas_call(
        _dsa_prep_kernel,
        grid=(TOKENS // ts,),
        in_specs=[pl.BlockSpec((ts, 2048), row),
                  pl.BlockSpec((1, A_Q_RANK), fixed),
                  pl.BlockSpec((1, A_KV_RANK), fixed),
                  pl.BlockSpec((1, IDX_DIM), fixed),
                  pl.BlockSpec((1, IDX_DIM), fixed)],
        out_specs=[pl.BlockSpec((ts, A_Q_RANK), row),
                   pl.BlockSpec((ts, A_KV_RANK), row),
                   pl.BlockSpec((ts, IDX_DIM), row),
                   pl.BlockSpec((ts, 128), row)],
        out_shape=[jax.ShapeDtypeStruct((TOKENS, A_Q_RANK), bf16),
                   jax.ShapeDtypeStruct((TOKENS, A_KV_RANK), bf16),
                   jax.ShapeDtypeStruct((TOKENS, IDX_DIM), bf16),
                   jax.ShapeDtypeStruct((TOKENS, 128), f32)],
        compiler_params=_params("arbitrary"),
        name="dsa_prep",
    )(proj, g_cq, g_ckv, g_ki, b_ki)


def _qlat_kernel(q_ref, w_ref, o_ref):
    for h in range(A_HEADS):
        r = jnp.dot(q_ref[:, h * A_HEAD_DIM:(h + 1) * A_HEAD_DIM], w_ref[h], preferred_element_type=f32)
        o_ref[:, h] = r.astype(o_ref.dtype).reshape(o_ref.shape[0], Q_BLOCK, A_KV_RANK)


def _qlat(qq, wuk_t):
    tm = 512
    q_blk = IDX_HEADS * IDX_DIM // A_WIDTH
    return pl.pallas_call(
        _qlat_kernel,
        grid=(TOKENS // tm,),
        in_specs=[pl.BlockSpec((tm, A_WIDTH), lambda i: (i, q_blk)),
                  pl.BlockSpec((A_HEADS, A_HEAD_DIM, A_KV_RANK), lambda i: (0, 0, 0))],
        out_specs=pl.BlockSpec((tm // Q_BLOCK, A_HEADS, Q_BLOCK, A_KV_RANK), lambda i: (i, 0, 0, 0)),
        out_shape=jax.ShapeDtypeStruct((TOKENS // Q_BLOCK, A_HEADS, Q_BLOCK, A_KV_RANK), bf16),
        compiler_params=_params("arbitrary"),
        name="q_lat",
    )(qq, wuk_t)


def _t5_kernel(t5_ref, o_ref):
    h = pl.program_id(0)
    half = T5_BUCKETS // 2
    exact = half // 2
    qi = lax.broadcasted_iota(i32, (Q_BLOCK, NEAR), 0)
    kj = lax.broadcasted_iota(i32, (Q_BLOCK, NEAR), 1)
    rel = kj - NEAR_BACK - qi
    n = jnp.abs(rel)
    log_ratio = jnp.log(jnp.maximum(n, 1).astype(f32) / exact) / math.log(T5_MAX_DIST / exact)
    large = jnp.minimum(exact + (log_ratio * (half - exact)).astype(i32), half - 1)
    bucket = jnp.where(rel > 0, half, 0) + jnp.where(n < exact, n, large)
    acc = jnp.zeros((Q_BLOCK, NEAR), f32)
    for k in range(T5_BUCKETS):
        acc = jnp.where(bucket == k, t5_ref[k, h], acc)
    o_ref[0] = (acc - t5_ref[half - 1, h]) * (1.0 / ATTN_SCALE)


def _t5_table(t5_bias):
    return pl.pallas_call(
        _t5_kernel,
        grid=(A_HEADS,),
        in_specs=[pl.BlockSpec(memory_space=pltpu.SMEM)],
        out_specs=pl.BlockSpec((1, Q_BLOCK, NEAR), lambda h: (h, 0, 0)),
        out_shape=jax.ShapeDtypeStruct((A_HEADS, Q_BLOCK, NEAR), f32),
        compiler_params=_params("arbitrary"),
        name="t5_table",
    )(t5_bias)


DSA_COL_STEP = 512
DSA_HEAD_GROUP = 8
DSA_ROWS = DSA_HEAD_GROUP * Q_BLOCK


def _dsa_block(width, qb, t5_ref, qi_ref, kit_ref, wi_ref, ql_ref, kv_ref, wuv_ref, pt_ref, o_ref,
               key_ref, mask_ref, lg_ref, e_ref, l_ref):
    q0 = qb * Q_BLOCK
    ct_w = 256
    w = wi_ref[...]
    score_ref = lg_ref.at[0:Q_BLOCK]

    def col_tile(ct, carry):
        c0 = pl.multiple_of(ct * ct_w, ct_w)
        kt = kit_ref[:, pl.ds(c0, ct_w)]
        acc = jnp.zeros((Q_BLOCK, ct_w), f32)
        for h in range(IDX_HEADS):
            x = jnp.dot(qi_ref[:, h * IDX_DIM:(h + 1) * IDX_DIM], kt, preferred_element_type=f32)
            acc = acc + jnp.maximum(x, 0.0) * w[:, h:h + 1]
        score_ref[:, pl.ds(c0, ct_w)] = acc
        return carry

    lax.fori_loop(0, width // ct_w, col_tile, 0)

    row = lax.broadcasted_iota(i32, (Q_BLOCK, width), 0)
    col = lax.broadcasted_iota(i32, (Q_BLOCK, width), 1)
    adm = (col // CHUNK) <= ((q0 + row) // CHUNK)
    score = jnp.where(adm, score_ref[:, 0:width], -jnp.inf)
    bits = pltpu.bitcast(score, i32)
    key_ref[:, 0:width] = jnp.where(bits < 0, bits ^ 0x7FFFFFFF, bits)

    def bisect(b, thr_u):
        cand_u = thr_u | jnp.left_shift(jnp.int32(1), 31 - b)
        hit = jnp.where(key_ref[:, 0:width] >= (cand_u ^ INT_MIN), 1.0, 0.0)
        cnt = jnp.sum(hit, axis=1, keepdims=True)
        return jnp.where(cnt >= TOPK, cand_u, thr_u)

    thr_u = lax.fori_loop(0, 32, bisect, jnp.zeros((Q_BLOCK, 1), i32))
    sel = (key_ref[:, 0:width] >= (thr_u ^ INT_MIN)) & adm
    mask_ref[:, 0:width] = jnp.where(sel, 0.0, -jnp.inf)

    far_bucket = T5_BUCKETS // 2 - 1

    kv = kv_ref[0:width, :]

    def head_group(g, carry):
        h0 = g * DSA_HEAD_GROUP
        qg = ql_ref[0, pl.ds(h0, DSA_HEAD_GROUP)].reshape(DSA_ROWS, A_KV_RANK)
        lg_ref[:, 0:width] = lax.dot_general(qg, kv, (((1,), (1,)), ((), ())), preferred_element_type=f32)

        @pl.when(qb == 0)
        def _():
            for hh in range(DSA_HEAD_GROUP):
                lg_ref[hh * Q_BLOCK:(hh + 1) * Q_BLOCK, 0:Q_BLOCK] += pt_ref[h0 + hh, :, NEAR_BACK:NEAR]

        @pl.when(qb > 0)
        def _():
            w0 = pl.multiple_of(q0 - NEAR_BACK, 128)
            for hh in range(DSA_HEAD_GROUP):
                lg_ref[hh * Q_BLOCK:(hh + 1) * Q_BLOCK, pl.ds(w0, NEAR)] += pt_ref[h0 + hh]

        for hh in range(DSA_HEAD_GROUP):
            rs = slice(hh * Q_BLOCK, (hh + 1) * Q_BLOCK)
            x = lg_ref[rs, 0:width] * ATTN_SCALE + (mask_ref[:, 0:width] + t5_ref[far_bucket, h0 + hh])
            m = jnp.max(x, axis=1, keepdims=True)
            e = jnp.exp(x - m)
            l_ref[rs, :] = jnp.sum(e, axis=1, keepdims=True)
            e_ref[rs, 0:width] = e.astype(bf16)

        ol = jnp.dot(e_ref[:, 0:width], kv, preferred_element_type=f32) / l_ref[...]
        for hh in range(DSA_HEAD_GROUP):
            o = jnp.dot(ol[hh * Q_BLOCK:(hh + 1) * Q_BLOCK].astype(bf16), wuv_ref[h0 + hh],
                        preferred_element_type=f32)
            c0 = pl.multiple_of((h0 + hh) * A_HEAD_DIM, A_HEAD_DIM)
            o_ref[:, pl.ds(c0, A_HEAD_DIM)] = o.astype(o_ref.dtype)
        return carry

    lax.fori_loop(0, A_HEADS // DSA_HEAD_GROUP, head_group, 0)


def _dsa_kernel(t5_ref, qi_ref, kit_ref, wi_ref, ql_ref, kv_ref, wuv_ref, pt_ref, o_ref,
                key_ref, mask_ref, lg_ref, e_ref, l_ref):
    qb = pl.program_id(1)
    blocks_per_step = DSA_COL_STEP // Q_BLOCK
    for n in range(SEQ // DSA_COL_STEP):
        pl.when(qb // blocks_per_step == n)(functools.partial(
            _dsa_block, (n + 1) * DSA_COL_STEP, qb, t5_ref, qi_ref, kit_ref, wi_ref, ql_ref, kv_ref,
            wuv_ref, pt_ref, o_ref, key_ref, mask_ref, lg_ref, e_ref, l_ref))


def _dsa(t5_bias, qq, kidx_t, widx, qlat, ckv, wuv, ptab):
    nqb = SEQ // Q_BLOCK
    return pl.pallas_call(
        _dsa_kernel,
        grid=(BATCH, nqb),
        in_specs=[pl.BlockSpec(memory_space=pltpu.SMEM),
                  pl.BlockSpec((Q_BLOCK, IDX_HEADS * IDX_DIM), lambda b, i: (b * nqb + i, 0)),
                  pl.BlockSpec((IDX_DIM, SEQ), lambda b, i: (0, b)),
                  pl.BlockSpec((Q_BLOCK, 128), lambda b, i: (b * nqb + i, 0)),
                  pl.BlockSpec((1, A_HEADS, Q_BLOCK, A_KV_RANK), lambda b, i: (b * nqb + i, 0, 0, 0)),
                  pl.BlockSpec((SEQ, A_KV_RANK), lambda b, i: (b, 0)),
                  pl.BlockSpec((A_HEADS, A_KV_RANK, A_HEAD_DIM), lambda b, i: (0, 0, 0)),
                  pl.BlockSpec((A_HEADS, Q_BLOCK, NEAR), lambda b, i: (0, 0, 0))],
        out_specs=pl.BlockSpec((Q_BLOCK, A_WIDTH), lambda b, i: (b * nqb + i, 0)),
        out_shape=jax.ShapeDtypeStruct((TOKENS, A_WIDTH), bf16),
        scratch_shapes=[pltpu.VMEM((Q_BLOCK, SEQ), i32),
                        pltpu.VMEM((Q_BLOCK, SEQ), f32),
                        pltpu.VMEM((DSA_ROWS, SEQ), f32),
                        pltpu.VMEM((DSA_ROWS, SEQ), bf16),
                        pltpu.VMEM((DSA_ROWS, 1), f32)],
        compiler_params=_params("arbitrary", "arbitrary"),
        name="dsa_main",
    )(t5_bias, qq, kidx_t, widx, qlat, ckv, wuv, ptab)


def _ret_kernel(q_ref, k_ref, v_ref, g_ref, cc_ref, ss_ref, dm_ref, qd_ref, kd_ref, cd_ref, gr_ref,
                o_ref, qs_ref, ks_ref):
    cc = cc_ref[...]
    ss = ss_ref[...]
    q = q_ref[...]
    k = k_ref[...] * (R_QK_DIM ** -0.5)
    half = R_QK_DIM // 2
    qs_ref[...] = q * cc + pltpu.roll(q, half, 1) * ss
    ks_ref[...] = k * cc + pltpu.roll(k, half, 1) * ss
    dm = dm_ref[0]
    qd = qd_ref[0]
    kd = kd_ref[0]
    cd = cd_ref[0]
    gr = gr_ref[0]

    def chunk(n, state):
        r0 = pl.multiple_of(n * CHUNK, CHUNK)
        qn = qs_ref[pl.ds(r0, CHUNK), :]
        kn = ks_ref[pl.ds(r0, CHUNK), :]
        vn = v_ref[pl.ds(r0, CHUNK), :].astype(bf16)
        att = lax.dot_general(qn.astype(bf16), kn.astype(bf16), (((1,), (1,)), ((), ())),
                              preferred_element_type=f32) * dm
        y = jnp.dot(att.astype(bf16), vn, preferred_element_type=f32)
        y = y + jnp.dot((qn * qd).astype(bf16), state.astype(bf16), preferred_element_type=f32)
        kv = lax.dot_general((kn * kd).astype(bf16), vn, (((0,), (0,)), ((), ())),
                             preferred_element_type=f32)
        y = y * lax.rsqrt(jnp.mean(y * y, axis=-1, keepdims=True) + EPS) * gr
        gate = g_ref[pl.ds(r0, CHUNK), :]
        o_ref[pl.ds(r0, CHUNK), :] = (gate * jax.nn.sigmoid(gate) * y).astype(o_ref.dtype)
        return state * cd + kv

    lax.fori_loop(0, SEQ // CHUNK, chunk, jnp.zeros((R_QK_DIM, R_V_DIM), f32), unroll=8)


def _retention(proj, cc, ss, dmask, qdec, kdec, cdec, g_ret):
    qk_blk = OFF_RQ // R_QK_DIM
    k_blk = OFF_RK // R_QK_DIM
    v_blk = OFF_RV // R_V_DIM
    g_blk = OFF_RG // R_V_DIM
    per_head = lambda b, h: (h, 0, 0)
    return pl.pallas_call(
        _ret_kernel,
        grid=(BATCH, R_HEADS),
        in_specs=[pl.BlockSpec((SEQ, R_QK_DIM), lambda b, h: (b, qk_blk + h)),
                  pl.BlockSpec((SEQ, R_QK_DIM), lambda b, h: (b, k_blk + h)),
                  pl.BlockSpec((SEQ, R_V_DIM), lambda b, h: (b, v_blk + h)),
                  pl.BlockSpec((SEQ, R_V_DIM), lambda b, h: (b, g_blk + h)),
                  pl.BlockSpec((SEQ, R_QK_DIM), lambda b, h: (0, 0)),
                  pl.BlockSpec((SEQ, R_QK_DIM), lambda b, h: (0, 0)),
                  pl.BlockSpec((1, CHUNK, CHUNK), per_head),
                  pl.BlockSpec((1, CHUNK, R_QK_DIM), per_head),
                  pl.BlockSpec((1, CHUNK, R_QK_DIM), per_head),
                  pl.BlockSpec((1, 1, R_V_DIM), per_head),
                  pl.BlockSpec((1, 1, R_V_DIM), per_head)],
        out_specs=pl.BlockSpec((SEQ, R_V_DIM), lambda b, h: (b, h)),
        out_shape=jax.ShapeDtypeStruct((TOKENS, R_WIDTH), bf16),
        scratch_shapes=[pltpu.VMEM((SEQ, R_QK_DIM), f32), pltpu.VMEM((SEQ, R_QK_DIM), f32)],
        compiler_params=_params("arbitrary", "arbitrary"),
        name="retention",
    )(proj, proj, proj, proj, cc, ss, dmask, qdec, kdec, cdec, g_ret)


def _merge_kernel(ya_ref, yr_ref, wa_ref, wr_ref, ga_ref, gr_ref, o_ref, wab_ref, wrb_ref):
    @pl.when(pl.program_id(1) == 0)
    def _():
        wab_ref[...] = wa_ref[...].astype(bf16)
        wrb_ref[...] = wr_ref[...].astype(bf16)

    pa = jnp.dot(ya_ref[...], wab_ref[...], preferred_element_type=f32)
    pr = jnp.dot(yr_ref[...], wrb_ref[...], preferred_element_type=f32)
    o_ref[...] = (ga_ref[...].astype(f32) * pa + gr_ref[...].astype(f32) * pr).astype(o_ref.dtype)


def _merge(ya, yr, w_up, gates):
    assert A_WIDTH == R_WIDTH
    tm, tn = 1024, 512
    nj = D_MODEL // tn
    return pl.pallas_call(
        _merge_kernel,
        grid=(nj, TOKENS // tm),
        in_specs=[pl.BlockSpec((tm, A_WIDTH), lambda j, i: (i, 0)),
                  pl.BlockSpec((tm, R_WIDTH), lambda j, i: (i, 0)),
                  pl.BlockSpec((A_WIDTH, tn), lambda j, i: (0, j)),
                  pl.BlockSpec((R_WIDTH, tn), lambda j, i: (1, j)),
                  pl.BlockSpec((tm, tn), lambda j, i: (i, j)),
                  pl.BlockSpec((tm, tn), lambda j, i: (i, nj + j))],
        out_specs=pl.BlockSpec((tm, tn), lambda j, i: (i, j)),
        out_shape=jax.ShapeDtypeStruct((TOKENS, D_MODEL), bf16),
        scratch_shapes=[pltpu.VMEM((A_WIDTH, tn), bf16), pltpu.VMEM((R_WIDTH, tn), bf16)],
        compiler_params=_params("arbitrary", "arbitrary"),
        name="merge_up",
    )(ya, yr, w_up, w_up, gates, gates)


ROUTE_CAND_GROUPS = ((0, 16), (1, 8), (2, 8), (3, 8), (4, 8), (5, 8), (6, 8), (7, 8))
ROUTE_CAND_ROWS = sum(n for _, n in ROUTE_CAND_GROUPS) + 8


def _peer_route_kernel(q_ref, k_ref, s1_o, s2_o, a1_o, m2_o, thr_o, v1_ref, v2_ref, cand_ref, ec_ref):
    hq = P_QUERY_DIM // 2
    s1_all = jnp.dot(k_ref[0], q_ref[0:hq, :].astype(bf16), preferred_element_type=f32)
    s2_all = jnp.dot(k_ref[1], q_ref[hq:P_QUERY_DIM, :].astype(bf16), preferred_element_type=f32)

    def top_values(s, v_ref):
        cur = s
        for r in range(P_TOPK):
            m = jnp.max(cur, axis=0, keepdims=True)
            v_ref[r:r + 1, :] = m
            cur = jnp.where(cur == m, -jnp.inf, cur)

    for lt in range(s1_all.shape[1] // 128):
        ls = slice(lt * 128, (lt + 1) * 128)
        s1 = s1_all[:, ls]
        s2 = s2_all[:, ls]
        top_values(s1, v1_ref)
        top_values(s2, v2_ref)
        v1 = v1_ref[...]
        v2 = v2_ref[...]
        m1 = v1[0:1]
        m2 = v2[0:1]
        e1 = jnp.exp(v1 - m1)
        e2 = jnp.exp(v2 - m2)
        off = 0
        for r1, n in ROUTE_CAND_GROUPS:
            cand_ref[off:off + n, :] = v1[r1:r1 + 1] + v2[0:n]
            ec_ref[off:off + n, :] = e1[r1:r1 + 1] * e2[0:n]
            off += n
        cand_ref[off:off + 8, :] = v1[8:16] + v2[0:1]
        ec_ref[off:off + 8, :] = e1[8:16] * e2[0:1]
        cand = cand_ref[...]
        cur = cand
        thr = None
        for r in range(P_TOPK):
            thr = jnp.max(cur, axis=0, keepdims=True)
            cur = jnp.where(cur == thr, -jnp.inf, cur)
        z = jnp.sum(jnp.where(cand >= thr, ec_ref[...], 0.0), axis=0, keepdims=True)
        a1 = 0.5 * jnp.exp(s1 - m1) / z
        for r in range(PEER_NC):
            s1_o[0, r, :, ls] = s1[r * PEER_ROWS:(r + 1) * PEER_ROWS]
            a1_o[0, r, :, ls] = a1[r * PEER_ROWS:(r + 1) * PEER_ROWS]
        s2_o[0, lt] = s2
        m2_o[0, :, ls] = m2
        thr_o[0, :, ls] = thr


def _peer_route(q_t, keys):
    tl = 512
    big = lambda h, j: (h, 0, j)
    chunked = lambda h, j: (h, 0, 0, j)
    chunked_shape = jax.ShapeDtypeStruct((P_HEADS, PEER_NC, PEER_ROWS, TOKENS), f32)
    row_shape = jax.ShapeDtypeStruct((P_HEADS, 1, TOKENS), f32)
    return pl.pallas_call(
        _peer_route_kernel,
        grid=(P_HEADS, TOKENS // tl),
        in_specs=[pl.BlockSpec((P_QUERY_DIM, tl), lambda h, j: (h, j)),
                  pl.BlockSpec((2, P_NKEYS, P_QUERY_DIM // 2), lambda h, j: (0, 0, 0))],
        out_specs=[pl.BlockSpec((1, PEER_NC, PEER_ROWS, tl), chunked),
                   pl.BlockSpec((1, tl // 128, P_NKEYS, 128), lambda h, j: (h, j, 0, 0)),
                   pl.BlockSpec((1, PEER_NC, PEER_ROWS, tl), chunked),
                   pl.BlockSpec((1, 1, tl), big),
                   pl.BlockSpec((1, 1, tl), big)],
        out_shape=[chunked_shape, jax.ShapeDtypeStruct((P_HEADS, TOKENS // 128, P_NKEYS, 128), f32),
                   chunked_shape, row_shape, row_shape],
        scratch_shapes=[pltpu.VMEM((P_TOPK, 128), f32), pltpu.VMEM((P_TOPK, 128), f32),
                        pltpu.VMEM((ROUTE_CAND_ROWS, 128), f32), pltpu.VMEM((ROUTE_CAND_ROWS, 128), f32)],
        compiler_params=_params("arbitrary", "arbitrary"),
        name="peer_route",
    )(q_t, keys)


PEER_TM = 512
PEER_TE = 512
PEER_NC = P_EXPERTS // PEER_TE
PEER_ROWS = PEER_TE // P_NKEYS
PEER_KT = 64
PEER_KPIECES = 8


def _peer_kernel(ht_ref, u_ref, vt_ref, s1_ref, a1_ref, s2_ref, m2_ref, thr_ref, o_ref,
                 b2_ref, pre_ref, w_ref):
    c = pl.program_id(1)

    @pl.when(c == 0)
    def _():
        o_ref[...] = jnp.zeros_like(o_ref)
        for lt in range(PEER_TM // 128):
            b2_ref[:, lt] = jnp.exp(s2_ref[:, lt] - m2_ref[:, :, lt * 128:(lt + 1) * 128])

    @pl.when(c > 0)
    def _():
        pre = pre_ref[...]
        act2 = pre * (1.0 + lax.erf(pre * np.float32(np.sqrt(0.5))))
        coef = (w_ref[0:PEER_TE, :] * act2).astype(bf16)
        o_ref[...] += jnp.dot(vt_ref[0], coef, preferred_element_type=f32)

    @pl.when(c < PEER_NC)
    def _():
        z = pl.multiple_of(jnp.minimum(c, 0), PEER_TE)
        units = [(j, lt, kt) for j in range(PEER_ROWS) for lt in range(PEER_TM // 128)
                 for kt in range(P_NKEYS // PEER_KT)]
        per_piece = len(units) // PEER_KPIECES
        kw = D_MODEL // PEER_KPIECES
        for kq in range(PEER_KPIECES):
            part = jnp.dot(u_ref[:, kq * kw:(kq + 1) * kw], ht_ref[kq * kw:(kq + 1) * kw, :],
                           preferred_element_type=f32)
            if kq == 0:
                pre_ref[...] = part
            else:
                pre_ref[...] += part
            w_ref[pl.ds(z + PEER_TE, 8), 0:128] = part[PEER_TE - 8:PEER_TE, PEER_TM - 128:PEER_TM]
            for j, lt, kt in units[kq * per_piece:(kq + 1) * per_piece]:
                ls = slice(lt * 128, (lt + 1) * 128)
                ks = slice(kt * PEER_KT, (kt + 1) * PEER_KT)
                wj = jnp.zeros((PEER_KT, 128), f32)
                for h in range(P_HEADS):
                    sel = (s1_ref[h, 0, j:j + 1, ls] + s2_ref[h, lt, ks, :]) >= thr_ref[h, :, ls]
                    wj = wj + jnp.where(sel, b2_ref[h, lt, ks, :], 0.0) * a1_ref[h, 0, j:j + 1, ls]
                w0 = pl.multiple_of(z + (j * P_NKEYS + kt * PEER_KT), PEER_KT)
                w_ref[pl.ds(w0, PEER_KT), ls] = wj


def _peer(h_t, u, v_t, s1, a1, s2, m2, thr):
    tm, te, nc = PEER_TM, PEER_TE, PEER_NC
    s1r, a1r = s1, a1
    assert v_t.shape == (nc, D_MODEL, te)
    tok3 = lambda i, c: (0, 0, i)
    cur = lambda i, c: (0, jnp.minimum(c, nc - 1), 0, i)
    return pl.pallas_call(
        _peer_kernel,
        grid=(TOKENS // tm, nc + 1),
        in_specs=[pl.BlockSpec((D_MODEL, tm), lambda i, c: (0, i)),
                  pl.BlockSpec((te, D_MODEL), lambda i, c: (jnp.minimum(c, nc - 1), 0)),
                  pl.BlockSpec((1, D_MODEL, te), lambda i, c: (jnp.maximum(c - 1, 0), 0, 0)),
                  pl.BlockSpec((P_HEADS, 1, PEER_ROWS, tm), cur),
                  pl.BlockSpec((P_HEADS, 1, PEER_ROWS, tm), cur),
                  pl.BlockSpec((P_HEADS, tm // 128, P_NKEYS, 128), lambda i, c: (0, i, 0, 0)),
                  pl.BlockSpec((P_HEADS, 1, tm), tok3),
                  pl.BlockSpec((P_HEADS, 1, tm), tok3)],
        out_specs=pl.BlockSpec((D_MODEL, tm), lambda i, c: (0, i)),
        out_shape=jax.ShapeDtypeStruct((D_MODEL, TOKENS), f32),
        scratch_shapes=[pltpu.VMEM((P_HEADS, tm // 128, P_NKEYS, 128), f32),
                        pltpu.VMEM((te, tm), f32), pltpu.VMEM((te + 8, tm), f32)],
        compiler_params=_params("arbitrary", "arbitrary"),
        name="peer_experts",
    )(h_t, u, v_t, s1r, a1r, s2, m2, thr)


def _final_kernel(x_ref, yt_ref, gt_ref, g_ref, o_ref):
    x = x_ref[...] + gt_ref[0] * yt_ref[...].T
    o_ref[...] = x * lax.rsqrt(jnp.mean(x * x, axis=-1, keepdims=True) + EPS) * g_ref[...]


def _final(x1, y_t, modr, g_final):
    ts = 256
    per_b = SEQ // ts
    return pl.pallas_call(
        _final_kernel,
        grid=(TOKENS // ts,),
        in_specs=[pl.BlockSpec((ts, D_MODEL), lambda i: (i, 0)),
                  pl.BlockSpec((D_MODEL, ts), lambda i: (0, i)),
                  pl.BlockSpec((1, 1, D_MODEL), lambda i: ((i // per_b) * N_MOD + 5, 0, 0)),
                  pl.BlockSpec((1, D_MODEL), lambda i: (0, 0))],
        out_specs=pl.BlockSpec((ts, D_MODEL), lambda i: (i, 0)),
        out_shape=jax.ShapeDtypeStruct((TOKENS, D_MODEL), f32),
        compiler_params=_params("arbitrary"),
        name="final_norm",
    )(x1, y_t, modr, g_final)


def _retention_tables():
    half = R_QK_DIM // 2
    inv = 1.0 / (ROT_BASE ** jnp.linspace(0.0, 1.0, half, dtype=f32))
    ang = jnp.arange(SEQ, dtype=f32)[:, None] * inv[None, :]
    cos, sin = jnp.cos(ang), jnp.sin(ang)
    cc = jnp.concatenate([cos, cos], axis=-1)
    ss = jnp.concatenate([-sin, sin], axis=-1)
    log_g = jnp.log(1.0 - jnp.power(2.0, -5.0 - jnp.arange(R_HEADS, dtype=f32)))
    j = jnp.arange(CHUNK, dtype=f32)
    diff = j[:, None] - j[None, :]
    dmask = jnp.where(diff[None] >= 0, jnp.exp(jnp.maximum(diff, 0.0)[None] * log_g[:, None, None]), 0.0)
    kdec = jnp.exp((CHUNK - 1.0 - j)[None, :] * log_g[:, None])
    qdec = jnp.exp((j + 1.0)[None, :] * log_g[:, None])
    cdec = jnp.exp(CHUNK * log_g)
    kdec = jnp.broadcast_to(kdec[:, :, None], (R_HEADS, CHUNK, R_QK_DIM))
    qdec = jnp.broadcast_to(qdec[:, :, None], (R_HEADS, CHUNK, R_QK_DIM))
    cdec = jnp.broadcast_to(cdec[:, None, None], (R_HEADS, 1, R_V_DIM))
    return cc, ss, dmask, qdec, kdec, cdec


def kernel(x, c, w_ada, b_ada, g_mix, w_in, g_cq, g_ckv, w_uq, w_uk, w_uv, w_qi, g_ki, b_ki, t5_bias, g_ret,
           w_up, w_gate, b_gate, w_out, g_ffn, w_pq, sub_keys, u_exp, v_exp, g_final):
    x2d = x.reshape(TOKENS, D_MODEL)

    w_in_p = _w_in_layout(w_in[0].T)
    w_q_all = jnp.concatenate(
        [w_qi[0].reshape(A_Q_RANK, IDX_HEADS * IDX_DIM), w_uq[0].reshape(A_Q_RANK, A_WIDTH)], axis=1).astype(bf16)
    wuk_t = jnp.transpose(w_uk[0], (1, 2, 0)).astype(bf16)
    wuv_h = jnp.transpose(w_uv[0], (1, 0, 2)).astype(bf16)
    w_pq_t = w_pq[0].T.astype(bf16)
    keys_b = sub_keys[0].astype(bf16)

    c8 = jnp.pad(c, ((0, 8 - BATCH), (0, 0)))
    mod = _ada(c8, w_ada[0], b_ada[0].reshape(1, N_MOD * D_MODEL))[:BATCH]
    modr = mod.reshape(BATCH * N_MOD, 1, D_MODEL)

    h = _normmod(x2d, g_mix[0].reshape(1, D_MODEL), modr, 0, 1)
    proj, v_t = _matmul(h, w_in_p, f32, tm=1024, tn=512, name="in_proj",
                        side=v_exp[0], side_transposed=True)
    gates, u_b = _matmul_w32(h, w_gate[0], bf16, tm=1024, tn=512, epilogue=_gate_epilogue,
                             extra=(b_gate[0].reshape(1, 2 * D_MODEL),),
                             extra_specs=(pl.BlockSpec((1, 512), lambda j, i: (0, j)),), name="gates",
                             side=u_exp[0])

    cqn, ckvn, kidx, widx = _dsa_prep(proj, g_cq[0].reshape(1, -1), g_ckv[0].reshape(1, -1),
                                      g_ki[0].reshape(1, -1), b_ki[0].reshape(1, -1))
    qq = _matmul(cqn, w_q_all, bf16, tm=1024, tn=1024, name="q_up")
    qlat = _qlat(qq, wuk_t)
    ptab = _t5_table(t5_bias)
    y_a = _dsa(t5_bias, qq, kidx.T, widx, qlat, ckvn, wuv_h, ptab)

    cc, ss, dmask, qdec, kdec, cdec = _retention_tables()
    y_r = _retention(proj, cc, ss, dmask, qdec, kdec, cdec, g_ret[0].reshape(R_HEADS, 1, R_V_DIM))

    merged = _merge(y_a, y_r, w_up[0], gates)
    per_b = SEQ // 1024
    x1 = _matmul_w32(merged, w_out[0], f32, tm=1024, tn=512, epilogue=_resid_epilogue,
                     extra=(x2d, modr),
                     extra_specs=(pl.BlockSpec((1024, 512), lambda j, i: (i, j)),
                                  pl.BlockSpec((1, 1, 512), lambda j, i: ((i // per_b) * N_MOD + 2, 0, j))),
                     name="out_proj")

    h2_t = _normmod(x1, g_ffn[0].reshape(1, D_MODEL), modr, 3, 4, transposed=True)
    q_t = _matmul(w_pq_t, h2_t, f32, tm=1024, tn=512, name="peer_q")
    s1, s2, a1, m2, thr = _peer_route(q_t, keys_b)
    y_t = _peer(h2_t, u_b, v_t, s1, a1, s2, m2, thr)
    out = _final(x1, y_t, modr, g_final.reshape(1, D_MODEL))
    return out.reshape(BATCH, SEQ, D_MODEL)
```

```python
import functools
import math

import numpy as np
import jax
import jax.numpy as jnp
from jax import lax
from jax.experimental import pallas as pl
from jax.experimental.pallas import tpu as pltpu

f32 = jnp.float32
bf16 = jnp.bfloat16
i32 = jnp.int32

D_MODEL = 4096
BATCH = 4
SEQ = 2048
TOKENS = BATCH * SEQ
CHUNK = 64
EPS = 1e-6
N_MOD = 6
A_HEADS = 16
A_HEAD_DIM = 128
A_Q_RANK = 1024
A_KV_RANK = 512
A_WIDTH = A_HEADS * A_HEAD_DIM
IDX_HEADS = 64
IDX_DIM = 128
TOPK = 256
T5_BUCKETS = 32
T5_MAX_DIST = 128
R_HEADS = 8
R_QK_DIM = 128
R_V_DIM = 256
R_QK_WIDTH = R_HEADS * R_QK_DIM
R_WIDTH = R_HEADS * R_V_DIM
ROT_BASE = 10000.0
P_HEADS = 8
P_QUERY_DIM = 256
P_NKEYS = 128
P_TOPK = 16
P_EXPERTS = P_NKEYS * P_NKEYS

PROJ_WIDTH = 8192
OFF_CQ, OFF_CKV, OFF_KI, OFF_WI = 0, 1024, 1536, 1664
OFF_RQ, OFF_RK, OFF_RV, OFF_RG = 2048, 3072, 4096, 6144

Q_BLOCK = 128
NEAR_BACK = 128
NEAR = NEAR_BACK + Q_BLOCK
ATTN_SCALE = A_HEAD_DIM ** -0.5
LOG2E = math.log2(math.e)
VMEM_LIMIT = 56 * 1024 * 1024
INT_MIN = -2147483648


def _params(*sem, flags=None):
    return pltpu.CompilerParams(dimension_semantics=sem, vmem_limit_bytes=VMEM_LIMIT, flags=flags)


W_IN_SLAB = 64
W_IN_TILE = 512
W_IN_SLABS = W_IN_TILE // W_IN_SLAB
W_IN_USED = OFF_WI + IDX_HEADS
W_IN_PAD_TILE = W_IN_USED // W_IN_TILE
W_IN_SHIFT = (OFF_RQ - W_IN_USED) // W_IN_SLAB


def _w_in_kernel(*refs):
    o_ref = refs[-1]
    j = pl.program_id(0)
    first_pad = (W_IN_USED - W_IN_PAD_TILE * W_IN_TILE) // W_IN_SLAB
    parts = []
    for k in range(W_IN_SLABS):
        x = refs[k][...]
        if k >= first_pad:
            x = jnp.where(j == W_IN_PAD_TILE, 0.0, x)
        parts.append(x)
    o_ref[...] = jnp.concatenate(parts, axis=0).T.astype(bf16)


def _w_in_layout(w_in_t):
    def slab(k):
        def index(j):
            src = jnp.where(j <= W_IN_PAD_TILE, j * W_IN_SLABS + k, j * W_IN_SLABS + k - W_IN_SHIFT)
            return (src, 0)
        return pl.BlockSpec((W_IN_SLAB, D_MODEL), index)

    return pl.pallas_call(
        _w_in_kernel,
        grid=(PROJ_WIDTH // W_IN_TILE,),
        in_specs=[slab(k) for k in range(W_IN_SLABS)],
        out_specs=pl.BlockSpec((D_MODEL, W_IN_TILE), lambda j: (0, j)),
        out_shape=jax.ShapeDtypeStruct((D_MODEL, PROJ_WIDTH), bf16),
        compiler_params=_params("arbitrary"),
        name="w_in_layout",
    )(*([w_in_t] * W_IN_SLABS))


def _ada_kernel(c_ref, w_ref, b_ref, o_ref):
    c = c_ref[...]
    ca = (c * jax.nn.sigmoid(c)).astype(bf16)
    o_ref[...] = jnp.dot(ca, w_ref[...].astype(bf16), preferred_element_type=f32) + b_ref[...]


def _ada(c8, w, b):
    n = w.shape[1]
    tn = 512
    return pl.pallas_call(
        _ada_kernel,
        grid=(n // tn,),
        in_specs=[pl.BlockSpec((8, D_MODEL), lambda j: (0, 0)),
                  pl.BlockSpec((D_MODEL, tn), lambda j: (0, j)),
                  pl.BlockSpec((1, tn), lambda j: (0, j))],
        out_specs=pl.BlockSpec((8, tn), lambda j: (0, j)),
        out_shape=jax.ShapeDtypeStruct((8, n), f32),
        compiler_params=_params("arbitrary"),
        name="ada_mod",
    )(c8, w, b)


def _normmod_kernel(x_ref, g_ref, sh_ref, sc_ref, o_ref, *, transposed):
    x = x_ref[...]
    y = x * lax.rsqrt(jnp.mean(x * x, axis=-1, keepdims=True) + EPS) * g_ref[...]
    y = y * (1.0 + sc_ref[0]) + sh_ref[0]
    o_ref[...] = (y.T if transposed else y).astype(o_ref.dtype)


def _normmod(x2d, g, modr, shift_slot, scale_slot, transposed=False):
    ts = 256
    per_b = SEQ // ts
    if transposed:
        out_spec = pl.BlockSpec((D_MODEL, ts), lambda i: (0, i))
        out_shape = jax.ShapeDtypeStruct((D_MODEL, TOKENS), bf16)
    else:
        out_spec = pl.BlockSpec((ts, D_MODEL), lambda i: (i, 0))
        out_shape = jax.ShapeDtypeStruct((TOKENS, D_MODEL), bf16)
    return pl.pallas_call(
        functools.partial(_normmod_kernel, transposed=transposed),
        grid=(TOKENS // ts,),
        in_specs=[pl.BlockSpec((ts, D_MODEL), lambda i: (i, 0)),
                  pl.BlockSpec((1, D_MODEL), lambda i: (0, 0)),
                  pl.BlockSpec((1, 1, D_MODEL), lambda i: ((i // per_b) * N_MOD + shift_slot, 0, 0)),
                  pl.BlockSpec((1, 1, D_MODEL), lambda i: ((i // per_b) * N_MOD + scale_slot, 0, 0))],
        out_specs=out_spec,
        out_shape=out_shape,
        compiler_params=_params("arbitrary"),
        name="norm_modulate",
    )(x2d, g, modr, modr)


SIDE_CHUNK = 512


def _side_cast_specs(side, transposed, n_steps, step_of):
    r, c = side.shape
    rows = r // n_steps
    in_spec = pl.BlockSpec((rows, c), lambda g0, g1: (step_of(g0, g1), 0))
    if transposed:
        per = SIDE_CHUNK // rows
        out_spec = pl.BlockSpec((1, c, rows), lambda g0, g1: (step_of(g0, g1) // per, 0, step_of(g0, g1) % per))
        out_shape = jax.ShapeDtypeStruct((r // SIDE_CHUNK, c, SIDE_CHUNK), bf16)
    else:
        out_spec = pl.BlockSpec((rows, c), lambda g0, g1: (step_of(g0, g1), 0))
        out_shape = jax.ShapeDtypeStruct((r, c), bf16)
    return in_spec, out_spec, out_shape


def _side_cast(side_ref, so_ref, transposed):
    x = side_ref[...]
    if transposed:
        so_ref[0] = x.T.astype(bf16)
    else:
        so_ref[...] = x.astype(bf16)


def _mm_kernel(a_ref, w_ref, *rest, epilogue, n_extra, side_transposed):
    extra = rest[:n_extra]
    rest = rest[n_extra:]
    if side_transposed is None:
        (o_ref,) = rest
    else:
        side_ref, o_ref, so_ref = rest
        _side_cast(side_ref, so_ref, side_transposed)
    acc = jnp.dot(a_ref[...], w_ref[...], preferred_element_type=f32)
    o_ref[...] = epilogue(acc, *extra).astype(o_ref.dtype)


def _matmul(a, w, out_dtype, *, tm, tn, epilogue=None, extra=(), extra_specs=(), name,
            side=None, side_transposed=False):
    m, k = a.shape
    n = w.shape[1]
    if epilogue is None:
        epilogue = lambda acc: acc
    grid = (m // tm, n // tn)
    in_specs = [pl.BlockSpec((tm, k), lambda i, j: (i, 0)),
                pl.BlockSpec((k, tn), lambda i, j: (0, j)),
                *extra_specs]
    out_specs = pl.BlockSpec((tm, tn), lambda i, j: (i, j))
    out_shape = jax.ShapeDtypeStruct((m, n), out_dtype)
    operands = [a, w, *extra]
    if side is not None:
        s_in, s_out, s_shape = _side_cast_specs(side, side_transposed, grid[0] * grid[1],
                                                lambda i, j: i * grid[1] + j)
        in_specs.append(s_in)
        out_specs, out_shape = [out_specs, s_out], [out_shape, s_shape]
        operands.append(side)
    return pl.pallas_call(
        functools.partial(_mm_kernel, epilogue=epilogue, n_extra=len(extra),
                          side_transposed=side_transposed if side is not None else None),
        grid=grid,
        in_specs=in_specs,
        out_specs=out_specs,
        out_shape=out_shape,
        compiler_params=_params("arbitrary", "arbitrary"),
        name=name,
    )(*operands)


def _mm_w32_kernel(a_ref, w_ref, *rest, epilogue, n_extra, side_transposed):
    extra = rest[:n_extra]
    rest = rest[n_extra:]
    if side_transposed is None:
        o_ref, wb_ref = rest
    else:
        side_ref, o_ref, so_ref, wb_ref = rest
        _side_cast(side_ref, so_ref, side_transposed)

    @pl.when(pl.program_id(1) == 0)
    def _():
        wb_ref[...] = w_ref[...].astype(bf16)

    acc = jnp.dot(a_ref[...], wb_ref[...], preferred_element_type=f32)
    o_ref[...] = epilogue(acc, *extra).astype(o_ref.dtype)


def _matmul_w32(a, w, out_dtype, *, tm, tn, epilogue=None, extra=(), extra_specs=(), name,
                side=None, side_transposed=False):
    m, k = a.shape
    n = w.shape[1]
    if epilogue is None:
        epilogue = lambda acc: acc
    grid = (n // tn, m // tm)
    in_specs = [pl.BlockSpec((tm, k), lambda j, i: (i, 0)),
                pl.BlockSpec((k, tn), lambda j, i: (0, j)),
                *extra_specs]
    out_specs = pl.BlockSpec((tm, tn), lambda j, i: (i, j))
    out_shape = jax.ShapeDtypeStruct((m, n), out_dtype)
    operands = [a, w, *extra]
    if side is not None:
        s_in, s_out, s_shape = _side_cast_specs(side, side_transposed, grid[0] * grid[1],
                                                lambda j, i: j * grid[1] + i)
        in_specs.append(s_in)
        out_specs, out_shape = [out_specs, s_out], [out_shape, s_shape]
        operands.append(side)
    return pl.pallas_call(
        functools.partial(_mm_w32_kernel, epilogue=epilogue, n_extra=len(extra),
                          side_transposed=side_transposed if side is not None else None),
        grid=grid,
        in_specs=in_specs,
        out_specs=out_specs,
        out_shape=out_shape,
        scratch_shapes=[pltpu.VMEM((k, tn), bf16)],
        compiler_params=_params("arbitrary", "arbitrary"),
        name=name,
    )(*operands)


def _gate_epilogue(acc, b_ref):
    return jax.nn.sigmoid(acc + b_ref[...])


def _resid_epilogue(acc, x_ref, g_ref):
    return x_ref[...] + g_ref[0] * acc


def _rms(x, g):
    return x * lax.rsqrt(jnp.mean(x * x, axis=-1, keepdims=True) + EPS) * g


def _dsa_prep_kernel(p_ref, gcq_ref, gckv_ref, gki_ref, bki_ref, cq_o, ckv_o, ki_o, wi_o):
    cq_o[...] = _rms(p_ref[:, OFF_CQ:OFF_CQ + A_Q_RANK], gcq_ref[...]).astype(bf16)
    ckv_o[...] = _rms(p_ref[:, OFF_CKV:OFF_CKV + A_KV_RANK], gckv_ref[...]).astype(bf16)
    ki = p_ref[:, OFF_KI:OFF_KI + IDX_DIM]
    mu = jnp.mean(ki, axis=-1, keepdims=True)
    var = jnp.mean(jnp.square(ki - mu), axis=-1, keepdims=True)
    ki_o[...] = ((ki - mu) * lax.rsqrt(var + EPS) * gki_ref[...] + bki_ref[...]).astype(bf16)
    wi_o[...] = p_ref[:, OFF_WI:OFF_WI + 128] * (IDX_HEADS ** -0.5 * IDX_DIM ** -0.5)


def _dsa_prep(proj, g_cq, g_ckv, g_ki, b_ki):
    ts = 512
    row = lambda i: (i, 0)
    fixed = lambda i: (0, 0)
    return pl.pallas_call(
        _dsa_prep_kernel,
        grid=(TOKENS // ts,),
        in_specs=[pl.BlockSpec((ts, 2048), row),
                  pl.BlockSpec((1, A_Q_RANK), fixed),
                  pl.BlockSpec((1, A_KV_RANK), fixed),
                  pl.BlockSpec((1, IDX_DIM), fixed),
                  pl.BlockSpec((1, IDX_DIM), fixed)],
        out_specs=[pl.BlockSpec((ts, A_Q_RANK), row),
                   pl.BlockSpec((ts, A_KV_RANK), row),
                   pl.BlockSpec((ts, IDX_DIM), row),
                   pl.BlockSpec((ts, 128), row)],
        out_shape=[jax.ShapeDtypeStruct((TOKENS, A_Q_RANK), bf16),
                   jax.ShapeDtypeStruct((TOKENS, A_KV_RANK), bf16),
                   jax.ShapeDtypeStruct((TOKENS, IDX_DIM), bf16),
                   jax.ShapeDtypeStruct((TOKENS, 128), f32)],
        compiler_params=_params("arbitrary"),
        name="dsa_prep",
    )(proj, g_cq, g_ckv, g_ki, b_ki)


def _qlat_kernel(q_ref, w_ref, o_ref):
    for h in range(A_HEADS):
        r = jnp.dot(q_ref[:, h * A_HEAD_DIM:(h + 1) * A_HEAD_DIM], w_ref[h], preferred_element_type=f32)
        o_ref[:, h] = r.astype(o_ref.dtype).reshape(o_ref.shape[0], Q_BLOCK, A_KV_RANK)


def _qlat(qq, wuk_t):
    tm = 512
    q_blk = IDX_HEADS * IDX_DIM // A_WIDTH
    return pl.pallas_call(
        _qlat_kernel,
        grid=(TOKENS // tm,),
        in_specs=[pl.BlockSpec((tm, A_WIDTH), lambda i: (i, q_blk)),
                  pl.BlockSpec((A_HEADS, A_HEAD_DIM, A_KV_RANK), lambda i: (0, 0, 0))],
        out_specs=pl.BlockSpec((tm // Q_BLOCK, A_HEADS, Q_BLOCK, A_KV_RANK), lambda i: (i, 0, 0, 0)),
        out_shape=jax.ShapeDtypeStruct((TOKENS // Q_BLOCK, A_HEADS, Q_BLOCK, A_KV_RANK), bf16),
        compiler_params=_params("arbitrary"),
        name="q_lat",
    )(qq, wuk_t)


def _t5_kernel(t5_ref, o_ref):
    h = pl.program_id(0)
    half = T5_BUCKETS // 2
    exact = half // 2
    qi = lax.broadcasted_iota(i32, (Q_BLOCK, NEAR), 0)
    kj = lax.broadcasted_iota(i32, (Q_BLOCK, NEAR), 1)
    rel = kj - NEAR_BACK - qi
    n = jnp.abs(rel)
    log_ratio = jnp.log(jnp.maximum(n, 1).astype(f32) / exact) / math.log(T5_MAX_DIST / exact)
    large = jnp.minimum(exact + (log_ratio * (half - exact)).astype(i32), half - 1)
    bucket = jnp.where(rel > 0, half, 0) + jnp.where(n < exact, n, large)
    acc = jnp.zeros((Q_BLOCK, NEAR), f32)
    for k in range(T5_BUCKETS):
        acc = jnp.where(bucket == k, t5_ref[k, h], acc)
    o_ref[0] = (acc - t5_ref[half - 1, h]) * (1.0 / ATTN_SCALE)


def _t5_table(t5_bias):
    return pl.pallas_call(
        _t5_kernel,
        grid=(A_HEADS,),
        in_specs=[pl.BlockSpec(memory_space=pltpu.SMEM)],
        out_specs=pl.BlockSpec((1, Q_BLOCK, NEAR), lambda h: (h, 0, 0)),
        out_shape=jax.ShapeDtypeStruct((A_HEADS, Q_BLOCK, NEAR), f32),
        compiler_params=_params("arbitrary"),
        name="t5_table",
    )(t5_bias)


DSA_COL_STEP = 512
DSA_HEAD_GROUP = 8
DSA_ROWS = DSA_HEAD_GROUP * Q_BLOCK


def _dsa_block(width, qb, t5_ref, qi_ref, kit_ref, wi_ref, ql_ref, kv_ref, wuv_ref, pt_ref, o_ref,
               key_ref, mask_ref, lg_ref, e_ref, l_ref):
    q0 = qb * Q_BLOCK
    ct_w = 256
    w = wi_ref[...]
    score_ref = lg_ref.at[0:Q_BLOCK]

    def col_tile(ct, carry):
        c0 = pl.multiple_of(ct * ct_w, ct_w)
        kt = kit_ref[:, pl.ds(c0, ct_w)]
        acc = jnp.zeros((Q_BLOCK, ct_w), f32)
        for h in range(IDX_HEADS):
            x = jnp.dot(qi_ref[:, h * IDX_DIM:(h + 1) * IDX_DIM], kt, preferred_element_type=f32)
            acc = acc + jnp.maximum(x, 0.0) * w[:, h:h + 1]
        score_ref[:, pl.ds(c0, ct_w)] = acc
        return carry

    lax.fori_loop(0, width // ct_w, col_tile, 0)

    row = lax.broadcasted_iota(i32, (Q_BLOCK, width), 0)
    col = lax.broadcasted_iota(i32, (Q_BLOCK, width), 1)
    adm = (col // CHUNK) <= ((q0 + row) // CHUNK)
    score = jnp.where(adm, score_ref[:, 0:width], -jnp.inf)
    bits = pltpu.bitcast(score, i32)
    key_ref[:, 0:width] = jnp.where(bits < 0, bits ^ 0x7FFFFFFF, bits)

    def bisect(b, thr_u):
        cand_u = thr_u | jnp.left_shift(jnp.int32(1), 31 - b)
        hit = jnp.where(key_ref[:, 0:width] >= (cand_u ^ INT_MIN), 1.0, 0.0)
        cnt = jnp.sum(hit, axis=1, keepdims=True)
        return jnp.where(cnt >= TOPK, cand_u, thr_u)

    thr_u = lax.fori_loop(0, 32, bisect, jnp.zeros((Q_BLOCK, 1), i32))
    sel = (key_ref[:, 0:width] >= (thr_u ^ INT_MIN)) & adm
    mask_ref[:, 0:width] = jnp.where(sel, 0.0, -jnp.inf)

    far_bucket = T5_BUCKETS // 2 - 1

    kv = kv_ref[0:width, :]

    def head_group(g, carry):
        h0 = g * DSA_HEAD_GROUP
        qg = ql_ref[0, pl.ds(h0, DSA_HEAD_GROUP)].reshape(DSA_ROWS, A_KV_RANK)
        lg_ref[:, 0:width] = lax.dot_general(qg, kv, (((1,), (1,)), ((), ())), preferred_element_type=f32)

        @pl.when(qb == 0)
        def _():
            for hh in range(DSA_HEAD_GROUP):
                lg_ref[hh * Q_BLOCK:(hh + 1) * Q_BLOCK, 0:Q_BLOCK] += pt_ref[h0 + hh, :, NEAR_BACK:NEAR]

        @pl.when(qb > 0)
        def _():
            w0 = pl.multiple_of(q0 - NEAR_BACK, 128)
            for hh in range(DSA_HEAD_GROUP):
                lg_ref[hh * Q_BLOCK:(hh + 1) * Q_BLOCK, pl.ds(w0, NEAR)] += pt_ref[h0 + hh]

        for hh in range(DSA_HEAD_GROUP):
            rs = slice(hh * Q_BLOCK, (hh + 1) * Q_BLOCK)
            x = lg_ref[rs, 0:width] * (ATTN_SCALE * LOG2E) + (
                mask_ref[:, 0:width] + t5_ref[far_bucket, h0 + hh] * LOG2E)
            m = jnp.max(x, axis=1, keepdims=True)
            e = jnp.exp2(x - m)
            l_ref[rs, :] = jnp.sum(e, axis=1, keepdims=True)
            e_ref[rs, 0:width] = e.astype(bf16)

        ol = jnp.dot(e_ref[:, 0:width], kv, preferred_element_type=f32) / l_ref[...]
        for hh in range(DSA_HEAD_GROUP):
            o = jnp.dot(ol[hh * Q_BLOCK:(hh + 1) * Q_BLOCK].astype(bf16), wuv_ref[h0 + hh],
                        preferred_element_type=f32)
            c0 = pl.multiple_of((h0 + hh) * A_HEAD_DIM, A_HEAD_DIM)
            o_ref[:, pl.ds(c0, A_HEAD_DIM)] = o.astype(o_ref.dtype)
        return carry

    lax.fori_loop(0, A_HEADS // DSA_HEAD_GROUP, head_group, 0)


def _dsa_kernel(t5_ref, qi_ref, kit_ref, wi_ref, ql_ref, kv_ref, wuv_ref, pt_ref, o_ref,
                key_ref, mask_ref, lg_ref, e_ref, l_ref):
    qb = pl.program_id(1)
    blocks_per_step = DSA_COL_STEP // Q_BLOCK
    for n in range(SEQ // DSA_COL_STEP):
        pl.when(qb // blocks_per_step == n)(functools.partial(
            _dsa_block, (n + 1) * DSA_COL_STEP, qb, t5_ref, qi_ref, kit_ref, wi_ref, ql_ref, kv_ref,
            wuv_ref, pt_ref, o_ref, key_ref, mask_ref, lg_ref, e_ref, l_ref))


def _dsa(t5_bias, qq, kidx_t, widx, qlat, ckv, wuv, ptab):
    nqb = SEQ // Q_BLOCK
    return pl.pallas_call(
        _dsa_kernel,
        grid=(BATCH, nqb),
        in_specs=[pl.BlockSpec(memory_space=pltpu.SMEM),
                  pl.BlockSpec((Q_BLOCK, IDX_HEADS * IDX_DIM), lambda b, i: (b * nqb + i, 0)),
                  pl.BlockSpec((IDX_DIM, SEQ), lambda b, i: (0, b)),
                  pl.BlockSpec((Q_BLOCK, 128), lambda b, i: (b * nqb + i, 0)),
                  pl.BlockSpec((1, A_HEADS, Q_BLOCK, A_KV_RANK), lambda b, i: (b * nqb + i, 0, 0, 0)),
                  pl.BlockSpec((SEQ, A_KV_RANK), lambda b, i: (b, 0)),
                  pl.BlockSpec((A_HEADS, A_KV_RANK, A_HEAD_DIM), lambda b, i: (0, 0, 0)),
                  pl.BlockSpec((A_HEADS, Q_BLOCK, NEAR), lambda b, i: (0, 0, 0))],
        out_specs=pl.BlockSpec((Q_BLOCK, A_WIDTH), lambda b, i: (b * nqb + i, 0)),
        out_shape=jax.ShapeDtypeStruct((TOKENS, A_WIDTH), bf16),
        scratch_shapes=[pltpu.VMEM((Q_BLOCK, SEQ), i32),
                        pltpu.VMEM((Q_BLOCK, SEQ), f32),
                        pltpu.VMEM((DSA_ROWS, SEQ), f32),
                        pltpu.VMEM((DSA_ROWS, SEQ), bf16),
                        pltpu.VMEM((DSA_ROWS, 1), f32)],
        compiler_params=_params("arbitrary", "arbitrary"),
        name="dsa_main",
    )(t5_bias, qq, kidx_t, widx, qlat, ckv, wuv, ptab)


def _ret_kernel(q_ref, k_ref, v_ref, g_ref, cc_ref, ss_ref, dm_ref, qd_ref, kd_ref, cd_ref, gr_ref,
                o_ref, qs_ref, ks_ref):
    cc = cc_ref[...]
    ss = ss_ref[...]
    q = q_ref[...]
    k = k_ref[...] * (R_QK_DIM ** -0.5)
    half = R_QK_DIM // 2
    qs_ref[...] = q * cc + pltpu.roll(q, half, 1) * ss
    ks_ref[...] = k * cc + pltpu.roll(k, half, 1) * ss
    dm = dm_ref[0]
    qd = qd_ref[0]
    kd = kd_ref[0]
    cd = cd_ref[0]
    gr = gr_ref[0]

    def chunk(n, state):
        r0 = pl.multiple_of(n * CHUNK, CHUNK)
        qn = qs_ref[pl.ds(r0, CHUNK), :]
        kn = ks_ref[pl.ds(r0, CHUNK), :]
        vn = v_ref[pl.ds(r0, CHUNK), :].astype(bf16)
        att = lax.dot_general(qn.astype(bf16), kn.astype(bf16), (((1,), (1,)), ((), ())),
                              preferred_element_type=f32) * dm
        y = jnp.dot(att.astype(bf16), vn, preferred_element_type=f32)
        y = y + jnp.dot((qn * qd).astype(bf16), state.astype(bf16), preferred_element_type=f32)
        kv = lax.dot_general((kn * kd).astype(bf16), vn, (((0,), (0,)), ((), ())),
                             preferred_element_type=f32)
        y = y * lax.rsqrt(jnp.mean(y * y, axis=-1, keepdims=True) + EPS) * gr
        gate = g_ref[pl.ds(r0, CHUNK), :]
        o_ref[pl.ds(r0, CHUNK), :] = (gate * jax.nn.sigmoid(gate) * y).astype(o_ref.dtype)
        return state * cd + kv

    lax.fori_loop(0, SEQ // CHUNK, chunk, jnp.zeros((R_QK_DIM, R_V_DIM), f32), unroll=8)


def _retention(proj, cc, ss, dmask, qdec, kdec, cdec, g_ret):
    qk_blk = OFF_RQ // R_QK_DIM
    k_blk = OFF_RK // R_QK_DIM
    v_blk = OFF_RV // R_V_DIM
    g_blk = OFF_RG // R_V_DIM
    per_head = lambda b, h: (h, 0, 0)
    return pl.pallas_call(
        _ret_kernel,
        grid=(BATCH, R_HEADS),
        in_specs=[pl.BlockSpec((SEQ, R_QK_DIM), lambda b, h: (b, qk_blk + h)),
                  pl.BlockSpec((SEQ, R_QK_DIM), lambda b, h: (b, k_blk + h)),
                  pl.BlockSpec((SEQ, R_V_DIM), lambda b, h: (b, v_blk + h)),
                  pl.BlockSpec((SEQ, R_V_DIM), lambda b, h: (b, g_blk + h)),
                  pl.BlockSpec((SEQ, R_QK_DIM), lambda b, h: (0, 0)),
                  pl.BlockSpec((SEQ, R_QK_DIM), lambda b, h: (0, 0)),
                  pl.BlockSpec((1, CHUNK, CHUNK), per_head),
                  pl.BlockSpec((1, CHUNK, R_QK_DIM), per_head),
                  pl.BlockSpec((1, CHUNK, R_QK_DIM), per_head),
                  pl.BlockSpec((1, 1, R_V_DIM), per_head),
                  pl.BlockSpec((1, 1, R_V_DIM), per_head)],
        out_specs=pl.BlockSpec((SEQ, R_V_DIM), lambda b, h: (b, h)),
        out_shape=jax.ShapeDtypeStruct((TOKENS, R_WIDTH), bf16),
        scratch_shapes=[pltpu.VMEM((SEQ, R_QK_DIM), f32), pltpu.VMEM((SEQ, R_QK_DIM), f32)],
        compiler_params=_params("arbitrary", "arbitrary"),
        name="retention",
    )(proj, proj, proj, proj, cc, ss, dmask, qdec, kdec, cdec, g_ret)


def _merge_kernel(ya_ref, yr_ref, wa_ref, wr_ref, ga_ref, gr_ref, o_ref, wab_ref, wrb_ref):
    @pl.when(pl.program_id(1) == 0)
    def _():
        wab_ref[...] = wa_ref[...].astype(bf16)
        wrb_ref[...] = wr_ref[...].astype(bf16)

    pa = jnp.dot(ya_ref[...], wab_ref[...], preferred_element_type=f32)
    pr = jnp.dot(yr_ref[...], wrb_ref[...], preferred_element_type=f32)
    o_ref[...] = (ga_ref[...].astype(f32) * pa + gr_ref[...].astype(f32) * pr).astype(o_ref.dtype)


def _merge(ya, yr, w_up, gates):
    assert A_WIDTH == R_WIDTH
    tm, tn = 1024, 512
    nj = D_MODEL // tn
    return pl.pallas_call(
        _merge_kernel,
        grid=(nj, TOKENS // tm),
        in_specs=[pl.BlockSpec((tm, A_WIDTH), lambda j, i: (i, 0)),
                  pl.BlockSpec((tm, R_WIDTH), lambda j, i: (i, 0)),
                  pl.BlockSpec((A_WIDTH, tn), lambda j, i: (0, j)),
                  pl.BlockSpec((R_WIDTH, tn), lambda j, i: (1, j)),
                  pl.BlockSpec((tm, tn), lambda j, i: (i, j)),
                  pl.BlockSpec((tm, tn), lambda j, i: (i, nj + j))],
        out_specs=pl.BlockSpec((tm, tn), lambda j, i: (i, j)),
        out_shape=jax.ShapeDtypeStruct((TOKENS, D_MODEL), bf16),
        scratch_shapes=[pltpu.VMEM((A_WIDTH, tn), bf16), pltpu.VMEM((R_WIDTH, tn), bf16)],
        compiler_params=_params("arbitrary", "arbitrary"),
        name="merge_up",
    )(ya, yr, w_up, w_up, gates, gates)


ROUTE_CAND_GROUPS = ((0, 16), (1, 8), (2, 8), (3, 8), (4, 8), (5, 8), (6, 8), (7, 8))
ROUTE_CAND_ROWS = sum(n for _, n in ROUTE_CAND_GROUPS) + 8


def _peer_route_kernel(q_ref, k_ref, s1_o, s2_o, a1_o, m2_o, thr_o, v1_ref, v2_ref, cand_ref, ec_ref):
    hq = P_QUERY_DIM // 2
    s1_all = jnp.dot(k_ref[0], q_ref[0:hq, :].astype(bf16), preferred_element_type=f32)
    s2_all = jnp.dot(k_ref[1], q_ref[hq:P_QUERY_DIM, :].astype(bf16), preferred_element_type=f32)

    def top_values(s, v_ref):
        cur = s
        for r in range(P_TOPK):
            m = jnp.max(cur, axis=0, keepdims=True)
            v_ref[r:r + 1, :] = m
            cur = jnp.where(cur == m, -jnp.inf, cur)

    for lt in range(s1_all.shape[1] // 128):
        ls = slice(lt * 128, (lt + 1) * 128)
        s1 = s1_all[:, ls]
        s2 = s2_all[:, ls]
        top_values(s1, v1_ref)
        top_values(s2, v2_ref)
        v1 = v1_ref[...]
        v2 = v2_ref[...]
        m1 = v1[0:1]
        m2 = v2[0:1]
        e1 = jnp.exp(v1 - m1)
        e2 = jnp.exp(v2 - m2)
        off = 0
        for r1, n in ROUTE_CAND_GROUPS:
            cand_ref[off:off + n, :] = v1[r1:r1 + 1] + v2[0:n]
            ec_ref[off:off + n, :] = e1[r1:r1 + 1] * e2[0:n]
            off += n
        cand_ref[off:off + 8, :] = v1[8:16] + v2[0:1]
        ec_ref[off:off + 8, :] = e1[8:16] * e2[0:1]
        cand = cand_ref[...]
        cur = cand
        thr = None
        for r in range(P_TOPK):
            thr = jnp.max(cur, axis=0, keepdims=True)
            cur = jnp.where(cur == thr, -jnp.inf, cur)
        z = jnp.sum(jnp.where(cand >= thr, ec_ref[...], 0.0), axis=0, keepdims=True)
        a1 = 0.5 * jnp.exp(s1 - m1) / z
        for r in range(PEER_NC):
            s1_o[0, r, :, ls] = s1[r * PEER_ROWS:(r + 1) * PEER_ROWS]
            a1_o[0, r, :, ls] = a1[r * PEER_ROWS:(r + 1) * PEER_ROWS]
        s2_o[0, lt] = s2
        m2_o[0, :, ls] = m2
        thr_o[0, :, ls] = thr


def _peer_route(q_t, keys):
    tl = 512
    big = lambda h, j: (h, 0, j)
    chunked = lambda h, j: (h, 0, 0, j)
    chunked_shape = jax.ShapeDtypeStruct((P_HEADS, PEER_NC, PEER_ROWS, TOKENS), f32)
    row_shape = jax.ShapeDtypeStruct((P_HEADS, 1, TOKENS), f32)
    return pl.pallas_call(
        _peer_route_kernel,
        grid=(P_HEADS, TOKENS // tl),
        in_specs=[pl.BlockSpec((P_QUERY_DIM, tl), lambda h, j: (h, j)),
                  pl.BlockSpec((2, P_NKEYS, P_QUERY_DIM // 2), lambda h, j: (0, 0, 0))],
        out_specs=[pl.BlockSpec((1, PEER_NC, PEER_ROWS, tl), chunked),
                   pl.BlockSpec((1, tl // 128, P_NKEYS, 128), lambda h, j: (h, j, 0, 0)),
                   pl.BlockSpec((1, PEER_NC, PEER_ROWS, tl), chunked),
                   pl.BlockSpec((1, 1, tl), big),
                   pl.BlockSpec((1, 1, tl), big)],
        out_shape=[chunked_shape, jax.ShapeDtypeStruct((P_HEADS, TOKENS // 128, P_NKEYS, 128), f32),
                   chunked_shape, row_shape, row_shape],
        scratch_shapes=[pltpu.VMEM((P_TOPK, 128), f32), pltpu.VMEM((P_TOPK, 128), f32),
                        pltpu.VMEM((ROUTE_CAND_ROWS, 128), f32), pltpu.VMEM((ROUTE_CAND_ROWS, 128), f32)],
        compiler_params=_params("arbitrary", "arbitrary"),
        name="peer_route",
    )(q_t, keys)


PEER_TM = 512
PEER_TE = 512
PEER_NC = P_EXPERTS // PEER_TE
PEER_ROWS = PEER_TE // P_NKEYS
PEER_KT = 64
PEER_KPIECES = 8


def _peer_kernel(ht_ref, u_ref, vt_ref, s1_ref, a1_ref, s2_ref, m2_ref, thr_ref, o_ref,
                 b2_ref, pre_ref, w_ref):
    c = pl.program_id(1)

    @pl.when(c == 0)
    def _():
        o_ref[...] = jnp.zeros_like(o_ref)
        for lt in range(PEER_TM // 128):
            b2_ref[:, lt] = jnp.exp(s2_ref[:, lt] - m2_ref[:, :, lt * 128:(lt + 1) * 128])

    @pl.when(c > 0)
    def _():
        pre = pre_ref[...]
        act2 = pre * (1.0 + lax.erf(pre * np.float32(np.sqrt(0.5))))
        coef = (w_ref[0:PEER_TE, :] * act2).astype(bf16)
        o_ref[...] += jnp.dot(vt_ref[0], coef, preferred_element_type=f32)

    @pl.when(c < PEER_NC)
    def _():
        z = pl.multiple_of(jnp.minimum(c, 0), PEER_TE)
        units = [(j, lt, kt) for j in range(PEER_ROWS) for lt in range(PEER_TM // 128)
                 for kt in range(P_NKEYS // PEER_KT)]
        per_piece = len(units) // PEER_KPIECES
        kw = D_MODEL // PEER_KPIECES
        for kq in range(PEER_KPIECES):
            part = jnp.dot(u_ref[:, kq * kw:(kq + 1) * kw], ht_ref[kq * kw:(kq + 1) * kw, :],
                           preferred_element_type=f32)
            if kq == 0:
                pre_ref[...] = part
            else:
                pre_ref[...] += part
            w_ref[pl.ds(z + PEER_TE, 8), 0:128] = part[PEER_TE - 8:PEER_TE, PEER_TM - 128:PEER_TM]
            for j, lt, kt in units[kq * per_piece:(kq + 1) * per_piece]:
                ls = slice(lt * 128, (lt + 1) * 128)
                ks = slice(kt * PEER_KT, (kt + 1) * PEER_KT)
                wj = jnp.zeros((PEER_KT, 128), f32)
                for h in range(P_HEADS):
                    sel = (s1_ref[h, 0, j:j + 1, ls] + s2_ref[h, lt, ks, :]) >= thr_ref[h, :, ls]
                    wj = wj + jnp.where(sel, b2_ref[h, lt, ks, :], 0.0) * a1_ref[h, 0, j:j + 1, ls]
                w0 = pl.multiple_of(z + (j * P_NKEYS + kt * PEER_KT), PEER_KT)
                w_ref[pl.ds(w0, PEER_KT), ls] = wj


def _peer(h_t, u, v_t, s1, a1, s2, m2, thr):
    tm, te, nc = PEER_TM, PEER_TE, PEER_NC
    s1r, a1r = s1, a1
    assert v_t.shape == (nc, D_MODEL, te)
    tok3 = lambda i, c: (0, 0, i)
    cur = lambda i, c: (0, jnp.minimum(c, nc - 1), 0, i)
    return pl.pallas_call(
        _peer_kernel,
        grid=(TOKENS // tm, nc + 1),
        in_specs=[pl.BlockSpec((D_MODEL, tm), lambda i, c: (0, i)),
                  pl.BlockSpec((te, D_MODEL), lambda i, c: (jnp.minimum(c, nc - 1), 0)),
                  pl.BlockSpec((1, D_MODEL, te), lambda i, c: (jnp.maximum(c - 1, 0), 0, 0)),
                  pl.BlockSpec((P_HEADS, 1, PEER_ROWS, tm), cur),
                  pl.BlockSpec((P_HEADS, 1, PEER_ROWS, tm), cur),
                  pl.BlockSpec((P_HEADS, tm // 128, P_NKEYS, 128), lambda i, c: (0, i, 0, 0)),
                  pl.BlockSpec((P_HEADS, 1, tm), tok3),
                  pl.BlockSpec((P_HEADS, 1, tm), tok3)],
        out_specs=pl.BlockSpec((D_MODEL, tm), lambda i, c: (0, i)),
        out_shape=jax.ShapeDtypeStruct((D_MODEL, TOKENS), f32),
        scratch_shapes=[pltpu.VMEM((P_HEADS, tm // 128, P_NKEYS, 128), f32),
                        pltpu.VMEM((te, tm), f32), pltpu.VMEM((te + 8, tm), f32)],
        compiler_params=_params("arbitrary", "arbitrary"),
        name="peer_experts",
    )(h_t, u, v_t, s1r, a1r, s2, m2, thr)


def _final_kernel(x_ref, yt_ref, gt_ref, g_ref, o_ref):
    x = x_ref[...] + gt_ref[0] * yt_ref[...].T
    o_ref[...] = x * lax.rsqrt(jnp.mean(x * x, axis=-1, keepdims=True) + EPS) * g_ref[...]


def _final(x1, y_t, modr, g_final):
    ts = 256
    per_b = SEQ // ts
    return pl.pallas_call(
        _final_kernel,
        grid=(TOKENS // ts,),
        in_specs=[pl.BlockSpec((ts, D_MODEL), lambda i: (i, 0)),
                  pl.BlockSpec((D_MODEL, ts), lambda i: (0, i)),
                  pl.BlockSpec((1, 1, D_MODEL), lambda i: ((i // per_b) * N_MOD + 5, 0, 0)),
                  pl.BlockSpec((1, D_MODEL), lambda i: (0, 0))],
        out_specs=pl.BlockSpec((ts, D_MODEL), lambda i: (i, 0)),
        out_shape=jax.ShapeDtypeStruct((TOKENS, D_MODEL), f32),
        compiler_params=_params("arbitrary"),
        name="final_norm",
    )(x1, y_t, modr, g_final)


def _retention_tables():
    half = R_QK_DIM // 2
    inv = 1.0 / (ROT_BASE ** jnp.linspace(0.0, 1.0, half, dtype=f32))
    ang = jnp.arange(SEQ, dtype=f32)[:, None] * inv[None, :]
    cos, sin = jnp.cos(ang), jnp.sin(ang)
    cc = jnp.concatenate([cos, cos], axis=-1)
    ss = jnp.concatenate([-sin, sin], axis=-1)
    log_g = jnp.log(1.0 - jnp.power(2.0, -5.0 - jnp.arange(R_HEADS, dtype=f32)))
    j = jnp.arange(CHUNK, dtype=f32)
    diff = j[:, None] - j[None, :]
    dmask = jnp.where(diff[None] >= 0, jnp.exp(jnp.maximum(diff, 0.0)[None] * log_g[:, None, None]), 0.0)
    kdec = jnp.exp((CHUNK - 1.0 - j)[None, :] * log_g[:, None])
    qdec = jnp.exp((j + 1.0)[None, :] * log_g[:, None])
    cdec = jnp.exp(CHUNK * log_g)
    kdec = jnp.broadcast_to(kdec[:, :, None], (R_HEADS, CHUNK, R_QK_DIM))
    qdec = jnp.broadcast_to(qdec[:, :, None], (R_HEADS, CHUNK, R_QK_DIM))
    cdec = jnp.broadcast_to(cdec[:, None, None], (R_HEADS, 1, R_V_DIM))
    return cc, ss, dmask, qdec, kdec, cdec


def kernel(x, c, w_ada, b_ada, g_mix, w_in, g_cq, g_ckv, w_uq, w_uk, w_uv, w_qi, g_ki, b_ki, t5_bias, g_ret,
           w_up, w_gate, b_gate, w_out, g_ffn, w_pq, sub_keys, u_exp, v_exp, g_final):
    x2d = x.reshape(TOKENS, D_MODEL)

    w_in_p = _w_in_layout(w_in[0].T)
    w_q_all = jnp.concatenate(
        [w_qi[0].reshape(A_Q_RANK, IDX_HEADS * IDX_DIM), w_uq[0].reshape(A_Q_RANK, A_WIDTH)], axis=1).astype(bf16)
    wuk_t = jnp.transpose(w_uk[0], (1, 2, 0)).astype(bf16)
    wuv_h = jnp.transpose(w_uv[0], (1, 0, 2)).astype(bf16)
    w_pq_t = w_pq[0].T.astype(bf16)
    keys_b = sub_keys[0].astype(bf16)

    c8 = jnp.pad(c, ((0, 8 - BATCH), (0, 0)))
    mod = _ada(c8, w_ada[0], b_ada[0].reshape(1, N_MOD * D_MODEL))[:BATCH]
    modr = mod.reshape(BATCH * N_MOD, 1, D_MODEL)

    h = _normmod(x2d, g_mix[0].reshape(1, D_MODEL), modr, 0, 1)
    proj, v_t = _matmul(h, w_in_p, f32, tm=1024, tn=512, name="in_proj",
                        side=v_exp[0], side_transposed=True)
    gates, u_b = _matmul_w32(h, w_gate[0], bf16, tm=1024, tn=512, epilogue=_gate_epilogue,
                             extra=(b_gate[0].reshape(1, 2 * D_MODEL),),
                             extra_specs=(pl.BlockSpec((1, 512), lambda j, i: (0, j)),), name="gates",
                             side=u_exp[0])

    cqn, ckvn, kidx, widx = _dsa_prep(proj, g_cq[0].reshape(1, -1), g_ckv[0].reshape(1, -1),
                                      g_ki[0].reshape(1, -1), b_ki[0].reshape(1, -1))
    qq = _matmul(cqn, w_q_all, bf16, tm=2048, tn=1024, name="q_up")
    qlat = _qlat(qq, wuk_t)
    ptab = _t5_table(t5_bias)
    y_a = _dsa(t5_bias, qq, kidx.T, widx, qlat, ckvn, wuv_h, ptab)

    cc, ss, dmask, qdec, kdec, cdec = _retention_tables()
    y_r = _retention(proj, cc, ss, dmask, qdec, kdec, cdec, g_ret[0].reshape(R_HEADS, 1, R_V_DIM))

    merged = _merge(y_a, y_r, w_up[0], gates)
    per_b = SEQ // 1024
    x1 = _matmul_w32(merged, w_out[0], f32, tm=1024, tn=512, epilogue=_resid_epilogue,
                     extra=(x2d, modr),
                     extra_specs=(pl.BlockSpec((1024, 512), lambda j, i: (i, j)),
                                  pl.BlockSpec((1, 1, 512), lambda j, i: ((i // per_b) * N_MOD + 2, 0, j))),
                     name="out_proj")

    h2_t = _normmod(x1, g_ffn[0].reshape(1, D_MODEL), modr, 3, 4, transposed=True)
    q_t = _matmul(w_pq_t, h2_t, f32, tm=1024, tn=512, name="peer_q")
    s1, s2, a1, m2, thr = _peer_route(q_t, keys_b)
    y_t = _peer(h2_t, u_b, v_t, s1, a1, s2, m2, thr)
    out = _final(x1, y_t, modr, g_final.reshape(1, D_MODEL))
    return out.reshape(BATCH, SEQ, D_MODEL)
```

```python
import functools
import math

import numpy as np
import jax
import jax.numpy as jnp
from jax import lax
from jax.experimental import pallas as pl
from jax.experimental.pallas import tpu as pltpu

f32 = jnp.float32
bf16 = jnp.bfloat16
i32 = jnp.int32

D_MODEL = 4096
BATCH = 4
SEQ = 2048
TOKENS = BATCH * SEQ
CHUNK = 64
EPS = 1e-6
N_MOD = 6
A_HEADS = 16
A_HEAD_DIM = 128
A_Q_RANK = 1024
A_KV_RANK = 512
A_WIDTH = A_HEADS * A_HEAD_DIM
IDX_HEADS = 64
IDX_DIM = 128
TOPK = 256
T5_BUCKETS = 32
T5_MAX_DIST = 128
R_HEADS = 8
R_QK_DIM = 128
R_V_DIM = 256
R_QK_WIDTH = R_HEADS * R_QK_DIM
R_WIDTH = R_HEADS * R_V_DIM
ROT_BASE = 10000.0
P_HEADS = 8
P_QUERY_DIM = 256
P_NKEYS = 128
P_TOPK = 16
P_EXPERTS = P_NKEYS * P_NKEYS

PROJ_WIDTH = 8192
OFF_CQ, OFF_CKV, OFF_KI, OFF_WI = 0, 1024, 1536, 1664
OFF_RQ, OFF_RK, OFF_RV, OFF_RG = 2048, 3072, 4096, 6144

Q_BLOCK = 128
NEAR_BACK = 128
NEAR = NEAR_BACK + Q_BLOCK
ATTN_SCALE = A_HEAD_DIM ** -0.5
LOG2E = math.log2(math.e)
VMEM_LIMIT = 56 * 1024 * 1024
INT_MIN = -2147483648


def _params(*sem, flags=None):
    return pltpu.CompilerParams(dimension_semantics=sem, vmem_limit_bytes=VMEM_LIMIT, flags=flags)


W_IN_SLAB = 64
W_IN_TILE = 512
W_IN_SLABS = W_IN_TILE // W_IN_SLAB
W_IN_USED = OFF_WI + IDX_HEADS
W_IN_PAD_TILE = W_IN_USED // W_IN_TILE
W_IN_SHIFT = (OFF_RQ - W_IN_USED) // W_IN_SLAB


def _w_in_kernel(*refs):
    o_ref = refs[-1]
    j = pl.program_id(0)
    first_pad = (W_IN_USED - W_IN_PAD_TILE * W_IN_TILE) // W_IN_SLAB
    parts = []
    for k in range(W_IN_SLABS):
        x = refs[k][...]
        if k >= first_pad:
            x = jnp.where(j == W_IN_PAD_TILE, 0.0, x)
        parts.append(x)
    o_ref[...] = jnp.concatenate(parts, axis=0).T.astype(bf16)


def _w_in_layout(w_in_t):
    def slab(k):
        def index(j):
            src = jnp.where(j <= W_IN_PAD_TILE, j * W_IN_SLABS + k, j * W_IN_SLABS + k - W_IN_SHIFT)
            return (src, 0)
        return pl.BlockSpec((W_IN_SLAB, D_MODEL), index)

    return pl.pallas_call(
        _w_in_kernel,
        grid=(PROJ_WIDTH // W_IN_TILE,),
        in_specs=[slab(k) for k in range(W_IN_SLABS)],
        out_specs=pl.BlockSpec((D_MODEL, W_IN_TILE), lambda j: (0, j)),
        out_shape=jax.ShapeDtypeStruct((D_MODEL, PROJ_WIDTH), bf16),
        compiler_params=_params("arbitrary"),
        name="w_in_layout",
    )(*([w_in_t] * W_IN_SLABS))


def _ada_kernel(c_ref, w_ref, b_ref, o_ref):
    c = c_ref[...]
    ca = (c * jax.nn.sigmoid(c)).astype(bf16)
    o_ref[...] = jnp.dot(ca, w_ref[...].astype(bf16), preferred_element_type=f32) + b_ref[...]


def _ada(c8, w, b):
    n = w.shape[1]
    tn = 1024
    return pl.pallas_call(
        _ada_kernel,
        grid=(n // tn,),
        in_specs=[pl.BlockSpec((8, D_MODEL), lambda j: (0, 0)),
                  pl.BlockSpec((D_MODEL, tn), lambda j: (0, j)),
                  pl.BlockSpec((1, tn), lambda j: (0, j))],
        out_specs=pl.BlockSpec((8, tn), lambda j: (0, j)),
        out_shape=jax.ShapeDtypeStruct((8, n), f32),
        compiler_params=_params("arbitrary"),
        name="ada_mod",
    )(c8, w, b)


def _normmod_kernel(x_ref, g_ref, sh_ref, sc_ref, o_ref, *, transposed):
    x = x_ref[...]
    y = x * lax.rsqrt(jnp.mean(x * x, axis=-1, keepdims=True) + EPS) * g_ref[...]
    y = y * (1.0 + sc_ref[0]) + sh_ref[0]
    o_ref[...] = (y.T if transposed else y).astype(o_ref.dtype)


def _normmod(x2d, g, modr, shift_slot, scale_slot, transposed=False):
    ts = 256
    per_b = SEQ // ts
    if transposed:
        out_spec = pl.BlockSpec((D_MODEL, ts), lambda i: (0, i))
        out_shape = jax.ShapeDtypeStruct((D_MODEL, TOKENS), bf16)
    else:
        out_spec = pl.BlockSpec((ts, D_MODEL), lambda i: (i, 0))
        out_shape = jax.ShapeDtypeStruct((TOKENS, D_MODEL), bf16)
    return pl.pallas_call(
        functools.partial(_normmod_kernel, transposed=transposed),
        grid=(TOKENS // ts,),
        in_specs=[pl.BlockSpec((ts, D_MODEL), lambda i: (i, 0)),
                  pl.BlockSpec((1, D_MODEL), lambda i: (0, 0)),
                  pl.BlockSpec((1, 1, D_MODEL), lambda i: ((i // per_b) * N_MOD + shift_slot, 0, 0)),
                  pl.BlockSpec((1, 1, D_MODEL), lambda i: ((i // per_b) * N_MOD + scale_slot, 0, 0))],
        out_specs=out_spec,
        out_shape=out_shape,
        compiler_params=_params("arbitrary"),
        name="norm_modulate",
    )(x2d, g, modr, modr)


SIDE_CHUNK = 512


def _side_cast_specs(side, transposed, n_steps, step_of):
    r, c = side.shape
    rows = r // n_steps
    in_spec = pl.BlockSpec((rows, c), lambda g0, g1: (step_of(g0, g1), 0))
    if transposed:
        per = SIDE_CHUNK // rows
        out_spec = pl.BlockSpec((1, c, rows), lambda g0, g1: (step_of(g0, g1) // per, 0, step_of(g0, g1) % per))
        out_shape = jax.ShapeDtypeStruct((r // SIDE_CHUNK, c, SIDE_CHUNK), bf16)
    else:
        out_spec = pl.BlockSpec((rows, c), lambda g0, g1: (step_of(g0, g1), 0))
        out_shape = jax.ShapeDtypeStruct((r, c), bf16)
    return in_spec, out_spec, out_shape


def _side_cast(side_ref, so_ref, transposed):
    x = side_ref[...]
    if transposed:
        so_ref[0] = x.T.astype(bf16)
    else:
        so_ref[...] = x.astype(bf16)


def _mm_kernel(a_ref, w_ref, *rest, epilogue, n_extra, side_transposed):
    extra = rest[:n_extra]
    rest = rest[n_extra:]
    if side_transposed is None:
        (o_ref,) = rest
    else:
        side_ref, o_ref, so_ref = rest
        _side_cast(side_ref, so_ref, side_transposed)
    acc = jnp.dot(a_ref[...], w_ref[...], preferred_element_type=f32)
    o_ref[...] = epilogue(acc, *extra).astype(o_ref.dtype)


def _matmul(a, w, out_dtype, *, tm, tn, epilogue=None, extra=(), extra_specs=(), name,
            side=None, side_transposed=False):
    m, k = a.shape
    n = w.shape[1]
    if epilogue is None:
        epilogue = lambda acc: acc
    grid = (m // tm, n // tn)
    in_specs = [pl.BlockSpec((tm, k), lambda i, j: (i, 0)),
                pl.BlockSpec((k, tn), lambda i, j: (0, j)),
                *extra_specs]
    out_specs = pl.BlockSpec((tm, tn), lambda i, j: (i, j))
    out_shape = jax.ShapeDtypeStruct((m, n), out_dtype)
    operands = [a, w, *extra]
    if side is not None:
        s_in, s_out, s_shape = _side_cast_specs(side, side_transposed, grid[0] * grid[1],
                                                lambda i, j: i * grid[1] + j)
        in_specs.append(s_in)
        out_specs, out_shape = [out_specs, s_out], [out_shape, s_shape]
        operands.append(side)
    return pl.pallas_call(
        functools.partial(_mm_kernel, epilogue=epilogue, n_extra=len(extra),
                          side_transposed=side_transposed if side is not None else None),
        grid=grid,
        in_specs=in_specs,
        out_specs=out_specs,
        out_shape=out_shape,
        compiler_params=_params("arbitrary", "arbitrary"),
        name=name,
    )(*operands)


def _mm_w32_kernel(a_ref, w_ref, *rest, epilogue, n_extra, side_transposed):
    extra = rest[:n_extra]
    rest = rest[n_extra:]
    if side_transposed is None:
        o_ref, wb_ref = rest
    else:
        side_ref, o_ref, so_ref, wb_ref = rest
        _side_cast(side_ref, so_ref, side_transposed)

    @pl.when(pl.program_id(1) == 0)
    def _():
        wb_ref[...] = w_ref[...].astype(bf16)

    acc = jnp.dot(a_ref[...], wb_ref[...], preferred_element_type=f32)
    o_ref[...] = epilogue(acc, *extra).astype(o_ref.dtype)


def _matmul_w32(a, w, out_dtype, *, tm, tn, epilogue=None, extra=(), extra_specs=(), name,
                side=None, side_transposed=False):
    m, k = a.shape
    n = w.shape[1]
    if epilogue is None:
        epilogue = lambda acc: acc
    grid = (n // tn, m // tm)
    in_specs = [pl.BlockSpec((tm, k), lambda j, i: (i, 0)),
                pl.BlockSpec((k, tn), lambda j, i: (0, j)),
                *extra_specs]
    out_specs = pl.BlockSpec((tm, tn), lambda j, i: (i, j))
    out_shape = jax.ShapeDtypeStruct((m, n), out_dtype)
    operands = [a, w, *extra]
    if side is not None:
        s_in, s_out, s_shape = _side_cast_specs(side, side_transposed, grid[0] * grid[1],
                                                lambda j, i: j * grid[1] + i)
        in_specs.append(s_in)
        out_specs, out_shape = [out_specs, s_out], [out_shape, s_shape]
        operands.append(side)
    return pl.pallas_call(
        functools.partial(_mm_w32_kernel, epilogue=epilogue, n_extra=len(extra),
                          side_transposed=side_transposed if side is not None else None),
        grid=grid,
        in_specs=in_specs,
        out_specs=out_specs,
        out_shape=out_shape,
        scratch_shapes=[pltpu.VMEM((k, tn), bf16)],
        compiler_params=_params("arbitrary", "arbitrary"),
        name=name,
    )(*operands)


def _gate_epilogue(acc, b_ref):
    return 0.5 * (1.0 + jnp.tanh(0.5 * (acc + b_ref[...])))


def _resid_epilogue(acc, x_ref, g_ref):
    return x_ref[...] + g_ref[0] * acc


def _rms(x, g):
    return x * lax.rsqrt(jnp.mean(x * x, axis=-1, keepdims=True) + EPS) * g


def _dsa_prep_kernel(p_ref, gcq_ref, gckv_ref, gki_ref, bki_ref, cq_o, ckv_o, ki_o, wi_o):
    cq_o[...] = _rms(p_ref[:, OFF_CQ:OFF_CQ + A_Q_RANK], gcq_ref[...]).astype(bf16)
    ckv_o[...] = _rms(p_ref[:, OFF_CKV:OFF_CKV + A_KV_RANK], gckv_ref[...]).astype(bf16)
    ki = p_ref[:, OFF_KI:OFF_KI + IDX_DIM]
    mu = jnp.mean(ki, axis=-1, keepdims=True)
    var = jnp.mean(jnp.square(ki - mu), axis=-1, keepdims=True)
    ki_o[...] = ((ki - mu) * lax.rsqrt(var + EPS) * gki_ref[...] + bki_ref[...]).astype(bf16)
    wi_o[...] = p_ref[:, OFF_WI:OFF_WI + 128] * (IDX_HEADS ** -0.5 * IDX_DIM ** -0.5)


def _dsa_prep(proj, g_cq, g_ckv, g_ki, b_ki):
    ts = 512
    row = lambda i: (i, 0)
    fixed = lambda i: (0, 0)
    return pl.pallas_call(
        _dsa_prep_kernel,
        grid=(TOKENS // ts,),
        in_specs=[pl.BlockSpec((ts, 2048), row),
                  pl.BlockSpec((1, A_Q_RANK), fixed),
                  pl.BlockSpec((1, A_KV_RANK), fixed),
                  pl.BlockSpec((1, IDX_DIM), fixed),
                  pl.BlockSpec((1, IDX_DIM), fixed)],
        out_specs=[pl.BlockSpec((ts, A_Q_RANK), row),
                   pl.BlockSpec((ts, A_KV_RANK), row),
                   pl.BlockSpec((ts, IDX_DIM), row),
                   pl.BlockSpec((ts, 128), row)],
        out_shape=[jax.ShapeDtypeStruct((TOKENS, A_Q_RANK), bf16),
                   jax.ShapeDtypeStruct((TOKENS, A_KV_RANK), bf16),
                   jax.ShapeDtypeStruct((TOKENS, IDX_DIM), bf16),
                   jax.ShapeDtypeStruct((TOKENS, 128), f32)],
        compiler_params=_params("arbitrary"),
        name="dsa_prep",
    )(proj, g_cq, g_ckv, g_ki, b_ki)


def _qlat_kernel(q_ref, w_ref, o_ref):
    for h in range(A_HEADS):
        r = jnp.dot(q_ref[:, h * A_HEAD_DIM:(h + 1) * A_HEAD_DIM], w_ref[h], preferred_element_type=f32)
        o_ref[:, h] = r.astype(o_ref.dtype).reshape(o_ref.shape[0], Q_BLOCK, A_KV_RANK)


def _qlat(qq, wuk_t):
    tm = 512
    q_blk = IDX_HEADS * IDX_DIM // A_WIDTH
    return pl.pallas_call(
        _qlat_kernel,
        grid=(TOKENS // tm,),
        in_specs=[pl.BlockSpec((tm, A_WIDTH), lambda i: (i, q_blk)),
                  pl.BlockSpec((A_HEADS, A_HEAD_DIM, A_KV_RANK), lambda i: (0, 0, 0))],
        out_specs=pl.BlockSpec((tm // Q_BLOCK, A_HEADS, Q_BLOCK, A_KV_RANK), lambda i: (i, 0, 0, 0)),
        out_shape=jax.ShapeDtypeStruct((TOKENS // Q_BLOCK, A_HEADS, Q_BLOCK, A_KV_RANK), bf16),
        compiler_params=_params("arbitrary"),
        name="q_lat",
    )(qq, wuk_t)


def _t5_kernel(t5_ref, o_ref):
    h = pl.program_id(0)
    half = T5_BUCKETS // 2
    exact = half // 2
    qi = lax.broadcasted_iota(i32, (Q_BLOCK, NEAR), 0)
    kj = lax.broadcasted_iota(i32, (Q_BLOCK, NEAR), 1)
    rel = kj - NEAR_BACK - qi
    n = jnp.abs(rel)
    log_ratio = jnp.log(jnp.maximum(n, 1).astype(f32) / exact) / math.log(T5_MAX_DIST / exact)
    large = jnp.minimum(exact + (log_ratio * (half - exact)).astype(i32), half - 1)
    bucket = jnp.where(rel > 0, half, 0) + jnp.where(n < exact, n, large)
    acc = jnp.zeros((Q_BLOCK, NEAR), f32)
    for k in range(T5_BUCKETS):
        acc = jnp.where(bucket == k, t5_ref[k, h], acc)
    o_ref[0] = (acc - t5_ref[half - 1, h]) * (1.0 / ATTN_SCALE)


def _t5_table(t5_bias):
    return pl.pallas_call(
        _t5_kernel,
        grid=(A_HEADS,),
        in_specs=[pl.BlockSpec(memory_space=pltpu.SMEM)],
        out_specs=pl.BlockSpec((1, Q_BLOCK, NEAR), lambda h: (h, 0, 0)),
        out_shape=jax.ShapeDtypeStruct((A_HEADS, Q_BLOCK, NEAR), f32),
        compiler_params=_params("arbitrary"),
        name="t5_table",
    )(t5_bias)


DSA_COL_STEP = 512
DSA_HEAD_GROUP = 8
DSA_ROWS = DSA_HEAD_GROUP * Q_BLOCK


def _dsa_block(width, qb, t5_ref, qi_ref, kit_ref, wi_ref, ql_ref, kv_ref, wuv_ref, pt_ref, o_ref,
               key_ref, mask_ref, lg_ref, e_ref, l_ref):
    q0 = qb * Q_BLOCK
    ct_w = 256
    w = wi_ref[...]
    score_ref = lg_ref.at[0:Q_BLOCK]

    def col_tile(ct, carry):
        c0 = pl.multiple_of(ct * ct_w, ct_w)
        kt = kit_ref[:, pl.ds(c0, ct_w)]
        acc = jnp.zeros((Q_BLOCK, ct_w), f32)
        for h in range(IDX_HEADS):
            x = jnp.dot(qi_ref[:, h * IDX_DIM:(h + 1) * IDX_DIM], kt, preferred_element_type=f32)
            acc = acc + jnp.maximum(x, 0.0) * w[:, h:h + 1]
        score_ref[:, pl.ds(c0, ct_w)] = acc
        return carry

    lax.fori_loop(0, width // ct_w, col_tile, 0)

    row = lax.broadcasted_iota(i32, (Q_BLOCK, width), 0)
    col = lax.broadcasted_iota(i32, (Q_BLOCK, width), 1)
    adm = (col // CHUNK) <= ((q0 + row) // CHUNK)
    score = jnp.where(adm, score_ref[:, 0:width], -jnp.inf)
    bits = pltpu.bitcast(score, i32)
    key_ref[:, 0:width] = jnp.where(bits < 0, bits ^ 0x7FFFFFFF, bits)

    def bisect(b, thr_u):
        cand_u = thr_u | jnp.left_shift(jnp.int32(1), 31 - b)
        hit = jnp.where(key_ref[:, 0:width] >= (cand_u ^ INT_MIN), 1.0, 0.0)
        cnt = jnp.sum(hit, axis=1, keepdims=True)
        return jnp.where(cnt >= TOPK, cand_u, thr_u)

    thr_u = lax.fori_loop(0, 32, bisect, jnp.zeros((Q_BLOCK, 1), i32))
    sel = (key_ref[:, 0:width] >= (thr_u ^ INT_MIN)) & adm
    mask_ref[:, 0:width] = jnp.where(sel, 0.0, -jnp.inf)

    far_bucket = T5_BUCKETS // 2 - 1

    kv = kv_ref[0:width, :]

    def head_group(g, carry):
        h0 = g * DSA_HEAD_GROUP
        qg = ql_ref[0, pl.ds(h0, DSA_HEAD_GROUP)].reshape(DSA_ROWS, A_KV_RANK)
        lg_ref[:, 0:width] = lax.dot_general(qg, kv, (((1,), (1,)), ((), ())), preferred_element_type=f32)

        @pl.when(qb == 0)
        def _():
            for hh in range(DSA_HEAD_GROUP):
                lg_ref[hh * Q_BLOCK:(hh + 1) * Q_BLOCK, 0:Q_BLOCK] += pt_ref[h0 + hh, :, NEAR_BACK:NEAR]

        @pl.when(qb > 0)
        def _():
            w0 = pl.multiple_of(q0 - NEAR_BACK, 128)
            for hh in range(DSA_HEAD_GROUP):
                lg_ref[hh * Q_BLOCK:(hh + 1) * Q_BLOCK, pl.ds(w0, NEAR)] += pt_ref[h0 + hh]

        for hh in range(DSA_HEAD_GROUP):
            rs = slice(hh * Q_BLOCK, (hh + 1) * Q_BLOCK)
            x = lg_ref[rs, 0:width] * (ATTN_SCALE * LOG2E) + (
                mask_ref[:, 0:width] + t5_ref[far_bucket, h0 + hh] * LOG2E)
            m = jnp.max(x, axis=1, keepdims=True)
            e = jnp.exp2(x - m)
            l_ref[rs, :] = jnp.sum(e, axis=1, keepdims=True)
            e_ref[rs, 0:width] = e.astype(bf16)

        ol = jnp.dot(e_ref[:, 0:width], kv, preferred_element_type=f32) / l_ref[...]
        for hh in range(DSA_HEAD_GROUP):
            o = jnp.dot(ol[hh * Q_BLOCK:(hh + 1) * Q_BLOCK].astype(bf16), wuv_ref[h0 + hh],
                        preferred_element_type=f32)
            c0 = pl.multiple_of((h0 + hh) * A_HEAD_DIM, A_HEAD_DIM)
            o_ref[:, pl.ds(c0, A_HEAD_DIM)] = o.astype(o_ref.dtype)
        return carry

    lax.fori_loop(0, A_HEADS // DSA_HEAD_GROUP, head_group, 0)


def _dsa_kernel(t5_ref, qi_ref, kit_ref, wi_ref, ql_ref, kv_ref, wuv_ref, pt_ref, o_ref,
                key_ref, mask_ref, lg_ref, e_ref, l_ref):
    qb = pl.program_id(1)
    blocks_per_step = DSA_COL_STEP // Q_BLOCK
    for n in range(SEQ // DSA_COL_STEP):
        pl.when(qb // blocks_per_step == n)(functools.partial(
            _dsa_block, (n + 1) * DSA_COL_STEP, qb, t5_ref, qi_ref, kit_ref, wi_ref, ql_ref, kv_ref,
            wuv_ref, pt_ref, o_ref, key_ref, mask_ref, lg_ref, e_ref, l_ref))


def _dsa(t5_bias, qq, kidx_t, widx, qlat, ckv, wuv, ptab):
    nqb = SEQ // Q_BLOCK
    return pl.pallas_call(
        _dsa_kernel,
        grid=(BATCH, nqb),
        in_specs=[pl.BlockSpec(memory_space=pltpu.SMEM),
                  pl.BlockSpec((Q_BLOCK, IDX_HEADS * IDX_DIM), lambda b, i: (b * nqb + i, 0)),
                  pl.BlockSpec((IDX_DIM, SEQ), lambda b, i: (0, b)),
                  pl.BlockSpec((Q_BLOCK, 128), lambda b, i: (b * nqb + i, 0)),
                  pl.BlockSpec((1, A_HEADS, Q_BLOCK, A_KV_RANK), lambda b, i: (b * nqb + i, 0, 0, 0)),
                  pl.BlockSpec((SEQ, A_KV_RANK), lambda b, i: (b, 0)),
                  pl.BlockSpec((A_HEADS, A_KV_RANK, A_HEAD_DIM), lambda b, i: (0, 0, 0)),
                  pl.BlockSpec((A_HEADS, Q_BLOCK, NEAR), lambda b, i: (0, 0, 0))],
        out_specs=pl.BlockSpec((Q_BLOCK, A_WIDTH), lambda b, i: (b * nqb + i, 0)),
        out_shape=jax.ShapeDtypeStruct((TOKENS, A_WIDTH), bf16),
        scratch_shapes=[pltpu.VMEM((Q_BLOCK, SEQ), i32),
                        pltpu.VMEM((Q_BLOCK, SEQ), f32),
                        pltpu.VMEM((DSA_ROWS, SEQ), f32),
                        pltpu.VMEM((DSA_ROWS, SEQ), bf16),
                        pltpu.VMEM((DSA_ROWS, 1), f32)],
        compiler_params=_params("arbitrary", "arbitrary"),
        name="dsa_main",
    )(t5_bias, qq, kidx_t, widx, qlat, ckv, wuv, ptab)


def _ret_kernel(q_ref, k_ref, v_ref, g_ref, cc_ref, ss_ref, dm_ref, qd_ref, kd_ref, cd_ref, gr_ref,
                o_ref, qs_ref, ks_ref):
    cc = cc_ref[...]
    ss = ss_ref[...]
    q = q_ref[...]
    k = k_ref[...] * (R_QK_DIM ** -0.5)
    half = R_QK_DIM // 2
    qs_ref[...] = q * cc + pltpu.roll(q, half, 1) * ss
    ks_ref[...] = k * cc + pltpu.roll(k, half, 1) * ss
    dm = dm_ref[0]
    qd = qd_ref[0]
    kd = kd_ref[0]
    cd = cd_ref[0]
    gr = gr_ref[0]

    def chunk(n, state):
        r0 = pl.multiple_of(n * CHUNK, CHUNK)
        qn = qs_ref[pl.ds(r0, CHUNK), :]
        kn = ks_ref[pl.ds(r0, CHUNK), :]
        vn = v_ref[pl.ds(r0, CHUNK), :].astype(bf16)
        att = lax.dot_general(qn.astype(bf16), kn.astype(bf16), (((1,), (1,)), ((), ())),
                              preferred_element_type=f32) * dm
        y = jnp.dot(att.astype(bf16), vn, preferred_element_type=f32)
        y = y + jnp.dot((qn * qd).astype(bf16), state.astype(bf16), preferred_element_type=f32)
        kv = lax.dot_general((kn * kd).astype(bf16), vn, (((0,), (0,)), ((), ())),
                             preferred_element_type=f32)
        y = y * lax.rsqrt(jnp.mean(y * y, axis=-1, keepdims=True) + EPS) * gr
        gate = g_ref[pl.ds(r0, CHUNK), :]
        o_ref[pl.ds(r0, CHUNK), :] = (gate * jax.nn.sigmoid(gate) * y).astype(o_ref.dtype)
        return state * cd + kv

    lax.fori_loop(0, SEQ // CHUNK, chunk, jnp.zeros((R_QK_DIM, R_V_DIM), f32), unroll=16)


def _retention(proj, cc, ss, dmask, qdec, kdec, cdec, g_ret):
    qk_blk = OFF_RQ // R_QK_DIM
    k_blk = OFF_RK // R_QK_DIM
    v_blk = OFF_RV // R_V_DIM
    g_blk = OFF_RG // R_V_DIM
    per_head = lambda b, h: (h, 0, 0)
    return pl.pallas_call(
        _ret_kernel,
        grid=(BATCH, R_HEADS),
        in_specs=[pl.BlockSpec((SEQ, R_QK_DIM), lambda b, h: (b, qk_blk + h)),
                  pl.BlockSpec((SEQ, R_QK_DIM), lambda b, h: (b, k_blk + h)),
                  pl.BlockSpec((SEQ, R_V_DIM), lambda b, h: (b, v_blk + h)),
                  pl.BlockSpec((SEQ, R_V_DIM), lambda b, h: (b, g_blk + h)),
                  pl.BlockSpec((SEQ, R_QK_DIM), lambda b, h: (0, 0)),
                  pl.BlockSpec((SEQ, R_QK_DIM), lambda b, h: (0, 0)),
                  pl.BlockSpec((1, CHUNK, CHUNK), per_head),
                  pl.BlockSpec((1, CHUNK, R_QK_DIM), per_head),
                  pl.BlockSpec((1, CHUNK, R_QK_DIM), per_head),
                  pl.BlockSpec((1, 1, R_V_DIM), per_head),
                  pl.BlockSpec((1, 1, R_V_DIM), per_head)],
        out_specs=pl.BlockSpec((SEQ, R_V_DIM), lambda b, h: (b, h)),
        out_shape=jax.ShapeDtypeStruct((TOKENS, R_WIDTH), bf16),
        scratch_shapes=[pltpu.VMEM((SEQ, R_QK_DIM), f32), pltpu.VMEM((SEQ, R_QK_DIM), f32)],
        compiler_params=_params("arbitrary", "arbitrary"),
        name="retention",
    )(proj, proj, proj, proj, cc, ss, dmask, qdec, kdec, cdec, g_ret)


def _merge_kernel(ya_ref, yr_ref, wa_ref, wr_ref, ga_ref, gr_ref, o_ref, wab_ref, wrb_ref):
    @pl.when(pl.program_id(1) == 0)
    def _():
        wab_ref[...] = wa_ref[...].astype(bf16)
        wrb_ref[...] = wr_ref[...].astype(bf16)

    pa = jnp.dot(ya_ref[...], wab_ref[...], preferred_element_type=f32)
    pr = jnp.dot(yr_ref[...], wrb_ref[...], preferred_element_type=f32)
    o_ref[...] = (ga_ref[...].astype(f32) * pa + gr_ref[...].astype(f32) * pr).astype(o_ref.dtype)


def _merge(ya, yr, w_up, gates):
    assert A_WIDTH == R_WIDTH
    tm, tn = 1024, 512
    nj = D_MODEL // tn
    return pl.pallas_call(
        _merge_kernel,
        grid=(nj, TOKENS // tm),
        in_specs=[pl.BlockSpec((tm, A_WIDTH), lambda j, i: (i, 0)),
                  pl.BlockSpec((tm, R_WIDTH), lambda j, i: (i, 0)),
                  pl.BlockSpec((A_WIDTH, tn), lambda j, i: (0, j)),
                  pl.BlockSpec((R_WIDTH, tn), lambda j, i: (1, j)),
                  pl.BlockSpec((tm, tn), lambda j, i: (i, j)),
                  pl.BlockSpec((tm, tn), lambda j, i: (i, nj + j))],
        out_specs=pl.BlockSpec((tm, tn), lambda j, i: (i, j)),
        out_shape=jax.ShapeDtypeStruct((TOKENS, D_MODEL), bf16),
        scratch_shapes=[pltpu.VMEM((A_WIDTH, tn), bf16), pltpu.VMEM((R_WIDTH, tn), bf16)],
        compiler_params=_params("arbitrary", "arbitrary"),
        name="merge_up",
    )(ya, yr, w_up, w_up, gates, gates)


ROUTE_CAND_GROUPS = ((0, 16), (1, 8), (2, 8), (3, 8), (4, 8), (5, 8), (6, 8), (7, 8))
ROUTE_CAND_ROWS = sum(n for _, n in ROUTE_CAND_GROUPS) + 8


def _peer_route_kernel(q_ref, k_ref, s1_o, s2_o, a1_o, m2_o, thr_o, v1_ref, v2_ref, cand_ref, ec_ref):
    hq = P_QUERY_DIM // 2
    s1_all = jnp.dot(k_ref[0], q_ref[0:hq, :].astype(bf16), preferred_element_type=f32)
    s2_all = jnp.dot(k_ref[1], q_ref[hq:P_QUERY_DIM, :].astype(bf16), preferred_element_type=f32)

    def top_values(s, v_ref):
        cur = s
        for r in range(P_TOPK):
            m = jnp.max(cur, axis=0, keepdims=True)
            v_ref[r:r + 1, :] = m
            cur = jnp.where(cur == m, -jnp.inf, cur)

    for lt in range(s1_all.shape[1] // 128):
        ls = slice(lt * 128, (lt + 1) * 128)
        s1 = s1_all[:, ls]
        s2 = s2_all[:, ls]
        top_values(s1, v1_ref)
        top_values(s2, v2_ref)
        v1 = v1_ref[...]
        v2 = v2_ref[...]
        m1 = v1[0:1]
        m2 = v2[0:1]
        e1 = jnp.exp(v1 - m1)
        e2 = jnp.exp(v2 - m2)
        off = 0
        for r1, n in ROUTE_CAND_GROUPS:
            cand_ref[off:off + n, :] = v1[r1:r1 + 1] + v2[0:n]
            ec_ref[off:off + n, :] = e1[r1:r1 + 1] * e2[0:n]
            off += n
        cand_ref[off:off + 8, :] = v1[8:16] + v2[0:1]
        ec_ref[off:off + 8, :] = e1[8:16] * e2[0:1]
        cand = cand_ref[...]
        cur = cand
        thr = None
        for r in range(P_TOPK):
            thr = jnp.max(cur, axis=0, keepdims=True)
            cur = jnp.where(cur == thr, -jnp.inf, cur)
        z = jnp.sum(jnp.where(cand >= thr, ec_ref[...], 0.0), axis=0, keepdims=True)
        a1 = 0.5 * jnp.exp(s1 - m1) / z
        for r in range(PEER_NC):
            s1_o[0, r, :, ls] = s1[r * PEER_ROWS:(r + 1) * PEER_ROWS]
            a1_o[0, r, :, ls] = a1[r * PEER_ROWS:(r + 1) * PEER_ROWS]
        s2_o[0, lt] = s2
        m2_o[0, :, ls] = m2
        thr_o[0, :, ls] = thr


def _peer_route(q_t, keys):
    tl = 512
    big = lambda h, j: (h, 0, j)
    chunked = lambda h, j: (h, 0, 0, j)
    chunked_shape = jax.ShapeDtypeStruct((P_HEADS, PEER_NC, PEER_ROWS, TOKENS), f32)
    row_shape = jax.ShapeDtypeStruct((P_HEADS, 1, TOKENS), f32)
    return pl.pallas_call(
        _peer_route_kernel,
        grid=(P_HEADS, TOKENS // tl),
        in_specs=[pl.BlockSpec((P_QUERY_DIM, tl), lambda h, j: (h, j)),
                  pl.BlockSpec((2, P_NKEYS, P_QUERY_DIM // 2), lambda h, j: (0, 0, 0))],
        out_specs=[pl.BlockSpec((1, PEER_NC, PEER_ROWS, tl), chunked),
                   pl.BlockSpec((1, tl // 128, P_NKEYS, 128), lambda h, j: (h, j, 0, 0)),
                   pl.BlockSpec((1, PEER_NC, PEER_ROWS, tl), chunked),
                   pl.BlockSpec((1, 1, tl), big),
                   pl.BlockSpec((1, 1, tl), big)],
        out_shape=[chunked_shape, jax.ShapeDtypeStruct((P_HEADS, TOKENS // 128, P_NKEYS, 128), f32),
                   chunked_shape, row_shape, row_shape],
        scratch_shapes=[pltpu.VMEM((P_TOPK, 128), f32), pltpu.VMEM((P_TOPK, 128), f32),
                        pltpu.VMEM((ROUTE_CAND_ROWS, 128), f32), pltpu.VMEM((ROUTE_CAND_ROWS, 128), f32)],
        compiler_params=_params("arbitrary", "arbitrary"),
        name="peer_route",
    )(q_t, keys)


PEER_TM = 512
PEER_TE = 512
PEER_NC = P_EXPERTS // PEER_TE
PEER_ROWS = PEER_TE // P_NKEYS
PEER_KT = 64
PEER_KPIECES = 8


def _peer_kernel(ht_ref, u_ref, vt_ref, s1_ref, a1_ref, s2_ref, m2_ref, thr_ref, o_ref,
                 b2_ref, pre_ref, w_ref):
    c = pl.program_id(1)

    @pl.when(c == 0)
    def _():
        o_ref[...] = jnp.zeros_like(o_ref)
        for lt in range(PEER_TM // 128):
            b2_ref[:, lt] = jnp.exp(s2_ref[:, lt] - m2_ref[:, :, lt * 128:(lt + 1) * 128])

    @pl.when(c > 0)
    def _():
        pre = pre_ref[...]
        act2 = pre * (1.0 + lax.erf(pre * np.float32(np.sqrt(0.5))))
        coef = (w_ref[0:PEER_TE, :] * act2).astype(bf16)
        o_ref[...] += jnp.dot(vt_ref[0], coef, preferred_element_type=f32)

    @pl.when(c < PEER_NC)
    def _():
        z = pl.multiple_of(jnp.minimum(c, 0), PEER_TE)
        units = [(j, lt, kt) for j in range(PEER_ROWS) for lt in range(PEER_TM // 128)
                 for kt in range(P_NKEYS // PEER_KT)]
        per_piece = len(units) // PEER_KPIECES
        kw = D_MODEL // PEER_KPIECES
        for kq in range(PEER_KPIECES):
            part = jnp.dot(u_ref[:, kq * kw:(kq + 1) * kw], ht_ref[kq * kw:(kq + 1) * kw, :],
                           preferred_element_type=f32)
            if kq == 0:
                pre_ref[...] = part
            else:
                pre_ref[...] += part
            w_ref[pl.ds(z + PEER_TE, 8), 0:128] = part[PEER_TE - 8:PEER_TE, PEER_TM - 128:PEER_TM]
            for j, lt, kt in units[kq * per_piece:(kq + 1) * per_piece]:
                ls = slice(lt * 128, (lt + 1) * 128)
                ks = slice(kt * PEER_KT, (kt + 1) * PEER_KT)
                wj = jnp.zeros((PEER_KT, 128), f32)
                for h in range(P_HEADS):
                    sel = (s1_ref[h, 0, j:j + 1, ls] + s2_ref[h, lt, ks, :]) >= thr_ref[h, :, ls]
                    wj = wj + jnp.where(sel, b2_ref[h, lt, ks, :], 0.0) * a1_ref[h, 0, j:j + 1, ls]
                w0 = pl.multiple_of(z + (j * P_NKEYS + kt * PEER_KT), PEER_KT)
                w_ref[pl.ds(w0, PEER_KT), ls] = wj


def _peer(h_t, u, v_t, s1, a1, s2, m2, thr):
    tm, te, nc = PEER_TM, PEER_TE, PEER_NC
    s1r, a1r = s1, a1
    assert v_t.shape == (nc, D_MODEL, te)
    tok3 = lambda i, c: (0, 0, i)
    cur = lambda i, c: (0, jnp.minimum(c, nc - 1), 0, i)
    return pl.pallas_call(
        _peer_kernel,
        grid=(TOKENS // tm, nc + 1),
        in_specs=[pl.BlockSpec((D_MODEL, tm), lambda i, c: (0, i)),
                  pl.BlockSpec((te, D_MODEL), lambda i, c: (jnp.minimum(c, nc - 1), 0)),
                  pl.BlockSpec((1, D_MODEL, te), lambda i, c: (jnp.maximum(c - 1, 0), 0, 0)),
                  pl.BlockSpec((P_HEADS, 1, PEER_ROWS, tm), cur),
                  pl.BlockSpec((P_HEADS, 1, PEER_ROWS, tm), cur),
                  pl.BlockSpec((P_HEADS, tm // 128, P_NKEYS, 128), lambda i, c: (0, i, 0, 0)),
                  pl.BlockSpec((P_HEADS, 1, tm), tok3),
                  pl.BlockSpec((P_HEADS, 1, tm), tok3)],
        out_specs=pl.BlockSpec((D_MODEL, tm), lambda i, c: (0, i)),
        out_shape=jax.ShapeDtypeStruct((D_MODEL, TOKENS), f32),
        scratch_shapes=[pltpu.VMEM((P_HEADS, tm // 128, P_NKEYS, 128), f32),
                        pltpu.VMEM((te, tm), f32), pltpu.VMEM((te + 8, tm), f32)],
        compiler_params=_params("arbitrary", "arbitrary"),
        name="peer_experts",
    )(h_t, u, v_t, s1r, a1r, s2, m2, thr)


def _final_kernel(x_ref, yt_ref, gt_ref, g_ref, o_ref):
    x = x_ref[...] + gt_ref[0] * yt_ref[...].T
    o_ref[...] = x * lax.rsqrt(jnp.mean(x * x, axis=-1, keepdims=True) + EPS) * g_ref[...]


def _final(x1, y_t, modr, g_final):
    ts = 256
    per_b = SEQ // ts
    return pl.pallas_call(
        _final_kernel,
        grid=(TOKENS // ts,),
        in_specs=[pl.BlockSpec((ts, D_MODEL), lambda i: (i, 0)),
                  pl.BlockSpec((D_MODEL, ts), lambda i: (0, i)),
                  pl.BlockSpec((1, 1, D_MODEL), lambda i: ((i // per_b) * N_MOD + 5, 0, 0)),
                  pl.BlockSpec((1, D_MODEL), lambda i: (0, 0))],
        out_specs=pl.BlockSpec((ts, D_MODEL), lambda i: (i, 0)),
        out_shape=jax.ShapeDtypeStruct((TOKENS, D_MODEL), f32),
        compiler_params=_params("arbitrary"),
        name="final_norm",
    )(x1, y_t, modr, g_final)


def _retention_tables():
    half = R_QK_DIM // 2
    inv = 1.0 / (ROT_BASE ** jnp.linspace(0.0, 1.0, half, dtype=f32))
    ang = jnp.arange(SEQ, dtype=f32)[:, None] * inv[None, :]
    cos, sin = jnp.cos(ang), jnp.sin(ang)
    cc = jnp.concatenate([cos, cos], axis=-1)
    ss = jnp.concatenate([-sin, sin], axis=-1)
    log_g = jnp.log(1.0 - jnp.power(2.0, -5.0 - jnp.arange(R_HEADS, dtype=f32)))
    j = jnp.arange(CHUNK, dtype=f32)
    diff = j[:, None] - j[None, :]
    dmask = jnp.where(diff[None] >= 0, jnp.exp(jnp.maximum(diff, 0.0)[None] * log_g[:, None, None]), 0.0)
    kdec = jnp.exp((CHUNK - 1.0 - j)[None, :] * log_g[:, None])
    qdec = jnp.exp((j + 1.0)[None, :] * log_g[:, None])
    cdec = jnp.exp(CHUNK * log_g)
    kdec = jnp.broadcast_to(kdec[:, :, None], (R_HEADS, CHUNK, R_QK_DIM))
    qdec = jnp.broadcast_to(qdec[:, :, None], (R_HEADS, CHUNK, R_QK_DIM))
    cdec = jnp.broadcast_to(cdec[:, None, None], (R_HEADS, 1, R_V_DIM))
    return cc, ss, dmask, qdec, kdec, cdec


def kernel(x, c, w_ada, b_ada, g_mix, w_in, g_cq, g_ckv, w_uq, w_uk, w_uv, w_qi, g_ki, b_ki, t5_bias, g_ret,
           w_up, w_gate, b_gate, w_out, g_ffn, w_pq, sub_keys, u_exp, v_exp, g_final):
    x2d = x.reshape(TOKENS, D_MODEL)

    w_in_p = _w_in_layout(w_in[0].T)
    w_q_all = jnp.concatenate(
        [w_qi[0].reshape(A_Q_RANK, IDX_HEADS * IDX_DIM), w_uq[0].reshape(A_Q_RANK, A_WIDTH)], axis=1).astype(bf16)
    wuk_t = jnp.transpose(w_uk[0], (1, 2, 0)).astype(bf16)
    wuv_h = jnp.transpose(w_uv[0], (1, 0, 2)).astype(bf16)
    w_pq_t = w_pq[0].T.astype(bf16)
    keys_b = sub_keys[0].astype(bf16)

    c8 = jnp.pad(c, ((0, 8 - BATCH), (0, 0)))
    mod = _ada(c8, w_ada[0], b_ada[0].reshape(1, N_MOD * D_MODEL))[:BATCH]
    modr = mod.reshape(BATCH * N_MOD, 1, D_MODEL)

    h = _normmod(x2d, g_mix[0].reshape(1, D_MODEL), modr, 0, 1)
    proj, v_t = _matmul(h, w_in_p, f32, tm=1024, tn=512, name="in_proj",
                        side=v_exp[0], side_transposed=True)
    gates, u_b = _matmul_w32(h, w_gate[0], bf16, tm=1024, tn=512, epilogue=_gate_epilogue,
                             extra=(b_gate[0].reshape(1, 2 * D_MODEL),),
                             extra_specs=(pl.BlockSpec((1, 512), lambda j, i: (0, j)),), name="gates",
                             side=u_exp[0])

    cqn, ckvn, kidx, widx = _dsa_prep(proj, g_cq[0].reshape(1, -1), g_ckv[0].reshape(1, -1),
                                      g_ki[0].reshape(1, -1), b_ki[0].reshape(1, -1))
    qq = _matmul(cqn, w_q_all, bf16, tm=2048, tn=1024, name="q_up")
    qlat = _qlat(qq, wuk_t)
    ptab = _t5_table(t5_bias)
    y_a = _dsa(t5_bias, qq, kidx.T, widx, qlat, ckvn, wuv_h, ptab)

    cc, ss, dmask, qdec, kdec, cdec = _retention_tables()
    y_r = _retention(proj, cc, ss, dmask, qdec, kdec, cdec, g_ret[0].reshape(R_HEADS, 1, R_V_DIM))

    merged = _merge(y_a, y_r, w_up[0], gates)
    per_b = SEQ // 1024
    x1 = _matmul_w32(merged, w_out[0], f32, tm=1024, tn=512, epilogue=_resid_epilogue,
                     extra=(x2d, modr),
                     extra_specs=(pl.BlockSpec((1024, 512), lambda j, i: (i, j)),
                                  pl.BlockSpec((1, 1, 512), lambda j, i: ((i // per_b) * N_MOD + 2, 0, j))),
                     name="out_proj")

    h2_t = _normmod(x1, g_ffn[0].reshape(1, D_MODEL), modr, 3, 4, transposed=True)
    q_t = _matmul(w_pq_t, h2_t, f32, tm=1024, tn=512, name="peer_q")
    s1, s2, a1, m2, thr = _peer_route(q_t, keys_b)
    y_t = _peer(h2_t, u_b, v_t, s1, a1, s2, m2, thr)
    out = _final(x1, y_t, modr, g_final.reshape(1, D_MODEL))
    return out.reshape(BATCH, SEQ, D_MODEL)
```

```python
import functools
import math

import numpy as np
import jax
import jax.numpy as jnp
from jax import lax
from jax.experimental import pallas as pl
from jax.experimental.pallas import tpu as pltpu

f32 = jnp.float32
bf16 = jnp.bfloat16
i32 = jnp.int32

D_MODEL = 4096
BATCH = 4
SEQ = 2048
TOKENS = BATCH * SEQ
CHUNK = 64
EPS = 1e-6
N_MOD = 6
A_HEADS = 16
A_HEAD_DIM = 128
A_Q_RANK = 1024
A_KV_RANK = 512
A_WIDTH = A_HEADS * A_HEAD_DIM
IDX_HEADS = 64
IDX_DIM = 128
TOPK = 256
T5_BUCKETS = 32
T5_MAX_DIST = 128
R_HEADS = 8
R_QK_DIM = 128
R_V_DIM = 256
R_QK_WIDTH = R_HEADS * R_QK_DIM
R_WIDTH = R_HEADS * R_V_DIM
ROT_BASE = 10000.0
P_HEADS = 8
P_QUERY_DIM = 256
P_NKEYS = 128
P_TOPK = 16
P_EXPERTS = P_NKEYS * P_NKEYS

PROJ_WIDTH = 8192
OFF_CQ, OFF_CKV, OFF_KI, OFF_WI = 0, 1024, 1536, 1664
OFF_RQ, OFF_RK, OFF_RV, OFF_RG = 2048, 3072, 4096, 6144

Q_BLOCK = 128
NEAR_BACK = 128
NEAR = NEAR_BACK + Q_BLOCK
ATTN_SCALE = A_HEAD_DIM ** -0.5
LOG2E = math.log2(math.e)
VMEM_LIMIT = 56 * 1024 * 1024
INT_MIN = -2147483648


def _params(*sem, flags=None):
    return pltpu.CompilerParams(dimension_semantics=sem, vmem_limit_bytes=VMEM_LIMIT, flags=flags)


W_IN_SLAB = 64
W_IN_TILE = 512
W_IN_SLABS = W_IN_TILE // W_IN_SLAB
W_IN_USED = OFF_WI + IDX_HEADS
W_IN_PAD_TILE = W_IN_USED // W_IN_TILE
W_IN_SHIFT = (OFF_RQ - W_IN_USED) // W_IN_SLAB


def _w_in_kernel(*refs):
    o_ref = refs[-1]
    j = pl.program_id(0)
    first_pad = (W_IN_USED - W_IN_PAD_TILE * W_IN_TILE) // W_IN_SLAB
    parts = []
    for k in range(W_IN_SLABS):
        x = refs[k][...]
        if k >= first_pad:
            x = jnp.where(j == W_IN_PAD_TILE, 0.0, x)
        parts.append(x)
    o_ref[...] = jnp.concatenate(parts, axis=0).T.astype(bf16)


def _w_in_layout(w_in_t):
    def slab(k):
        def index(j):
            src = jnp.where(j <= W_IN_PAD_TILE, j * W_IN_SLABS + k, j * W_IN_SLABS + k - W_IN_SHIFT)
            return (src, 0)
        return pl.BlockSpec((W_IN_SLAB, D_MODEL), index)

    return pl.pallas_call(
        _w_in_kernel,
        grid=(PROJ_WIDTH // W_IN_TILE,),
        in_specs=[slab(k) for k in range(W_IN_SLABS)],
        out_specs=pl.BlockSpec((D_MODEL, W_IN_TILE), lambda j: (0, j)),
        out_shape=jax.ShapeDtypeStruct((D_MODEL, PROJ_WIDTH), bf16),
        compiler_params=_params("arbitrary"),
        name="w_in_layout",
    )(*([w_in_t] * W_IN_SLABS))


def _ada_kernel(c_ref, w_ref, b_ref, o_ref):
    c = c_ref[...]
    ca = (c * jax.nn.sigmoid(c)).astype(bf16)
    o_ref[...] = jnp.dot(ca, w_ref[...].astype(bf16), preferred_element_type=f32) + b_ref[...]


def _ada(c8, w, b):
    n = w.shape[1]
    tn = 1024
    return pl.pallas_call(
        _ada_kernel,
        grid=(n // tn,),
        in_specs=[pl.BlockSpec((8, D_MODEL), lambda j: (0, 0)),
                  pl.BlockSpec((D_MODEL, tn), lambda j: (0, j)),
                  pl.BlockSpec((1, tn), lambda j: (0, j))],
        out_specs=pl.BlockSpec((8, tn), lambda j: (0, j)),
        out_shape=jax.ShapeDtypeStruct((8, n), f32),
        compiler_params=_params("arbitrary"),
        name="ada_mod",
    )(c8, w, b)


def _normmod_kernel(x_ref, g_ref, sh_ref, sc_ref, o_ref, *, transposed):
    x = x_ref[...]
    y = x * lax.rsqrt(jnp.mean(x * x, axis=-1, keepdims=True) + EPS) * g_ref[...]
    y = y * (1.0 + sc_ref[0]) + sh_ref[0]
    o_ref[...] = (y.T if transposed else y).astype(o_ref.dtype)


def _normmod(x2d, g, modr, shift_slot, scale_slot, transposed=False):
    ts = 256
    per_b = SEQ // ts
    if transposed:
        out_spec = pl.BlockSpec((D_MODEL, ts), lambda i: (0, i))
        out_shape = jax.ShapeDtypeStruct((D_MODEL, TOKENS), bf16)
    else:
        out_spec = pl.BlockSpec((ts, D_MODEL), lambda i: (i, 0))
        out_shape = jax.ShapeDtypeStruct((TOKENS, D_MODEL), bf16)
    return pl.pallas_call(
        functools.partial(_normmod_kernel, transposed=transposed),
        grid=(TOKENS // ts,),
        in_specs=[pl.BlockSpec((ts, D_MODEL), lambda i: (i, 0)),
                  pl.BlockSpec((1, D_MODEL), lambda i: (0, 0)),
                  pl.BlockSpec((1, 1, D_MODEL), lambda i: ((i // per_b) * N_MOD + shift_slot, 0, 0)),
                  pl.BlockSpec((1, 1, D_MODEL), lambda i: ((i // per_b) * N_MOD + scale_slot, 0, 0))],
        out_specs=out_spec,
        out_shape=out_shape,
        compiler_params=_params("arbitrary"),
        name="norm_modulate",
    )(x2d, g, modr, modr)


SIDE_CHUNK = 512


def _side_cast_specs(side, transposed, n_steps, step_of):
    r, c = side.shape
    rows = r // n_steps
    in_spec = pl.BlockSpec((rows, c), lambda g0, g1: (step_of(g0, g1), 0))
    if transposed:
        per = SIDE_CHUNK // rows
        out_spec = pl.BlockSpec((1, c, rows), lambda g0, g1: (step_of(g0, g1) // per, 0, step_of(g0, g1) % per))
        out_shape = jax.ShapeDtypeStruct((r // SIDE_CHUNK, c, SIDE_CHUNK), bf16)
    else:
        out_spec = pl.BlockSpec((rows, c), lambda g0, g1: (step_of(g0, g1), 0))
        out_shape = jax.ShapeDtypeStruct((r, c), bf16)
    return in_spec, out_spec, out_shape


def _side_cast(side_ref, so_ref, transposed):
    x = side_ref[...]
    if transposed:
        so_ref[0] = x.T.astype(bf16)
    else:
        so_ref[...] = x.astype(bf16)


def _mm_kernel(a_ref, w_ref, *rest, epilogue, n_extra, side_transposed):
    extra = rest[:n_extra]
    rest = rest[n_extra:]
    if side_transposed is None:
        (o_ref,) = rest
    else:
        side_ref, o_ref, so_ref = rest
        _side_cast(side_ref, so_ref, side_transposed)
    acc = jnp.dot(a_ref[...], w_ref[...], preferred_element_type=f32)
    o_ref[...] = epilogue(acc, *extra).astype(o_ref.dtype)


def _matmul(a, w, out_dtype, *, tm, tn, epilogue=None, extra=(), extra_specs=(), name,
            side=None, side_transposed=False):
    m, k = a.shape
    n = w.shape[1]
    if epilogue is None:
        epilogue = lambda acc: acc
    grid = (m // tm, n // tn)
    in_specs = [pl.BlockSpec((tm, k), lambda i, j: (i, 0)),
                pl.BlockSpec((k, tn), lambda i, j: (0, j)),
                *extra_specs]
    out_specs = pl.BlockSpec((tm, tn), lambda i, j: (i, j))
    out_shape = jax.ShapeDtypeStruct((m, n), out_dtype)
    operands = [a, w, *extra]
    if side is not None:
        s_in, s_out, s_shape = _side_cast_specs(side, side_transposed, grid[0] * grid[1],
                                                lambda i, j: i * grid[1] + j)
        in_specs.append(s_in)
        out_specs, out_shape = [out_specs, s_out], [out_shape, s_shape]
        operands.append(side)
    return pl.pallas_call(
        functools.partial(_mm_kernel, epilogue=epilogue, n_extra=len(extra),
                          side_transposed=side_transposed if side is not None else None),
        grid=grid,
        in_specs=in_specs,
        out_specs=out_specs,
        out_shape=out_shape,
        compiler_params=_params("arbitrary", "arbitrary"),
        name=name,
    )(*operands)


def _mm_w32_kernel(a_ref, w_ref, *rest, epilogue, n_extra, side_transposed):
    extra = rest[:n_extra]
    rest = rest[n_extra:]
    if side_transposed is None:
        o_ref, wb_ref = rest
    else:
        side_ref, o_ref, so_ref, wb_ref = rest
        _side_cast(side_ref, so_ref, side_transposed)

    @pl.when(pl.program_id(1) == 0)
    def _():
        wb_ref[...] = w_ref[...].astype(bf16)

    acc = jnp.dot(a_ref[...], wb_ref[...], preferred_element_type=f32)
    o_ref[...] = epilogue(acc, *extra).astype(o_ref.dtype)


def _matmul_w32(a, w, out_dtype, *, tm, tn, epilogue=None, extra=(), extra_specs=(), name,
                side=None, side_transposed=False):
    m, k = a.shape
    n = w.shape[1]
    if epilogue is None:
        epilogue = lambda acc: acc
    grid = (n // tn, m // tm)
    in_specs = [pl.BlockSpec((tm, k), lambda j, i: (i, 0)),
                pl.BlockSpec((k, tn), lambda j, i: (0, j)),
                *extra_specs]
    out_specs = pl.BlockSpec((tm, tn), lambda j, i: (i, j))
    out_shape = jax.ShapeDtypeStruct((m, n), out_dtype)
    operands = [a, w, *extra]
    if side is not None:
        s_in, s_out, s_shape = _side_cast_specs(side, side_transposed, grid[0] * grid[1],
                                                lambda j, i: j * grid[1] + i)
        in_specs.append(s_in)
        out_specs, out_shape = [out_specs, s_out], [out_shape, s_shape]
        operands.append(side)
    return pl.pallas_call(
        functools.partial(_mm_w32_kernel, epilogue=epilogue, n_extra=len(extra),
                          side_transposed=side_transposed if side is not None else None),
        grid=grid,
        in_specs=in_specs,
        out_specs=out_specs,
        out_shape=out_shape,
        scratch_shapes=[pltpu.VMEM((k, tn), bf16)],
        compiler_params=_params("arbitrary", "arbitrary"),
        name=name,
    )(*operands)


def _gate_epilogue(acc, b_ref):
    return 0.5 * (1.0 + jnp.tanh(0.5 * (acc + b_ref[...])))


def _resid_epilogue(acc, x_ref, g_ref):
    return x_ref[...] + g_ref[0] * acc


def _rms(x, g):
    return x * lax.rsqrt(jnp.mean(x * x, axis=-1, keepdims=True) + EPS) * g


def _dsa_prep_kernel(p_ref, gcq_ref, gckv_ref, gki_ref, bki_ref, cq_o, ckv_o, ki_o, wi_o):
    cq_o[...] = _rms(p_ref[:, OFF_CQ:OFF_CQ + A_Q_RANK], gcq_ref[...]).astype(bf16)
    ckv_o[...] = _rms(p_ref[:, OFF_CKV:OFF_CKV + A_KV_RANK], gckv_ref[...]).astype(bf16)
    ki = p_ref[:, OFF_KI:OFF_KI + IDX_DIM]
    mu = jnp.mean(ki, axis=-1, keepdims=True)
    var = jnp.mean(jnp.square(ki - mu), axis=-1, keepdims=True)
    ki_o[...] = ((ki - mu) * lax.rsqrt(var + EPS) * gki_ref[...] + bki_ref[...]).astype(bf16)
    wi_o[...] = p_ref[:, OFF_WI:OFF_WI + 128] * (IDX_HEADS ** -0.5 * IDX_DIM ** -0.5)


def _dsa_prep(proj, g_cq, g_ckv, g_ki, b_ki):
    ts = 512
    row = lambda i: (i, 0)
    fixed = lambda i: (0, 0)
    return pl.pallas_call(
        _dsa_prep_kernel,
        grid=(TOKENS // ts,),
        in_specs=[pl.BlockSpec((ts, 2048), row),
                  pl.BlockSpec((1, A_Q_RANK), fixed),
                  pl.BlockSpec((1, A_KV_RANK), fixed),
                  pl.BlockSpec((1, IDX_DIM), fixed),
                  pl.BlockSpec((1, IDX_DIM), fixed)],
        out_specs=[pl.BlockSpec((ts, A_Q_RANK), row),
                   pl.BlockSpec((ts, A_KV_RANK), row),
                   pl.BlockSpec((ts, IDX_DIM), row),
                   pl.BlockSpec((ts, 128), row)],
        out_shape=[jax.ShapeDtypeStruct((TOKENS, A_Q_RANK), bf16),
                   jax.ShapeDtypeStruct((TOKENS, A_KV_RANK), bf16),
                   jax.ShapeDtypeStruct((TOKENS, IDX_DIM), bf16),
                   jax.ShapeDtypeStruct((TOKENS, 128), f32)],
        compiler_params=_params("arbitrary"),
        name="dsa_prep",
    )(proj, g_cq, g_ckv, g_ki, b_ki)


def _qlat_kernel(q_ref, w_ref, o_ref):
    for h in range(A_HEADS):
        r = jnp.dot(q_ref[:, h * A_HEAD_DIM:(h + 1) * A_HEAD_DIM], w_ref[h], preferred_element_type=f32)
        o_ref[:, h] = r.astype(o_ref.dtype).reshape(o_ref.shape[0], Q_BLOCK, A_KV_RANK)


def _qlat(qq, wuk_t):
    tm = 512
    q_blk = IDX_HEADS * IDX_DIM // A_WIDTH
    return pl.pallas_call(
        _qlat_kernel,
        grid=(TOKENS // tm,),
        in_specs=[pl.BlockSpec((tm, A_WIDTH), lambda i: (i, q_blk)),
                  pl.BlockSpec((A_HEADS, A_HEAD_DIM, A_KV_RANK), lambda i: (0, 0, 0))],
        out_specs=pl.BlockSpec((tm // Q_BLOCK, A_HEADS, Q_BLOCK, A_KV_RANK), lambda i: (i, 0, 0, 0)),
        out_shape=jax.ShapeDtypeStruct((TOKENS // Q_BLOCK, A_HEADS, Q_BLOCK, A_KV_RANK), bf16),
        compiler_params=_params("arbitrary"),
        name="q_lat",
    )(qq, wuk_t)


def _t5_kernel(t5_ref, o_ref):
    h = pl.program_id(0)
    half = T5_BUCKETS // 2
    exact = half // 2
    qi = lax.broadcasted_iota(i32, (Q_BLOCK, NEAR), 0)
    kj = lax.broadcasted_iota(i32, (Q_BLOCK, NEAR), 1)
    rel = kj - NEAR_BACK - qi
    n = jnp.abs(rel)
    log_ratio = jnp.log(jnp.maximum(n, 1).astype(f32) / exact) / math.log(T5_MAX_DIST / exact)
    large = jnp.minimum(exact + (log_ratio * (half - exact)).astype(i32), half - 1)
    bucket = jnp.where(rel > 0, half, 0) + jnp.where(n < exact, n, large)
    acc = jnp.zeros((Q_BLOCK, NEAR), f32)
    for k in range(T5_BUCKETS):
        acc = jnp.where(bucket == k, t5_ref[k, h], acc)
    o_ref[0] = (acc - t5_ref[half - 1, h]) * (1.0 / ATTN_SCALE)


def _t5_table(t5_bias):
    return pl.pallas_call(
        _t5_kernel,
        grid=(A_HEADS,),
        in_specs=[pl.BlockSpec(memory_space=pltpu.SMEM)],
        out_specs=pl.BlockSpec((1, Q_BLOCK, NEAR), lambda h: (h, 0, 0)),
        out_shape=jax.ShapeDtypeStruct((A_HEADS, Q_BLOCK, NEAR), f32),
        compiler_params=_params("arbitrary"),
        name="t5_table",
    )(t5_bias)


DSA_COL_STEP = 512
DSA_HEAD_GROUP = 8
DSA_ROWS = DSA_HEAD_GROUP * Q_BLOCK


def _dsa_block(width, qb, t5_ref, qi_ref, kit_ref, wi_ref, ql_ref, kv_ref, wuv_ref, pt_ref, o_ref,
               key_ref, mask_ref, lg_ref, e_ref, l_ref):
    q0 = qb * Q_BLOCK
    ct_w = 256
    w = wi_ref[...]
    score_ref = lg_ref.at[0:Q_BLOCK]

    def col_tile(ct, carry):
        c0 = pl.multiple_of(ct * ct_w, ct_w)
        kt = kit_ref[:, pl.ds(c0, ct_w)]
        acc = jnp.zeros((Q_BLOCK, ct_w), f32)
        for h in range(IDX_HEADS):
            x = jnp.dot(qi_ref[:, h * IDX_DIM:(h + 1) * IDX_DIM], kt, preferred_element_type=f32)
            acc = acc + jnp.maximum(x, 0.0) * w[:, h:h + 1]
        score_ref[:, pl.ds(c0, ct_w)] = acc
        return carry

    lax.fori_loop(0, width // ct_w, col_tile, 0)

    row = lax.broadcasted_iota(i32, (Q_BLOCK, width), 0)
    col = lax.broadcasted_iota(i32, (Q_BLOCK, width), 1)
    adm = (col // CHUNK) <= ((q0 + row) // CHUNK)
    score = jnp.where(adm, score_ref[:, 0:width], -jnp.inf)
    bits = pltpu.bitcast(score, i32)
    key_ref[:, 0:width] = jnp.where(bits < 0, bits ^ 0x7FFFFFFF, bits)

    def bisect(b, thr_u):
        cand_u = thr_u | jnp.left_shift(jnp.int32(1), 31 - b)
        hit = jnp.where(key_ref[:, 0:width] >= (cand_u ^ INT_MIN), 1.0, 0.0)
        cnt = jnp.sum(hit, axis=1, keepdims=True)
        return jnp.where(cnt >= TOPK, cand_u, thr_u)

    thr_u = lax.fori_loop(0, 32, bisect, jnp.zeros((Q_BLOCK, 1), i32))
    sel = (key_ref[:, 0:width] >= (thr_u ^ INT_MIN)) & adm
    mask_ref[:, 0:width] = jnp.where(sel, 0.0, -jnp.inf)

    far_bucket = T5_BUCKETS // 2 - 1

    kv = kv_ref[0:width, :]

    def head_group(g, carry):
        h0 = g * DSA_HEAD_GROUP
        qg = ql_ref[0, pl.ds(h0, DSA_HEAD_GROUP)].reshape(DSA_ROWS, A_KV_RANK)
        lg_ref[:, 0:width] = lax.dot_general(qg, kv, (((1,), (1,)), ((), ())), preferred_element_type=f32)

        @pl.when(qb == 0)
        def _():
            for hh in range(DSA_HEAD_GROUP):
                lg_ref[hh * Q_BLOCK:(hh + 1) * Q_BLOCK, 0:Q_BLOCK] += pt_ref[h0 + hh, :, NEAR_BACK:NEAR]

        @pl.when(qb > 0)
        def _():
            w0 = pl.multiple_of(q0 - NEAR_BACK, 128)
            for hh in range(DSA_HEAD_GROUP):
                lg_ref[hh * Q_BLOCK:(hh + 1) * Q_BLOCK, pl.ds(w0, NEAR)] += pt_ref[h0 + hh]

        for hh in range(DSA_HEAD_GROUP):
            rs = slice(hh * Q_BLOCK, (hh + 1) * Q_BLOCK)
            x = lg_ref[rs, 0:width] * (ATTN_SCALE * LOG2E) + (
                mask_ref[:, 0:width] + t5_ref[far_bucket, h0 + hh] * LOG2E)
            m = jnp.max(x, axis=1, keepdims=True)
            e = jnp.exp2(x - m)
            l_ref[rs, :] = jnp.sum(e, axis=1, keepdims=True)
            e_ref[rs, 0:width] = e.astype(bf16)

        ol = jnp.dot(e_ref[:, 0:width], kv, preferred_element_type=f32) / l_ref[...]
        for hh in range(DSA_HEAD_GROUP):
            o = jnp.dot(ol[hh * Q_BLOCK:(hh + 1) * Q_BLOCK].astype(bf16), wuv_ref[h0 + hh],
                        preferred_element_type=f32)
            c0 = pl.multiple_of((h0 + hh) * A_HEAD_DIM, A_HEAD_DIM)
            o_ref[:, pl.ds(c0, A_HEAD_DIM)] = o.astype(o_ref.dtype)
        return carry

    lax.fori_loop(0, A_HEADS // DSA_HEAD_GROUP, head_group, 0)


def _dsa_kernel(t5_ref, qi_ref, kit_ref, wi_ref, ql_ref, kv_ref, wuv_ref, pt_ref, o_ref,
                key_ref, mask_ref, lg_ref, e_ref, l_ref):
    qb = pl.program_id(1)
    blocks_per_step = DSA_COL_STEP // Q_BLOCK
    for n in range(SEQ // DSA_COL_STEP):
        pl.when(qb // blocks_per_step == n)(functools.partial(
            _dsa_block, (n + 1) * DSA_COL_STEP, qb, t5_ref, qi_ref, kit_ref, wi_ref, ql_ref, kv_ref,
            wuv_ref, pt_ref, o_ref, key_ref, mask_ref, lg_ref, e_ref, l_ref))


def _dsa(t5_bias, qq, kidx_t, widx, qlat, ckv, wuv, ptab):
    nqb = SEQ // Q_BLOCK
    return pl.pallas_call(
        _dsa_kernel,
        grid=(BATCH, nqb),
        in_specs=[pl.BlockSpec(memory_space=pltpu.SMEM),
                  pl.BlockSpec((Q_BLOCK, IDX_HEADS * IDX_DIM), lambda b, i: (b * nqb + i, 0)),
                  pl.BlockSpec((IDX_DIM, SEQ), lambda b, i: (0, b)),
                  pl.BlockSpec((Q_BLOCK, 128), lambda b, i: (b * nqb + i, 0)),
                  pl.BlockSpec((1, A_HEADS, Q_BLOCK, A_KV_RANK), lambda b, i: (b * nqb + i, 0, 0, 0)),
                  pl.BlockSpec((SEQ, A_KV_RANK), lambda b, i: (b, 0)),
                  pl.BlockSpec((A_HEADS, A_KV_RANK, A_HEAD_DIM), lambda b, i: (0, 0, 0)),
                  pl.BlockSpec((A_HEADS, Q_BLOCK, NEAR), lambda b, i: (0, 0, 0))],
        out_specs=pl.BlockSpec((Q_BLOCK, A_WIDTH), lambda b, i: (b * nqb + i, 0)),
        out_shape=jax.ShapeDtypeStruct((TOKENS, A_WIDTH), bf16),
        scratch_shapes=[pltpu.VMEM((Q_BLOCK, SEQ), i32),
                        pltpu.VMEM((Q_BLOCK, SEQ), f32),
                        pltpu.VMEM((DSA_ROWS, SEQ), f32),
                        pltpu.VMEM((DSA_ROWS, SEQ), bf16),
                        pltpu.VMEM((DSA_ROWS, 1), f32)],
        compiler_params=_params("arbitrary", "arbitrary"),
        name="dsa_main",
    )(t5_bias, qq, kidx_t, widx, qlat, ckv, wuv, ptab)


def _ret_kernel(q_ref, k_ref, v_ref, g_ref, cc_ref, ss_ref, dm_ref, qd_ref, kd_ref, cd_ref, gr_ref,
                o_ref, qs_ref, ks_ref):
    cc = cc_ref[...]
    ss = ss_ref[...]
    q = q_ref[...]
    k = k_ref[...] * (R_QK_DIM ** -0.5)
    half = R_QK_DIM // 2
    qs_ref[...] = q * cc + pltpu.roll(q, half, 1) * ss
    ks_ref[...] = k * cc + pltpu.roll(k, half, 1) * ss
    dm = dm_ref[0]
    qd = qd_ref[0]
    kd = kd_ref[0]
    cd = cd_ref[0]
    gr = gr_ref[0]

    def chunk(n, state):
        r0 = pl.multiple_of(n * CHUNK, CHUNK)
        qn = qs_ref[pl.ds(r0, CHUNK), :]
        kn = ks_ref[pl.ds(r0, CHUNK), :]
        vn = v_ref[pl.ds(r0, CHUNK), :].astype(bf16)
        att = lax.dot_general(qn.astype(bf16), kn.astype(bf16), (((1,), (1,)), ((), ())),
                              preferred_element_type=f32) * dm
        y = jnp.dot(att.astype(bf16), vn, preferred_element_type=f32)
        y = y + jnp.dot((qn * qd).astype(bf16), state.astype(bf16), preferred_element_type=f32)
        kv = lax.dot_general((kn * kd).astype(bf16), vn, (((0,), (0,)), ((), ())),
                             preferred_element_type=f32)
        y = y * lax.rsqrt(jnp.mean(y * y, axis=-1, keepdims=True) + EPS) * gr
        gate = g_ref[pl.ds(r0, CHUNK), :]
        o_ref[pl.ds(r0, CHUNK), :] = (gate * jax.nn.sigmoid(gate) * y).astype(o_ref.dtype)
        return state * cd + kv

    lax.fori_loop(0, SEQ // CHUNK, chunk, jnp.zeros((R_QK_DIM, R_V_DIM), f32), unroll=16)


def _retention(proj, cc, ss, dmask, qdec, kdec, cdec, g_ret):
    qk_blk = OFF_RQ // R_QK_DIM
    k_blk = OFF_RK // R_QK_DIM
    v_blk = OFF_RV // R_V_DIM
    g_blk = OFF_RG // R_V_DIM
    per_head = lambda b, h: (h, 0, 0)
    return pl.pallas_call(
        _ret_kernel,
        grid=(BATCH, R_HEADS),
        in_specs=[pl.BlockSpec((SEQ, R_QK_DIM), lambda b, h: (b, qk_blk + h)),
                  pl.BlockSpec((SEQ, R_QK_DIM), lambda b, h: (b, k_blk + h)),
                  pl.BlockSpec((SEQ, R_V_DIM), lambda b, h: (b, v_blk + h)),
                  pl.BlockSpec((SEQ, R_V_DIM), lambda b, h: (b, g_blk + h)),
                  pl.BlockSpec((SEQ, R_QK_DIM), lambda b, h: (0, 0)),
                  pl.BlockSpec((SEQ, R_QK_DIM), lambda b, h: (0, 0)),
                  pl.BlockSpec((1, CHUNK, CHUNK), per_head),
                  pl.BlockSpec((1, CHUNK, R_QK_DIM), per_head),
                  pl.BlockSpec((1, CHUNK, R_QK_DIM), per_head),
                  pl.BlockSpec((1, 1, R_V_DIM), per_head),
                  pl.BlockSpec((1, 1, R_V_DIM), per_head)],
        out_specs=pl.BlockSpec((SEQ, R_V_DIM), lambda b, h: (b, h)),
        out_shape=jax.ShapeDtypeStruct((TOKENS, R_WIDTH), bf16),
        scratch_shapes=[pltpu.VMEM((SEQ, R_QK_DIM), f32), pltpu.VMEM((SEQ, R_QK_DIM), f32)],
        compiler_params=_params("arbitrary", "arbitrary"),
        name="retention",
    )(proj, proj, proj, proj, cc, ss, dmask, qdec, kdec, cdec, g_ret)


def _merge_kernel(ya_ref, yr_ref, wa_ref, wr_ref, ga_ref, gr_ref, o_ref, wab_ref, wrb_ref):
    @pl.when(pl.program_id(1) == 0)
    def _():
        wab_ref[...] = wa_ref[...].astype(bf16)
        wrb_ref[...] = wr_ref[...].astype(bf16)

    pa = jnp.dot(ya_ref[...], wab_ref[...], preferred_element_type=f32)
    pr = jnp.dot(yr_ref[...], wrb_ref[...], preferred_element_type=f32)
    o_ref[...] = (ga_ref[...].astype(f32) * pa + gr_ref[...].astype(f32) * pr).astype(o_ref.dtype)


def _merge(ya, yr, w_up, gates):
    assert A_WIDTH == R_WIDTH
    tm, tn = 1024, 512
    nj = D_MODEL // tn
    return pl.pallas_call(
        _merge_kernel,
        grid=(nj, TOKENS // tm),
        in_specs=[pl.BlockSpec((tm, A_WIDTH), lambda j, i: (i, 0)),
                  pl.BlockSpec((tm, R_WIDTH), lambda j, i: (i, 0)),
                  pl.BlockSpec((A_WIDTH, tn), lambda j, i: (0, j)),
                  pl.BlockSpec((R_WIDTH, tn), lambda j, i: (1, j)),
                  pl.BlockSpec((tm, tn), lambda j, i: (i, j)),
                  pl.BlockSpec((tm, tn), lambda j, i: (i, nj + j))],
        out_specs=pl.BlockSpec((tm, tn), lambda j, i: (i, j)),
        out_shape=jax.ShapeDtypeStruct((TOKENS, D_MODEL), bf16),
        scratch_shapes=[pltpu.VMEM((A_WIDTH, tn), bf16), pltpu.VMEM((R_WIDTH, tn), bf16)],
        compiler_params=_params("arbitrary", "arbitrary"),
        name="merge_up",
    )(ya, yr, w_up, w_up, gates, gates)


ROUTE_CAND_GROUPS = ((0, 16), (1, 8), (2, 8), (3, 8), (4, 8), (5, 8), (6, 8), (7, 8))
ROUTE_CAND_ROWS = sum(n for _, n in ROUTE_CAND_GROUPS) + 8


def _peer_route_kernel(q_ref, k_ref, s1_o, s2_o, a1_o, m2_o, thr_o, v1_ref, v2_ref, cand_ref, ec_ref):
    hq = P_QUERY_DIM // 2
    s1_all = jnp.dot(k_ref[0], q_ref[0:hq, :].astype(bf16), preferred_element_type=f32)
    s2_all = jnp.dot(k_ref[1], q_ref[hq:P_QUERY_DIM, :].astype(bf16), preferred_element_type=f32)

    def top_values(s, v_ref):
        cur = s
        for r in range(P_TOPK):
            m = jnp.max(cur, axis=0, keepdims=True)
            v_ref[r:r + 1, :] = m
            cur = jnp.where(cur == m, -jnp.inf, cur)

    for lt in range(s1_all.shape[1] // 128):
        ls = slice(lt * 128, (lt + 1) * 128)
        s1 = s1_all[:, ls]
        s2 = s2_all[:, ls]
        top_values(s1, v1_ref)
        top_values(s2, v2_ref)
        v1 = v1_ref[...]
        v2 = v2_ref[...]
        m1 = v1[0:1]
        m2 = v2[0:1]
        e1 = jnp.exp(v1 - m1)
        e2 = jnp.exp(v2 - m2)
        off = 0
        for r1, n in ROUTE_CAND_GROUPS:
            cand_ref[off:off + n, :] = v1[r1:r1 + 1] + v2[0:n]
            ec_ref[off:off + n, :] = e1[r1:r1 + 1] * e2[0:n]
            off += n
        cand_ref[off:off + 8, :] = v1[8:16] + v2[0:1]
        ec_ref[off:off + 8, :] = e1[8:16] * e2[0:1]
        cand = cand_ref[...]
        cur = cand
        thr = None
        for r in range(P_TOPK):
            thr = jnp.max(cur, axis=0, keepdims=True)
            cur = jnp.where(cur == thr, -jnp.inf, cur)
        z = jnp.sum(jnp.where(cand >= thr, ec_ref[...], 0.0), axis=0, keepdims=True)
        a1 = 0.5 * jnp.exp(s1 - m1) / z
        for r in range(PEER_NC):
            s1_o[0, r, :, ls] = s1[r * PEER_ROWS:(r + 1) * PEER_ROWS]
            a1_o[0, r, :, ls] = a1[r * PEER_ROWS:(r + 1) * PEER_ROWS]
        s2_o[0, lt] = s2
        m2_o[0, :, ls] = m2
        thr_o[0, :, ls] = thr


def _peer_route(q_t, keys):
    tl = 512
    big = lambda h, j: (h, 0, j)
    chunked = lambda h, j: (h, 0, 0, j)
    chunked_shape = jax.ShapeDtypeStruct((P_HEADS, PEER_NC, PEER_ROWS, TOKENS), f32)
    row_shape = jax.ShapeDtypeStruct((P_HEADS, 1, TOKENS), f32)
    return pl.pallas_call(
        _peer_route_kernel,
        grid=(P_HEADS, TOKENS // tl),
        in_specs=[pl.BlockSpec((P_QUERY_DIM, tl), lambda h, j: (h, j)),
                  pl.BlockSpec((2, P_NKEYS, P_QUERY_DIM // 2), lambda h, j: (0, 0, 0))],
        out_specs=[pl.BlockSpec((1, PEER_NC, PEER_ROWS, tl), chunked),
                   pl.BlockSpec((1, tl // 128, P_NKEYS, 128), lambda h, j: (h, j, 0, 0)),
                   pl.BlockSpec((1, PEER_NC, PEER_ROWS, tl), chunked),
                   pl.BlockSpec((1, 1, tl), big),
                   pl.BlockSpec((1, 1, tl), big)],
        out_shape=[chunked_shape, jax.ShapeDtypeStruct((P_HEADS, TOKENS // 128, P_NKEYS, 128), f32),
                   chunked_shape, row_shape, row_shape],
        scratch_shapes=[pltpu.VMEM((P_TOPK, 128), f32), pltpu.VMEM((P_TOPK, 128), f32),
                        pltpu.VMEM((ROUTE_CAND_ROWS, 128), f32), pltpu.VMEM((ROUTE_CAND_ROWS, 128), f32)],
        compiler_params=_params("arbitrary", "arbitrary"),
        name="peer_route",
    )(q_t, keys)


PEER_TM = 512
PEER_TE = 512
PEER_NC = P_EXPERTS // PEER_TE
PEER_ROWS = PEER_TE // P_NKEYS
PEER_KT = 32
PEER_KPIECES = 8


def _peer_kernel(ht_ref, u_ref, vt_ref, s1_ref, a1_ref, s2_ref, m2_ref, thr_ref, o_ref,
                 b2_ref, pre_ref, w_ref):
    c = pl.program_id(1)

    @pl.when(c == 0)
    def _():
        o_ref[...] = jnp.zeros_like(o_ref)
        for lt in range(PEER_TM // 128):
            b2_ref[:, lt] = jnp.exp(s2_ref[:, lt] - m2_ref[:, :, lt * 128:(lt + 1) * 128])

    @pl.when(c > 0)
    def _():
        pre = pre_ref[...]
        act2 = pre * (1.0 + lax.erf(pre * np.float32(np.sqrt(0.5))))
        coef = (w_ref[0:PEER_TE, :] * act2).astype(bf16)
        o_ref[...] += jnp.dot(vt_ref[0], coef, preferred_element_type=f32)

    @pl.when(c < PEER_NC)
    def _():
        z = pl.multiple_of(jnp.minimum(c, 0), PEER_TE)
        units = [(j, lt, kt) for j in range(PEER_ROWS) for lt in range(PEER_TM // 128)
                 for kt in range(P_NKEYS // PEER_KT)]
        per_piece = len(units) // PEER_KPIECES
        kw = D_MODEL // PEER_KPIECES
        for kq in range(PEER_KPIECES):
            part = jnp.dot(u_ref[:, kq * kw:(kq + 1) * kw], ht_ref[kq * kw:(kq + 1) * kw, :],
                           preferred_element_type=f32)
            if kq == 0:
                pre_ref[...] = part
            else:
                pre_ref[...] += part
            w_ref[pl.ds(z + PEER_TE, 8), 0:128] = part[PEER_TE - 8:PEER_TE, PEER_TM - 128:PEER_TM]
            for j, lt, kt in units[kq * per_piece:(kq + 1) * per_piece]:
                ls = slice(lt * 128, (lt + 1) * 128)
                ks = slice(kt * PEER_KT, (kt + 1) * PEER_KT)
                wj = jnp.zeros((PEER_KT, 128), f32)
                for h in range(P_HEADS):
                    sel = (s1_ref[h, 0, j:j + 1, ls] + s2_ref[h, lt, ks, :]) >= thr_ref[h, :, ls]
                    wj = wj + jnp.where(sel, b2_ref[h, lt, ks, :], 0.0) * a1_ref[h, 0, j:j + 1, ls]
                w0 = pl.multiple_of(z + (j * P_NKEYS + kt * PEER_KT), PEER_KT)
                w_ref[pl.ds(w0, PEER_KT), ls] = wj


def _peer(h_t, u, v_t, s1, a1, s2, m2, thr):
    tm, te, nc = PEER_TM, PEER_TE, PEER_NC
    s1r, a1r = s1, a1
    assert v_t.shape == (nc, D_MODEL, te)
    tok3 = lambda i, c: (0, 0, i)
    cur = lambda i, c: (0, jnp.minimum(c, nc - 1), 0, i)
    return pl.pallas_call(
        _peer_kernel,
        grid=(TOKENS // tm, nc + 1),
        in_specs=[pl.BlockSpec((D_MODEL, tm), lambda i, c: (0, i)),
                  pl.BlockSpec((te, D_MODEL), lambda i, c: (jnp.minimum(c, nc - 1), 0)),
                  pl.BlockSpec((1, D_MODEL, te), lambda i, c: (jnp.maximum(c - 1, 0), 0, 0)),
                  pl.BlockSpec((P_HEADS, 1, PEER_ROWS, tm), cur),
                  pl.BlockSpec((P_HEADS, 1, PEER_ROWS, tm), cur),
                  pl.BlockSpec((P_HEADS, tm // 128, P_NKEYS, 128), lambda i, c: (0, i, 0, 0)),
                  pl.BlockSpec((P_HEADS, 1, tm), tok3),
                  pl.BlockSpec((P_HEADS, 1, tm), tok3)],
        out_specs=pl.BlockSpec((D_MODEL, tm), lambda i, c: (0, i)),
        out_shape=jax.ShapeDtypeStruct((D_MODEL, TOKENS), f32),
        scratch_shapes=[pltpu.VMEM((P_HEADS, tm // 128, P_NKEYS, 128), f32),
                        pltpu.VMEM((te, tm), f32), pltpu.VMEM((te + 8, tm), f32)],
        compiler_params=_params("arbitrary", "arbitrary"),
        name="peer_experts",
    )(h_t, u, v_t, s1r, a1r, s2, m2, thr)


def _final_kernel(x_ref, yt_ref, gt_ref, g_ref, o_ref):
    x = x_ref[...] + gt_ref[0] * yt_ref[...].T
    o_ref[...] = x * lax.rsqrt(jnp.mean(x * x, axis=-1, keepdims=True) + EPS) * g_ref[...]


def _final(x1, y_t, modr, g_final):
    ts = 256
    per_b = SEQ // ts
    return pl.pallas_call(
        _final_kernel,
        grid=(TOKENS // ts,),
        in_specs=[pl.BlockSpec((ts, D_MODEL), lambda i: (i, 0)),
                  pl.BlockSpec((D_MODEL, ts), lambda i: (0, i)),
                  pl.BlockSpec((1, 1, D_MODEL), lambda i: ((i // per_b) * N_MOD + 5, 0, 0)),
                  pl.BlockSpec((1, D_MODEL), lambda i: (0, 0))],
        out_specs=pl.BlockSpec((ts, D_MODEL), lambda i: (i, 0)),
        out_shape=jax.ShapeDtypeStruct((TOKENS, D_MODEL), f32),
        compiler_params=_params("arbitrary"),
        name="final_norm",
    )(x1, y_t, modr, g_final)


def _retention_tables():
    half = R_QK_DIM // 2
    inv = 1.0 / (ROT_BASE ** jnp.linspace(0.0, 1.0, half, dtype=f32))
    ang = jnp.arange(SEQ, dtype=f32)[:, None] * inv[None, :]
    cos, sin = jnp.cos(ang), jnp.sin(ang)
    cc = jnp.concatenate([cos, cos], axis=-1)
    ss = jnp.concatenate([-sin, sin], axis=-1)
    log_g = jnp.log(1.0 - jnp.power(2.0, -5.0 - jnp.arange(R_HEADS, dtype=f32)))
    j = jnp.arange(CHUNK, dtype=f32)
    diff = j[:, None] - j[None, :]
    dmask = jnp.where(diff[None] >= 0, jnp.exp(jnp.maximum(diff, 0.0)[None] * log_g[:, None, None]), 0.0)
    kdec = jnp.exp((CHUNK - 1.0 - j)[None, :] * log_g[:, None])
    qdec = jnp.exp((j + 1.0)[None, :] * log_g[:, None])
    cdec = jnp.exp(CHUNK * log_g)
    kdec = jnp.broadcast_to(kdec[:, :, None], (R_HEADS, CHUNK, R_QK_DIM))
    qdec = jnp.broadcast_to(qdec[:, :, None], (R_HEADS, CHUNK, R_QK_DIM))
    cdec = jnp.broadcast_to(cdec[:, None, None], (R_HEADS, 1, R_V_DIM))
    return cc, ss, dmask, qdec, kdec, cdec


def kernel(x, c, w_ada, b_ada, g_mix, w_in, g_cq, g_ckv, w_uq, w_uk, w_uv, w_qi, g_ki, b_ki, t5_bias, g_ret,
           w_up, w_gate, b_gate, w_out, g_ffn, w_pq, sub_keys, u_exp, v_exp, g_final):
    x2d = x.reshape(TOKENS, D_MODEL)

    w_in_p = _w_in_layout(w_in[0].T)
    w_q_all = jnp.concatenate(
        [w_qi[0].reshape(A_Q_RANK, IDX_HEADS * IDX_DIM), w_uq[0].reshape(A_Q_RANK, A_WIDTH)], axis=1).astype(bf16)
    wuk_t = jnp.transpose(w_uk[0], (1, 2, 0)).astype(bf16)
    wuv_h = jnp.transpose(w_uv[0], (1, 0, 2)).astype(bf16)
    w_pq_t = w_pq[0].T.astype(bf16)
    keys_b = sub_keys[0].astype(bf16)

    c8 = jnp.pad(c, ((0, 8 - BATCH), (0, 0)))
    mod = _ada(c8, w_ada[0], b_ada[0].reshape(1, N_MOD * D_MODEL))[:BATCH]
    modr = mod.reshape(BATCH * N_MOD, 1, D_MODEL)

    h = _normmod(x2d, g_mix[0].reshape(1, D_MODEL), modr, 0, 1)
    proj, v_t = _matmul(h, w_in_p, f32, tm=1024, tn=512, name="in_proj",
                        side=v_exp[0], side_transposed=True)
    gates, u_b = _matmul_w32(h, w_gate[0], bf16, tm=1024, tn=512, epilogue=_gate_epilogue,
                             extra=(b_gate[0].reshape(1, 2 * D_MODEL),),
                             extra_specs=(pl.BlockSpec((1, 512), lambda j, i: (0, j)),), name="gates",
                             side=u_exp[0])

    cqn, ckvn, kidx, widx = _dsa_prep(proj, g_cq[0].reshape(1, -1), g_ckv[0].reshape(1, -1),
                                      g_ki[0].reshape(1, -1), b_ki[0].reshape(1, -1))
    qq = _matmul(cqn, w_q_all, bf16, tm=2048, tn=1024, name="q_up")
    qlat = _qlat(qq, wuk_t)
    ptab = _t5_table(t5_bias)
    y_a = _dsa(t5_bias, qq, kidx.T, widx, qlat, ckvn, wuv_h, ptab)

    cc, ss, dmask, qdec, kdec, cdec = _retention_tables()
    y_r = _retention(proj, cc, ss, dmask, qdec, kdec, cdec, g_ret[0].reshape(R_HEADS, 1, R_V_DIM))

    merged = _merge(y_a, y_r, w_up[0], gates)
    per_b = SEQ // 1024
    x1 = _matmul_w32(merged, w_out[0], f32, tm=1024, tn=512, epilogue=_resid_epilogue,
                     extra=(x2d, modr),
                     extra_specs=(pl.BlockSpec((1024, 512), lambda j, i: (i, j)),
                                  pl.BlockSpec((1, 1, 512), lambda j, i: ((i // per_b) * N_MOD + 2, 0, j))),
                     name="out_proj")

    h2_t = _normmod(x1, g_ffn[0].reshape(1, D_MODEL), modr, 3, 4, transposed=True)
    q_t = _matmul(w_pq_t, h2_t, f32, tm=1024, tn=512, name="peer_q")
    s1, s2, a1, m2, thr = _peer_route(q_t, keys_b)
    y_t = _peer(h2_t, u_b, v_t, s1, a1, s2, m2, thr)
    out = _final(x1, y_t, modr, g_final.reshape(1, D_MODEL))
    return out.reshape(BATCH, SEQ, D_MODEL)
```
